```python
import jax
import jax.numpy as jnp
from jax import lax
import numpy as np

D_MODEL = 2048
BATCH = 4
SEQ = 4096
DEPTH = 4

GRID_W = 64
CTX_LEN = 256
NORM_EPS = 1e-6
NEG_INF = -1e30
N_MOD = 6

NA_HEADS = 8
NA_HEAD_DIM = 128
NA_WIDTH = NA_HEADS * NA_HEAD_DIM
NA_WIN_H = 8
NA_WIN_W = 16

LRU_WIDTH = 1024
LRU_BLOCKS = 8
LRU_BLOCK_DIM = LRU_WIDTH // LRU_BLOCKS
LRU_CONV_W = 4
LRU_C = 8.0

MLA_HEADS = 8
MLA_Q_RANK = 512
MLA_KV_RANK = 256
MLA_NOPE_DIM = 128
MLA_ROPE_DIM = 64
MLA_V_DIM = 128
MLA_QK_DIM = MLA_NOPE_DIM + MLA_ROPE_DIM
MLA_WIDTH = MLA_HEADS * MLA_V_DIM
ROPE_THETA = 10000.0
Q_BLOCK = 128

N_BRANCH = 3
BRANCH_WIDTH = 1024
D_IN = 3 * NA_WIDTH + 2 * LRU_WIDTH + MLA_Q_RANK + MLA_KV_RANK + MLA_ROPE_DIM + N_BRANCH * D_MODEL

D_FF = 5632
FFN_CONV_W = 3

kernel_name = 'hybrid_natten_rglru_mla_prefix_dit_block'


def rmsnorm(x, g):
    xf = x.astype(jnp.float32)
    y = xf * lax.rsqrt(jnp.mean(xf * xf, axis=-1, keepdims=True) + NORM_EPS)
    return (y * g.astype(jnp.float32)).astype(x.dtype)


def modulate(h, shift, scale):
    return h * (1 + scale) + shift


def depthwise_conv(x, w, b):
    width = w.shape[0]
    n = x.shape[1]
    pad_left = width // 2
    xp = jnp.pad(x, ((0, 0), (pad_left, width - 1 - pad_left), (0, 0)))
    return sum(xp[:, i:i + n] * w[i] for i in range(width)) + b


def axial_rope_angles(n_tok, dim):
    t = jnp.arange(n_tok, dtype=jnp.int32)
    row = (t // GRID_W).astype(jnp.float32)
    col = (t % GRID_W).astype(jnp.float32)
    n_freq = dim // 4
    inv_freq = ROPE_THETA ** (-jnp.arange(n_freq, dtype=jnp.float32) / n_freq)
    ang = jnp.concatenate([row[:, None] * inv_freq, col[:, None] * inv_freq], axis=-1)
    return jnp.cos(ang), jnp.sin(ang)


def apply_rope(x, cos, sin):
    half = x.shape[-1] // 2
    xf = x.astype(jnp.float32)
    x1, x2 = xf[..., :half], xf[..., half:]
    return jnp.concatenate([x1 * cos - x2 * sin, x1 * sin + x2 * cos], axis=-1).astype(x.dtype)


def split_in_proj(z):
    sizes = (3 * NA_WIDTH, LRU_WIDTH, LRU_WIDTH, MLA_Q_RANK, MLA_KV_RANK, MLA_ROPE_DIM)
    offsets = [int(o) for o in np.cumsum(sizes)]
    return jnp.split(z, offsets, axis=-1)


def softmax_attention(q, k, v):
    scale = q.shape[-1] ** -0.5
    s = jnp.einsum('bqhd,bkhd->bhqk', q, k).astype(jnp.float32) * scale
    p = jax.nn.softmax(s, axis=-1).astype(v.dtype)
    o = jnp.einsum('bhqk,bkhd->bqhd', p, v)
    return o.reshape(o.shape[0], o.shape[1], -1)


def neighbourhood_attention(q, k, v, k_ctx, v_ctx, rpb):
    bsz, n, heads, hd = q.shape
    rows = n // GRID_W
    kh = min(NA_WIN_H, rows)
    r = np.arange(rows)
    key_rows = np.clip(r - kh // 2, 0, rows - kh)[:, None] + np.arange(kh)[None, :]
    row_idx = key_rows - r[:, None] + (NA_WIN_H - 1)
    cidx = np.arange(GRID_W)
    c_start = np.clip(cidx - NA_WIN_W // 2, 0, GRID_W - NA_WIN_W)
    in_win = (cidx[None, :] >= c_start[:, None]) & (cidx[None, :] < c_start[:, None] + NA_WIN_W)
    col_idx = np.clip(cidx[None, :] - cidx[:, None], -(NA_WIN_W - 1), NA_WIN_W - 1) + (NA_WIN_W - 1)
    bias = rpb.astype(jnp.float32)[:, row_idx][..., col_idx]
    bias = jnp.where(in_win[None, None, :, None, :], bias.transpose(0, 1, 3, 2, 4), NEG_INF)
    scale = hd ** -0.5
    qg = q.reshape(bsz, rows, GRID_W, heads, hd)
    kg = jnp.take(k.reshape(bsz, rows, GRID_W, heads, hd), key_rows, axis=1)
    vg = jnp.take(v.reshape(bsz, rows, GRID_W, heads, hd), key_rows, axis=1)
    s_win = jnp.einsum('brqhd,brjkhd->bhrqjk', qg, kg).astype(jnp.float32) * scale + bias
    s_ctx = jnp.einsum('brqhd,blhd->bhrql', qg, k_ctx).astype(jnp.float32) * scale
    n_win = kh * GRID_W
    s = jnp.concatenate([s_win.reshape(bsz, heads, rows, GRID_W, n_win), s_ctx], axis=-1)
    p = jax.nn.softmax(s, axis=-1).astype(v.dtype)
    p_win = p[..., :n_win].reshape(bsz, heads, rows, GRID_W, kh, GRID_W)
    o = jnp.einsum('bhrqjk,brjkhd->brqhd', p_win, vg) + jnp.einsum('bhrql,blhd->brqhd', p[..., n_win:], v_ctx)
    return o.reshape(bsz, n, heads * hd)


def rglru_coeffs(x, w_a, b_a, w_x, b_x, lam):
    bsz, n, _ = x.shape
    xb = x.reshape(bsz, n, LRU_BLOCKS, LRU_BLOCK_DIM)
    gate_a = jnp.einsum('bnkc,kcd->bnkd', xb, w_a).reshape(bsz, n, LRU_WIDTH) + b_a
    gate_x = jnp.einsum('bnkc,kcd->bnkd', xb, w_x).reshape(bsz, n, LRU_WIDTH) + b_x
    r = jax.nn.sigmoid(gate_a.astype(jnp.float32))
    i = jax.nn.sigmoid(gate_x.astype(jnp.float32))
    log_a = -LRU_C * r * jax.nn.softplus(-lam.astype(jnp.float32))
    a = jnp.exp(log_a)
    b = jnp.sqrt(-jnp.expm1(2.0 * log_a)) * (i * x.astype(jnp.float32))
    return a, b


def _combine(e1, e2):
    a1, b1 = e1
    a2, b2 = e2
    return a1 * a2, a2 * b1 + b2


def linear_recurrence(a, b, h0):
    b = b.at[:, 0].add(a[:, 0] * h0)
    _, h = lax.associative_scan(_combine, (a, b), axis=1)
    return h


def bidirectional_rglru(u, u_c, w_a, b_a, w_x, b_x, lam):
    h0 = jnp.zeros((u_c.shape[0], LRU_WIDTH), jnp.float32)
    a_c, b_c = rglru_coeffs(u_c, w_a[0], b_a[0], w_x[0], b_x[0], lam[0])
    a_l, b_l = rglru_coeffs(u, w_a[0], b_a[0], w_x[0], b_x[0], lam[0])
    h_cf = linear_recurrence(a_c, b_c, h0)
    h_lf = linear_recurrence(a_l, b_l, h_cf[:, -1])
    a_c, b_c = rglru_coeffs(jnp.flip(u_c, 1), w_a[1], b_a[1], w_x[1], b_x[1], lam[1])
    a_l, b_l = rglru_coeffs(jnp.flip(u, 1), w_a[1], b_a[1], w_x[1], b_x[1], lam[1])
    h_cb = linear_recurrence(a_c, b_c, h0)
    h_lb = linear_recurrence(a_l, b_l, h_cb[:, -1])
    y = (h_lf + jnp.flip(h_lb, 1)).astype(u.dtype)
    return y, h_cf, jnp.flip(h_cb, 1)


def mla_queries(cq, q_norm, w_q_up, cos, sin):
    bsz, n, _ = cq.shape
    q = (rmsnorm(cq, q_norm) @ w_q_up).reshape(bsz, n, MLA_HEADS, MLA_QK_DIM)
    q_nope, q_rope = q[..., :MLA_NOPE_DIM], q[..., MLA_NOPE_DIM:]
    if cos is not None:
        q_rope = apply_rope(q_rope, cos[:, None, :], sin[:, None, :])
    return jnp.concatenate([q_nope, q_rope], axis=-1)


def mla_keys_values(ckv, k_rope, kv_norm, w_kv_up, cos, sin):
    bsz, n, _ = ckv.shape
    kv = (rmsnorm(ckv, kv_norm) @ w_kv_up).reshape(bsz, n, MLA_HEADS, MLA_NOPE_DIM + MLA_V_DIM)
    k_nope, v = kv[..., :MLA_NOPE_DIM], kv[..., MLA_NOPE_DIM:]
    if cos is not None:
        k_rope = apply_rope(k_rope, cos, sin)
    k_rope = jnp.broadcast_to(k_rope[:, :, None, :], (bsz, n, MLA_HEADS, MLA_ROPE_DIM))
    return jnp.concatenate([k_nope, k_rope], axis=-1), v


def blockwise_attention(q, k, v, k_ctx, v_ctx):
    bsz, n, heads, dq = q.shape
    scale = dq ** -0.5
    kk = jnp.concatenate([k, k_ctx], axis=1)
    vv = jnp.concatenate([v, v_ctx], axis=1)
    qb = q.reshape(bsz, n // Q_BLOCK, Q_BLOCK, heads, dq).transpose(1, 0, 2, 3, 4)

    def attend(qi):
        s = jnp.einsum('bqhd,bkhd->bhqk', qi, kk).astype(jnp.float32) * scale
        p = jax.nn.softmax(s, axis=-1).astype(vv.dtype)
        return jnp.einsum('bhqk,bkhd->bqhd', p, vv)

    o = lax.map(attend, qb)
    return o.transpose(1, 0, 2, 3, 4).reshape(bsz, n, heads * vv.shape[-1])


def merge_branches(branches, gate_logits, w_branch, w_out):
    bsz, n, _ = gate_logits.shape
    gates = jax.nn.sigmoid(gate_logits).reshape(bsz, n, N_BRANCH, D_MODEL)
    y = sum(gates[:, :, i] * (branches[i] @ w_branch[i]) for i in range(N_BRANCH))
    return y @ w_out


def mixing_sublayer(h, hc, cos, sin, w_in, na_rpb, lru_conv_w, lru_conv_b, lru_w_a, lru_b_a, lru_w_x, lru_b_x,
                    lru_lam, mla_q_norm, mla_kv_norm, mla_w_q_up, mla_w_kv_up, w_branch, w_out, with_ctx_out):
    bsz, n, _ = h.shape
    n_ctx = hc.shape[1]
    na_qkv, lru_x, lru_g, cq, ckv, kr, gate_logits = split_in_proj(h @ w_in)
    na_qkv_c, lru_x_c, lru_g_c, cq_c, ckv_c, kr_c, gate_logits_c = split_in_proj(hc @ w_in)
    qkv = na_qkv.reshape(bsz, n, 3, NA_HEADS, NA_HEAD_DIM)
    qkv_c = na_qkv_c.reshape(bsz, n_ctx, 3, NA_HEADS, NA_HEAD_DIM)
    out_a = neighbourhood_attention(qkv[:, :, 0], qkv[:, :, 1], qkv[:, :, 2], qkv_c[:, :, 1], qkv_c[:, :, 2], na_rpb)
    u = depthwise_conv(lru_x, lru_conv_w, lru_conv_b)
    u_c = depthwise_conv(lru_x_c, lru_conv_w, lru_conv_b)
    y_b, h_cf, h_cb = bidirectional_rglru(u, u_c, lru_w_a, lru_b_a, lru_w_x, lru_b_x, lru_lam)
    out_b = jax.nn.gelu(lru_g) * y_b
    q_m = mla_queries(cq, mla_q_norm, mla_w_q_up, cos, sin)
    k_m, v_m = mla_keys_values(ckv, kr, mla_kv_norm, mla_w_kv_up, cos, sin)
    k_mc, v_mc = mla_keys_values(ckv_c, kr_c, mla_kv_norm, mla_w_kv_up, None, None)
    out_c = blockwise_attention(q_m, k_m, v_m, k_mc, v_mc)
    y = merge_branches((out_a, out_b, out_c), gate_logits, w_branch, w_out)
    if not with_ctx_out:
        return y, None
    out_ac = softmax_attention(qkv_c[:, :, 0], qkv_c[:, :, 1], qkv_c[:, :, 2])
    out_bc = jax.nn.gelu(lru_g_c) * (h_cf + h_cb).astype(lru_g_c.dtype)
    q_mc = mla_queries(cq_c, mla_q_norm, mla_w_q_up, None, None)
    out_cc = softmax_attention(q_mc, k_mc, v_mc)
    yc = merge_branches((out_ac, out_bc, out_cc), gate_logits_c, w_branch, w_out)
    return y, yc


def conv_ffn(h, w_up, conv_w, conv_b, w_down):
    u = depthwise_conv(h @ w_up, conv_w, conv_b)
    val, gate = jnp.split(u, 2, axis=-1)
    return (jax.nn.silu(gate) * val) @ w_down


def setup_inputs(seed: int = 0) -> dict:
    key = jax.random.key(seed)
    ks = jax.random.split(key, 32)
    f32 = jnp.float32
    D = D_MODEL

    def nrm(k, shape, scale):
        return jax.random.normal(k, shape, f32) * scale

    lam_u = jax.random.uniform(ks[16], (DEPTH, 2, LRU_WIDTH), f32, 0.9, 0.999)
    lam_s = lam_u ** (1.0 / LRU_C)
    return {
        'x': nrm(ks[0], (BATCH, SEQ, D), 1.0),
        'c': nrm(ks[1], (BATCH, D), 1.0),
        'ctx': nrm(ks[2], (BATCH, CTX_LEN, D), 1.0),
        'c_ctx': nrm(ks[3], (D,), 1.0),
        'w_mod': nrm(ks[4], (DEPTH, D, N_MOD * D), 0.5 * D ** -0.5),
        'b_mod': nrm(ks[5], (DEPTH, N_MOD * D), 0.02),
        'norm_mix': 1.0 + nrm(ks[6], (DEPTH, D), 0.05),
        'norm_ffn': 1.0 + nrm(ks[7], (DEPTH, D), 0.05),
        'w_in': nrm(ks[8], (DEPTH, D, D_IN), D ** -0.5),
        'na_rpb': nrm(ks[9], (DEPTH, NA_HEADS, 2 * NA_WIN_H - 1, 2 * NA_WIN_W - 1), 0.1),
        'lru_conv_w': nrm(ks[10], (DEPTH, LRU_CONV_W, LRU_WIDTH), LRU_CONV_W ** -0.5),
        'lru_conv_b': nrm(ks[11], (DEPTH, LRU_WIDTH), 0.02),
        'lru_w_a': nrm(ks[12], (DEPTH, 2, LRU_BLOCKS, LRU_BLOCK_DIM, LRU_BLOCK_DIM), LRU_BLOCK_DIM ** -0.5),
        'lru_b_a': nrm(ks[13], (DEPTH, 2, LRU_WIDTH), 0.02),
        'lru_w_x': nrm(ks[14], (DEPTH, 2, LRU_BLOCKS, LRU_BLOCK_DIM, LRU_BLOCK_DIM), LRU_BLOCK_DIM ** -0.5),
        'lru_b_x': nrm(ks[15], (DEPTH, 2, LRU_WIDTH), 0.02),
        'lru_lam': jnp.log(lam_s) - jnp.log1p(-lam_s),
        'mla_q_norm': 1.0 + nrm(ks[17], (DEPTH, MLA_Q_RANK), 0.05),
        'mla_kv_norm': 1.0 + nrm(ks[18], (DEPTH, MLA_KV_RANK), 0.05),
        'mla_w_q_up': nrm(ks[19], (DEPTH, MLA_Q_RANK, MLA_HEADS * MLA_QK_DIM), MLA_Q_RANK ** -0.5),
        'mla_w_kv_up': nrm(ks[20], (DEPTH, MLA_KV_RANK, MLA_HEADS * (MLA_NOPE_DIM + MLA_V_DIM)), MLA_KV_RANK ** -0.5),
        'w_branch': nrm(ks[21], (DEPTH, N_BRANCH, BRANCH_WIDTH, D), BRANCH_WIDTH ** -0.5),
        'w_out': nrm(ks[22], (DEPTH, D, D), D ** -0.5),
        'ffn_w_up': nrm(ks[23], (DEPTH, D, 2 * D_FF), D ** -0.5),
        'ffn_conv_w': nrm(ks[24], (DEPTH, FFN_CONV_W, 2 * D_FF), FFN_CONV_W ** -0.5),
        'ffn_conv_b': nrm(ks[25], (DEPTH, 2 * D_FF), 0.02),
        'ffn_w_down': nrm(ks[26], (DEPTH, D_FF, D), D_FF ** -0.5),
        'norm_final': 1.0 + nrm(ks[27], (D,), 0.05),
    }


def reference(x, c, ctx, c_ctx, w_mod, b_mod, norm_mix, norm_ffn, w_in, na_rpb, lru_conv_w, lru_conv_b,
              lru_w_a, lru_b_a, lru_w_x, lru_b_x, lru_lam, mla_q_norm, mla_kv_norm, mla_w_q_up, mla_w_kv_up,
              w_branch, w_out, ffn_w_up, ffn_conv_w, ffn_conv_b, ffn_w_down, norm_final):
    n = x.shape[1]
    cos, sin = axial_rope_angles(n, MLA_ROPE_DIM)
    silu_c = jax.nn.silu(c)
    silu_cc = jax.nn.silu(c_ctx)
    xc = ctx
    for l in range(DEPTH):
        last = l == DEPTH - 1
        mod = (silu_c @ w_mod[l] + b_mod[l])[:, None, :]
        mod_c = silu_cc @ w_mod[l] + b_mod[l]
        sh1, sc1, g1, sh2, sc2, g2 = jnp.split(mod, N_MOD, axis=-1)
        sh1c, sc1c, g1c, sh2c, sc2c, g2c = jnp.split(mod_c, N_MOD, axis=-1)
        h = modulate(rmsnorm(x, norm_mix[l]), sh1, sc1)
        hc = modulate(rmsnorm(xc, norm_mix[l]), sh1c, sc1c)
        y, yc = mixing_sublayer(h, hc, cos, sin, w_in[l], na_rpb[l], lru_conv_w[l], lru_conv_b[l], lru_w_a[l],
                                lru_b_a[l], lru_w_x[l], lru_b_x[l], lru_lam[l], mla_q_norm[l], mla_kv_norm[l],
                                mla_w_q_up[l], mla_w_kv_up[l], w_branch[l], w_out[l], not last)
        x = x + g1 * y
        h2 = modulate(rmsnorm(x, norm_ffn[l]), sh2, sc2)
        x = x + g2 * conv_ffn(h2, ffn_w_up[l], ffn_conv_w[l], ffn_conv_b[l], ffn_w_down[l])
        if not last:
            xc = xc + g1c * yc
            h2c = modulate(rmsnorm(xc, norm_ffn[l]), sh2c, sc2c)
            xc = xc + g2c * conv_ffn(h2c, ffn_w_up[l], ffn_conv_w[l], ffn_conv_b[l], ffn_w_down[l])
    return rmsnorm(x, norm_final)
```

```python
import functools

import numpy as np
import jax
import jax.numpy as jnp
from jax import lax
from jax.experimental import pallas as pl
from jax.experimental.pallas import tpu as pltpu

F32 = jnp.float32
BF16 = jnp.bfloat16

GRID_W = 64
NORM_EPS = 1e-6
NEG_INF = -1e30
N_MOD = 6
NA_HEAD_DIM = 128
NA_WIN_H = 8
NA_WIN_W = 16
LRU_C = 8.0
MLA_NOPE_DIM = 128
MLA_ROPE_DIM = 64
MLA_V_DIM = 128
MLA_QK_DIM = MLA_NOPE_DIM + MLA_ROPE_DIM
ROPE_THETA = 10000.0

LANE = 128
SUBLANE_F32 = 8
SUBLANE_BF16 = 16
MLA_HEAD_PAD = 256
NA_QROWS = 8
NA_KROWS = 16
MIB = 1024 * 1024


def _cparams(semantics, vmem_mib):
    return pltpu.CompilerParams(dimension_semantics=semantics, vmem_limit_bytes=vmem_mib * MIB)


def _tile(n, target, mult):
    best = None
    for t in range(mult, min(n, target) + 1, mult):
        if n % t == 0:
            best = t
    assert best is not None, (n, target, mult)
    return best


def _group_of_tile(i, tm, bn, n):
    return jnp.where(i * tm >= bn, 0, 1 + (i * tm) // n)


def _rmsnorm_f32(x, gamma):
    ms = jnp.mean(x * x, axis=-1, keepdims=True)
    return x * lax.rsqrt(ms + NORM_EPS) * gamma


def _segment_edges(row0, tm, bn, n, c):
    r = row0 + lax.broadcasted_iota(jnp.int32, (tm, 1), 0)
    is_lat = r < bn
    seg = jnp.where(is_lat, n, c)
    pos = jnp.where(is_lat, r % n, (r - bn) % c)
    return pos == 0, pos == seg - 1


def _shift_rows(z, prev_row, next_row, first, last):
    tm = z.shape[0]
    t = lax.broadcasted_iota(jnp.int32, (tm, 1), 0)
    zm1 = jnp.where(t == 0, prev_row, pltpu.roll(z, 1, 0))
    zm1 = jnp.where(first, 0.0, zm1)
    zp1 = jnp.where(t == tm - 1, next_row, pltpu.roll(z, tm - 1, 0))
    zp1 = jnp.where(last, 0.0, zp1)
    return zm1, zp1


def _mod_kernel(c_ref, w_ref, b_ref, o_ref):
    cc = c_ref[...]
    s = cc * jax.nn.sigmoid(cc)
    o_ref[...] = jnp.dot(s, w_ref[...], preferred_element_type=F32,
                         precision=lax.Precision.HIGHEST) + b_ref[...]


def _modulation(cond, w_mod, b_mod):
    depth, d, nm = w_mod.shape
    rows = cond.shape[0]
    tn = _tile(nm, 1024, LANE)
    return pl.pallas_call(
        _mod_kernel,
        out_shape=jax.ShapeDtypeStruct((depth, rows, nm), F32),
        grid=(depth, nm // tn),
        in_specs=[
            pl.BlockSpec((rows, d), lambda l, j: (0, 0)),
            pl.BlockSpec((None, d, tn), lambda l, j: (l, 0, j)),
            pl.BlockSpec((None, 1, tn), lambda l, j: (l, 0, j)),
        ],
        out_specs=pl.BlockSpec((None, rows, tn), lambda l, j: (l, 0, j)),
        compiler_params=_cparams(("arbitrary", "arbitrary"), 40),
        name="adaln_mod",
    )(cond, w_mod, b_mod.reshape(depth, 1, nm))


def _in_proj_kernel(x_ref, gam_ref, sh_ref, sc_ref, w_ref, o_ref, h_ref):
    @pl.when(pl.program_id(1) == 0)
    def _():
        y = _rmsnorm_f32(x_ref[...], gam_ref[...])
        h_ref[...] = (y * (1.0 + sc_ref[0]) + sh_ref[0]).astype(BF16)

    o_ref[...] = jnp.dot(h_ref[...], w_ref[...], preferred_element_type=F32).astype(o_ref.dtype)


def _in_proj(xall, gamma, shift, scale, w, l, dims):
    tt, d = xall.shape
    npad = w.shape[2]
    tm = _tile(dims["tile_rows"], 1024, SUBLANE_BF16)
    tn = _tile(npad, 1024, LANE)
    grp = functools.partial(_group_of_tile, tm=tm, bn=dims["bn"], n=dims["n"])
    return pl.pallas_call(
        _in_proj_kernel,
        out_shape=jax.ShapeDtypeStruct((tt, npad), BF16),
        grid=(tt // tm, npad // tn),
        in_specs=[
            pl.BlockSpec((tm, d), lambda i, j: (i, 0)),
            pl.BlockSpec((None, 1, d), lambda i, j: (l, 0, 0)),
            pl.BlockSpec((1, 1, d), lambda i, j: (grp(i), 0, 0)),
            pl.BlockSpec((1, 1, d), lambda i, j: (grp(i), 0, 0)),
            pl.BlockSpec((None, d, tn), lambda i, j: (l, 0, j)),
        ],
        out_specs=pl.BlockSpec((tm, tn), lambda i, j: (i, j)),
        scratch_shapes=[pltpu.VMEM((tm, d), BF16)],
        compiler_params=_cparams(("parallel", "arbitrary"), 48),
        name="in_proj",
    )(xall, gamma, shift, scale, w)


def _na_kernel(q_ref, k_ref, v_ref, kc_ref, vc_ref, bias_ref, o_ref, *, grid_rows, scale):
    g = pl.program_id(2)
    k0 = jnp.clip(NA_QROWS * g - NA_WIN_H // 2, 0, grid_rows - NA_KROWS) * GRID_W
    k0 = pl.multiple_of(k0, GRID_W * 4)
    nk = NA_KROWS * GRID_W
    q = q_ref[...]
    kw = k_ref[pl.ds(k0, nk), :]
    vw = v_ref[pl.ds(k0, nk), :]
    nt = (((1,), (1,)), ((), ()))
    s = lax.dot_general(q, kw, nt, preferred_element_type=F32) * scale + bias_ref[...]
    sc = lax.dot_general(q, kc_ref[...], nt, preferred_element_type=F32) * scale
    m = jnp.maximum(jnp.max(s, axis=-1, keepdims=True), jnp.max(sc, axis=-1, keepdims=True))
    p = jnp.exp(s - m)
    pc = jnp.exp(sc - m)
    denom = jnp.sum(p, axis=-1, keepdims=True) + jnp.sum(pc, axis=-1, keepdims=True)
    o = jnp.dot(p.astype(BF16), vw, preferred_element_type=F32)
    o = o + jnp.dot(pc.astype(BF16), vc_ref[...], preferred_element_type=F32)
    o_ref[...] = (o * (1.0 / denom)).astype(o_ref.dtype)


def _na_bias_tables(rpb, grid_rows):
    n_groups = grid_rows // NA_QROWS
    variants = sorted({0, min(1, n_groups - 1), n_groups - 1})
    cidx = np.arange(GRID_W)
    c_start = np.clip(cidx - NA_WIN_W // 2, 0, GRID_W - NA_WIN_W)
    in_win = (cidx[None, :] >= c_start[:, None]) & (cidx[None, :] < c_start[:, None] + NA_WIN_W)
    col_idx = np.clip(cidx[None, :] - cidx[:, None], -(NA_WIN_W - 1), NA_WIN_W - 1) + (NA_WIN_W - 1)
    kh = min(NA_WIN_H, grid_rows)
    tables = []
    for g in variants:
        k0 = int(np.clip(NA_QROWS * g - NA_WIN_H // 2, 0, grid_rows - NA_KROWS))
        r = NA_QROWS * g + np.arange(NA_QROWS)
        first_key_row = np.clip(r - kh // 2, 0, grid_rows - kh)
        kr = k0 + np.arange(NA_KROWS)
        in_rows = (kr[None, :] >= first_key_row[:, None]) & (kr[None, :] < first_key_row[:, None] + kh)
        row_idx = np.clip(kr[None, :] - r[:, None] + (NA_WIN_H - 1), 0, 2 * NA_WIN_H - 2)
        vals = rpb.astype(F32)[:, row_idx[:, None, :, None], col_idx[None, :, None, :]]
        ok = in_rows[:, None, :, None] & in_win[None, :, None, :]
        vals = jnp.where(ok[None], vals, NEG_INF)
        tables.append(vals.reshape(rpb.shape[0], NA_QROWS * GRID_W, NA_KROWS * GRID_W))
    return jnp.stack(tables, axis=1)


def _na_attention(z, bias, dims):
    tt = z.shape[0]
    b, n, c = dims["b"], dims["n"], dims["c"]
    heads = dims["na_heads"]
    grid_rows = n // GRID_W
    n_groups = grid_rows // NA_QROWS
    tq = NA_QROWS * GRID_W
    n_var = bias.shape[1]
    ctx_blk0 = dims["bn"] // c

    def variant(g):
        if n_var == 1:
            return 0
        if n_var == 2:
            return jnp.where(g == 0, 0, 1)
        return jnp.where(g == 0, 0, jnp.where(g == n_groups - 1, 2, 1))

    kern = functools.partial(_na_kernel, grid_rows=grid_rows, scale=NA_HEAD_DIM ** -0.5)
    return pl.pallas_call(
        kern,
        out_shape=jax.ShapeDtypeStruct((tt, heads * NA_HEAD_DIM), BF16),
        grid=(b, heads, n_groups),
        in_specs=[
            pl.BlockSpec((tq, NA_HEAD_DIM), lambda bi, h, g: (bi * n_groups + g, h)),
            pl.BlockSpec((n, NA_HEAD_DIM), lambda bi, h, g: (bi, heads + h)),
            pl.BlockSpec((n, NA_HEAD_DIM), lambda bi, h, g: (bi, 2 * heads + h)),
            pl.BlockSpec((c, NA_HEAD_DIM), lambda bi, h, g: (ctx_blk0 + bi, heads + h)),
            pl.BlockSpec((c, NA_HEAD_DIM), lambda bi, h, g: (ctx_blk0 + bi, 2 * heads + h)),
            pl.BlockSpec((None, None, tq, NA_KROWS * GRID_W), lambda bi, h, g: (h, variant(g), 0, 0)),
        ],
        out_specs=pl.BlockSpec((tq, NA_HEAD_DIM), lambda bi, h, g: (bi * n_groups + g, h)),
        compiler_params=_cparams(("parallel", "parallel", "arbitrary"), 40),
        name="na_attention",
    )(z, z, z, z, z, bias)


def _ctx_attn_kernel(q_ref, k_ref, v_ref, prev_ref, o_ref, *, scale):
    del prev_ref
    nt = (((1,), (1,)), ((), ()))
    s = lax.dot_general(q_ref[...], k_ref[...], nt, preferred_element_type=F32) * scale
    m = jnp.max(s, axis=-1, keepdims=True)
    p = jnp.exp(s - m)
    denom = jnp.sum(p, axis=-1, keepdims=True)
    o = jnp.dot(p.astype(BF16), v_ref[...], preferred_element_type=F32)
    o_ref[...] = (o * (1.0 / denom)).astype(o_ref.dtype)


def _ctx_attention(q_arr, k_arr, v_arr, out_buf, dims, heads, dq, q_col0, k_col0, v_col0, scale, name):
    b, c = dims["b"], dims["c"]
    blk0 = dims["bn"] // c
    dv = out_buf.shape[1] // heads
    return pl.pallas_call(
        functools.partial(_ctx_attn_kernel, scale=scale),
        out_shape=jax.ShapeDtypeStruct(out_buf.shape, out_buf.dtype),
        grid=(b, heads),
        in_specs=[
            pl.BlockSpec((c, dq), lambda bi, h: (blk0 + bi, q_col0 + h)),
            pl.BlockSpec((c, dq), lambda bi, h: (blk0 + bi, k_col0 + h)),
            pl.BlockSpec((c, dv), lambda bi, h: (blk0 + bi, v_col0 + h)),
            pl.BlockSpec(memory_space=pl.ANY),
        ],
        out_specs=pl.BlockSpec((c, dv), lambda bi, h: (blk0 + bi, h)),
        input_output_aliases={3: 0},
        compiler_params=_cparams(("parallel", "parallel"), 32),
        name=name,
    )(q_arr, k_arr, v_arr, out_buf)


def _lru_kernel(*refs, reverse, t_chunk, n_ctx_chunks, n_lat_chunks, n_blocks, block_dim):
    if reverse:
        (x_ref, xp_ref, xn_ref, cw_ref, cb_ref, wa_ref, ba_ref, wx_ref, bx_ref, lam_ref,
         hf_ref, g_ref, o_ref, carry_ref) = refs
    else:
        (x_ref, xp_ref, xn_ref, cw_ref, cb_ref, wa_ref, ba_ref, wx_ref, bx_ref, lam_ref,
         o_ref, carry_ref) = refs
    j = pl.program_id(1)

    @pl.when(j == 0)
    def _():
        carry_ref[...] = jnp.zeros_like(carry_ref)

    is_ctx = j < n_ctx_chunks
    pos = jnp.where(is_ctx, j, j - n_ctx_chunks)
    n_seg = jnp.where(is_ctx, n_ctx_chunks, n_lat_chunks)
    chunk = (n_seg - 1 - pos) if reverse else pos
    first = chunk == 0
    last = chunk == n_seg - 1

    xm = x_ref[...].astype(F32)
    xp = xp_ref[...].astype(F32)
    xn = xn_ref[...].astype(F32)
    hp = xp.shape[0]
    pm1 = jnp.where(first, 0.0, xp[hp - 1:hp, :])
    pm2 = jnp.where(first, 0.0, xp[hp - 2:hp - 1, :])
    nn0 = jnp.where(last, 0.0, xn[0:1, :])
    t = lax.broadcasted_iota(jnp.int32, (t_chunk, 1), 0)
    xm1 = jnp.where(t == 0, pm1, pltpu.roll(xm, 1, 0))
    xm2 = jnp.where(t == 0, pm2, jnp.where(t == 1, pm1, pltpu.roll(xm, 2, 0)))
    xp1 = jnp.where(t == t_chunk - 1, nn0, pltpu.roll(xm, t_chunk - 1, 0))
    cw = cw_ref[...]
    u = cw[0:1] * xm2 + cw[1:2] * xm1 + cw[2:3] * xm + cw[3:4] * xp1 + cb_ref[...]

    ub = u.astype(BF16)
    ga, gx = [], []
    for kb in range(n_blocks):
        blk = ub[:, kb * block_dim:(kb + 1) * block_dim]
        ga.append(jnp.dot(blk, wa_ref[kb], preferred_element_type=F32))
        gx.append(jnp.dot(blk, wx_ref[kb], preferred_element_type=F32))
    gate_a = jnp.concatenate(ga, axis=1) + ba_ref[...]
    gate_x = jnp.concatenate(gx, axis=1) + bx_ref[...]
    r = jax.nn.sigmoid(gate_a)
    i_gate = jax.nn.sigmoid(gate_x)
    nlam = -lam_ref[...]
    softplus = jnp.maximum(nlam, 0.0) + jnp.log1p(jnp.exp(-jnp.abs(nlam)))
    log_a = -LRU_C * r * softplus
    a = jnp.exp(log_a)
    bv = jnp.sqrt(-jnp.tanh(log_a) * (a * a + 1.0)) * (i_gate * u)

    k = 1
    while k < t_chunk:
        if reverse:
            keep = t < t_chunk - k
            a_s = pltpu.roll(a, t_chunk - k, 0)
            b_s = pltpu.roll(bv, t_chunk - k, 0)
        else:
            keep = t >= k
            a_s = pltpu.roll(a, k, 0)
            b_s = pltpu.roll(bv, k, 0)
        bv = jnp.where(keep, a * b_s + bv, bv)
        a = jnp.where(keep, a * a_s, a)
        k *= 2
    h = a * carry_ref[0:1, :] + bv
    if reverse:
        carry_ref[0:1, :] = h[0:1, :]
        g = g_ref[...].astype(F32)
        o_ref[...] = (jax.nn.gelu(g) * (hf_ref[...] + h)).astype(o_ref.dtype)
    else:
        carry_ref[0:1, :] = h[t_chunk - 1:t_chunk, :]
        o_ref[...] = h


def _lru_scan(z, l, direction, conv_w, conv_b, w_a, b_a, w_x, b_x, lam, h_fwd, dims):
    tt = z.shape[0]
    b, n, c, lw = dims["b"], dims["n"], dims["c"], dims["lru_width"]
    n_blocks, block_dim = w_a.shape[2], w_a.shape[3]
    t_chunk = _tile(c, 256, SUBLANE_BF16)
    halo = SUBLANE_BF16
    n_ctx_chunks, n_lat_chunks = c // t_chunk, n // t_chunk
    ctx_chunk0 = dims["bn"] // t_chunk
    col_x = dims["off_lru_x"] // lw
    col_g = dims["off_lru_g"] // lw
    reverse = direction == 1
    per_halo = t_chunk // halo
    n_halo_blocks = tt // halo

    def chunk_block(bi, j):
        is_ctx = j < n_ctx_chunks
        pos = jnp.where(is_ctx, j, j - n_ctx_chunks)
        if reverse:
            pos = jnp.where(is_ctx, n_ctx_chunks - 1 - pos, n_lat_chunks - 1 - pos)
        return jnp.where(is_ctx, ctx_chunk0 + bi * n_ctx_chunks + pos, bi * n_lat_chunks + pos)

    def prev_block(bi, j):
        return jnp.maximum(chunk_block(bi, j) * per_halo - 1, 0)

    def next_block(bi, j):
        return jnp.minimum((chunk_block(bi, j) + 1) * per_halo, n_halo_blocks - 1)

    vec = lambda: pl.BlockSpec((None, None, 1, lw), lambda bi, j: (l, direction, 0, 0))
    gate_w = lambda: pl.BlockSpec((None, None, n_blocks, block_dim, block_dim),
                                  lambda bi, j: (l, direction, 0, 0, 0))
    in_specs = [
        pl.BlockSpec((t_chunk, lw), lambda bi, j: (chunk_block(bi, j), col_x)),
        pl.BlockSpec((halo, lw), lambda bi, j: (prev_block(bi, j), col_x)),
        pl.BlockSpec((halo, lw), lambda bi, j: (next_block(bi, j), col_x)),
        pl.BlockSpec((None, conv_w.shape[1], lw), lambda bi, j: (l, 0, 0)),
        pl.BlockSpec((None, 1, lw), lambda bi, j: (l, 0, 0)),
        gate_w(), vec(), gate_w(), vec(), vec(),
    ]
    args = [z, z, z, conv_w, conv_b, w_a, b_a, w_x, b_x, lam]
    if reverse:
        in_specs += [
            pl.BlockSpec((t_chunk, lw), lambda bi, j: (chunk_block(bi, j), 0)),
            pl.BlockSpec((t_chunk, lw), lambda bi, j: (chunk_block(bi, j), col_g)),
        ]
        args += [h_fwd, z]
        out_dtype = BF16
    else:
        out_dtype = F32
    kern = functools.partial(_lru_kernel, reverse=reverse, t_chunk=t_chunk, n_ctx_chunks=n_ctx_chunks,
                             n_lat_chunks=n_lat_chunks, n_blocks=n_blocks, block_dim=block_dim)
    return pl.pallas_call(
        kern,
        out_shape=jax.ShapeDtypeStruct((tt, lw), out_dtype),
        grid=(b, n_ctx_chunks + n_lat_chunks),
        in_specs=in_specs,
        out_specs=pl.BlockSpec((t_chunk, lw), lambda bi, j: (chunk_block(bi, j), 0)),
        scratch_shapes=[pltpu.VMEM((SUBLANE_F32, lw), F32)],
        compiler_params=_cparams(("parallel", "arbitrary"), 40),
        name="lru_bwd" if reverse else "lru_fwd",
    )(*args)


def _mla_proj_kernel(z_ref, qn_ref, kvn_ref, wq_ref, wkv_ref, ct_ref, st_ref, q_ref, k_ref, v_ref,
                     *, q_rank, kv_rank, heads, q_scale):
    z = z_ref[...]
    cq = z[:, :q_rank].astype(F32)
    ckv = z[:, q_rank:q_rank + kv_rank].astype(F32)
    krp = z[:, q_rank + kv_rank:q_rank + kv_rank + LANE].astype(F32)
    ct = ct_ref[...]
    st = st_ref[...]

    def rope(tile):
        return tile * ct + pltpu.roll(tile, LANE // 2, 1) * st

    qn = _rmsnorm_f32(cq, qn_ref[...]).astype(BF16)
    q = jnp.dot(qn, wq_ref[...], preferred_element_type=F32)
    kvn = _rmsnorm_f32(ckv, kvn_ref[...]).astype(BF16)
    kv = jnp.dot(kvn, wkv_ref[...], preferred_element_type=F32)
    kr = rope(krp).astype(BF16)
    for h in range(heads):
        c0 = h * MLA_HEAD_PAD
        q_ref[:, c0:c0 + LANE] = (q[:, c0:c0 + LANE] * q_scale).astype(BF16)
        q_ref[:, c0 + LANE:c0 + 2 * LANE] = (rope(q[:, c0 + LANE:c0 + 2 * LANE]) * q_scale).astype(BF16)
        k_ref[:, c0:c0 + LANE] = kv[:, h * LANE:(h + 1) * LANE].astype(BF16)
        k_ref[:, c0 + LANE:c0 + 2 * LANE] = kr
    v_ref[...] = kv[:, heads * LANE:].astype(BF16)


def _mla_proj(z, l, q_norm, kv_norm, wq, wkv, ct, st, dims):
    tt = z.shape[0]
    heads, q_rank, kv_rank = dims["mla_heads"], dims["q_rank"], dims["kv_rank"]
    mg = dims["mla_group"]
    tm = _tile(dims["tile_rows"], 512, SUBLANE_BF16)
    col = dims["off_mla"] // mg
    kern = functools.partial(_mla_proj_kernel, q_rank=q_rank, kv_rank=kv_rank, heads=heads,
                             q_scale=MLA_QK_DIM ** -0.5)
    return pl.pallas_call(
        kern,
        out_shape=(jax.ShapeDtypeStruct((tt, heads * MLA_HEAD_PAD), BF16),
                   jax.ShapeDtypeStruct((tt, heads * MLA_HEAD_PAD), BF16),
                   jax.ShapeDtypeStruct((tt, heads * MLA_V_DIM), BF16)),
        grid=(tt // tm,),
        in_specs=[
            pl.BlockSpec((tm, mg), lambda i: (i, col)),
            pl.BlockSpec((None, 1, q_rank), lambda i: (l, 0, 0)),
            pl.BlockSpec((None, 1, kv_rank), lambda i: (l, 0, 0)),
            pl.BlockSpec((None, q_rank, heads * MLA_HEAD_PAD), lambda i: (l, 0, 0)),
            pl.BlockSpec((None, kv_rank, heads * 2 * LANE), lambda i: (l, 0, 0)),
            pl.BlockSpec((tm, LANE), lambda i: (i, 0)),
            pl.BlockSpec((tm, LANE), lambda i: (i, 0)),
        ],
        out_specs=(pl.BlockSpec((tm, heads * MLA_HEAD_PAD), lambda i: (i, 0)),
                   pl.BlockSpec((tm, heads * MLA_HEAD_PAD), lambda i: (i, 0)),
                   pl.BlockSpec((tm, heads * MLA_V_DIM), lambda i: (i, 0))),
        compiler_params=_cparams(("parallel",), 48),
        name="mla_proj",
    )(z, q_norm, kv_norm, wq, wkv, ct, st)


def _mla_flash_kernel(q_ref, k_ref, v_ref, kc_ref, vc_ref, o_ref, *, tk, n_kv_blocks):
    nt = (((1,), (1,)), ((), ()))
    q = q_ref[...]
    s = lax.dot_general(q, kc_ref[...], nt, preferred_element_type=F32)
    m = jnp.max(s, axis=-1, keepdims=True)
    p = jnp.exp(s - m)
    lsum = jnp.sum(p, axis=-1, keepdims=True)
    acc = jnp.dot(p.astype(BF16), vc_ref[...], preferred_element_type=F32)

    def body(ci, carry):
        m, lsum, acc = carry
        r0 = pl.multiple_of(ci * tk, tk)
        s = lax.dot_general(q, k_ref[pl.ds(r0, tk), :], nt, preferred_element_type=F32)
        m_new = jnp.maximum(m, jnp.max(s, axis=-1, keepdims=True))
        alpha = jnp.exp(m - m_new)
        p = jnp.exp(s - m_new)
        lsum = alpha * lsum + jnp.sum(p, axis=-1, keepdims=True)
        acc = alpha * acc + jnp.dot(p.astype(BF16), v_ref[pl.ds(r0, tk), :], preferred_element_type=F32)
        return m_new, lsum, acc

    m, lsum, acc = lax.fori_loop(0, n_kv_blocks, body, (m, lsum, acc))
    o_ref[...] = (acc * (1.0 / lsum)).astype(o_ref.dtype)


def _mla_attention(q, k, v, dims):
    tt = q.shape[0]
    b, n, c, heads = dims["b"], dims["n"], dims["c"], dims["mla_heads"]
    tq = _tile(n, 256, SUBLANE_BF16)
    tk = _tile(n, 512, SUBLANE_BF16)
    ctx_blk0 = dims["bn"] // c
    nq = n // tq
    return pl.pallas_call(
        functools.partial(_mla_flash_kernel, tk=tk, n_kv_blocks=n // tk),
        out_shape=jax.ShapeDtypeStruct((tt, heads * MLA_V_DIM), BF16),
        grid=(b, heads, nq),
        in_specs=[
            pl.BlockSpec((tq, MLA_HEAD_PAD), lambda bi, h, i: (bi * nq + i, h)),
            pl.BlockSpec((n, MLA_HEAD_PAD), lambda bi, h, i: (bi, h)),
            pl.BlockSpec((n, MLA_V_DIM), lambda bi, h, i: (bi, h)),
            pl.BlockSpec((c, MLA_HEAD_PAD), lambda bi, h, i: (ctx_blk0 + bi, h)),
            pl.BlockSpec((c, MLA_V_DIM), lambda bi, h, i: (ctx_blk0 + bi, h)),
        ],
        out_specs=pl.BlockSpec((tq, MLA_V_DIM), lambda bi, h, i: (bi * nq + i, h)),
        compiler_params=_cparams(("parallel", "parallel", "arbitrary"), 40),
        name="mla_attention",
    )(q, k, v, k, v)


def _merge_kernel(ba_ref, bb_ref, bc_ref, wb_ref, ga_ref, gb_ref, gc_ref, o_ref):
    y = None
    for br, gate, i in ((ba_ref, ga_ref, 0), (bb_ref, gb_ref, 1), (bc_ref, gc_ref, 2)):
        t = jnp.dot(br[...], wb_ref[i], preferred_element_type=F32)
        t = jax.nn.sigmoid(gate[...].astype(F32)) * t
        y = t if y is None else y + t
    o_ref[...] = y.astype(o_ref.dtype)


def _merge(br_a, br_b, br_c, w_branch, z, l, n_rows, dims):
    d = dims["d"]
    bw = br_a.shape[1]
    tm = _tile(dims["tile_rows"], 1024, SUBLANE_BF16)
    tn = _tile(d, 512, LANE)
    gate_col0 = dims["off_gate"] // tn
    per_gate = d // tn
    branch = lambda: pl.BlockSpec((tm, bw), lambda i, j: (i, 0))
    gate = lambda k: pl.BlockSpec((tm, tn), lambda i, j: (i, gate_col0 + k * per_gate + j))
    return pl.pallas_call(
        _merge_kernel,
        out_shape=jax.ShapeDtypeStruct((n_rows, d), BF16),
        grid=(n_rows // tm, d // tn),
        in_specs=[branch(), branch(), branch(),
                  pl.BlockSpec((None, 3, bw, tn), lambda i, j: (l, 0, 0, j)),
                  gate(0), gate(1), gate(2)],
        out_specs=pl.BlockSpec((tm, tn), lambda i, j: (i, j)),
        compiler_params=_cparams(("parallel", "arbitrary"), 48),
        name="merge",
    )(br_a, br_b, br_c, w_branch, z, z, z)


def _out_proj_kernel(y_ref, w_ref, x_ref, g_ref, o_ref):
    t = jnp.dot(y_ref[...], w_ref[...], preferred_element_type=F32)
    o_ref[...] = x_ref[...] + g_ref[0] * t


def _out_proj(y, w_out, xall, gate, l, dims):
    n_rows, d = y.shape
    tm = _tile(dims["tile_rows"], 1024, SUBLANE_BF16)
    tn = _tile(d, 512, LANE)
    grp = functools.partial(_group_of_tile, tm=tm, bn=dims["bn"], n=dims["n"])
    return pl.pallas_call(
        _out_proj_kernel,
        out_shape=jax.ShapeDtypeStruct((n_rows, d), F32),
        grid=(n_rows // tm, d // tn),
        in_specs=[
            pl.BlockSpec((tm, d), lambda i, j: (i, 0)),
            pl.BlockSpec((None, d, tn), lambda i, j: (l, 0, j)),
            pl.BlockSpec((tm, tn), lambda i, j: (i, j)),
            pl.BlockSpec((1, 1, tn), lambda i, j: (grp(i), 0, j)),
        ],
        out_specs=pl.BlockSpec((tm, tn), lambda i, j: (i, j)),
        compiler_params=_cparams(("parallel", "arbitrary"), 48),
        name="out_proj",
    )(y, w_out, xall, gate)


def _ffn_kernel(x_ref, xp_ref, xn_ref, gam_ref, sh_ref, sc_ref, g2_ref, wv_ref, wg_ref, cwv_ref, cwg_ref,
                cbv_ref, cbg_ref, wd_ref, nf_ref, o_ref, h_ref, *, tm, bn, n, c, final_norm):
    i = pl.program_id(0)
    f = pl.program_id(1)
    halo = SUBLANE_F32

    def norm_mod(x):
        y = _rmsnorm_f32(x, gam_ref[...])
        return (y * (1.0 + sc_ref[0]) + sh_ref[0]).astype(BF16)

    @pl.when(f == 0)
    def _():
        h_ref[0:tm, :] = norm_mod(x_ref[...])
        h_ref[tm:tm + 2 * halo, :] = norm_mod(jnp.concatenate([xp_ref[...], xn_ref[...]], axis=0))
        o_ref[...] = jnp.zeros_like(o_ref)

    hv = h_ref[...]
    first, last = _segment_edges(i * tm, tm, bn, n, c)

    def conv(w_ref, cw_ref, cb_ref):
        zz = jnp.dot(hv, w_ref[...], preferred_element_type=F32)
        zm = zz[0:tm]
        zm1, zp1 = _shift_rows(zm, zz[tm + halo - 1:tm + halo], zz[tm + halo:tm + halo + 1], first, last)
        cw = cw_ref[...]
        return cw[0:1] * zm1 + cw[1:2] * zm + cw[2:3] * zp1 + cb_ref[...]

    val = conv(wv_ref, cwv_ref, cbv_ref)
    gate = conv(wg_ref, cwg_ref, cbg_ref)
    act = (gate * jax.nn.sigmoid(gate) * val).astype(BF16)
    o_ref[...] += jnp.dot(act, wd_ref[...], preferred_element_type=F32)

    @pl.when(f == pl.num_programs(1) - 1)
    def _():
        xo = x_ref[...] + g2_ref[0] * o_ref[...]
        if final_norm:
            xo = _rmsnorm_f32(xo, nf_ref[...])
        o_ref[...] = xo


def _ffn(xall, gamma, shift, scale, gate2, w_up, conv_w, conv_b, w_down, norm_final, l, n_rows, final_norm, dims):
    tt, d = xall.shape
    ff = w_down.shape[1]
    tm = _tile(dims["tile_rows"], 512, SUBLANE_BF16)
    tf = _tile(ff, 512, LANE)
    nf = ff // tf
    halo = SUBLANE_F32
    per_halo = tm // halo
    n_halo_blocks = tt // halo
    grp = functools.partial(_group_of_tile, tm=tm, bn=dims["bn"], n=dims["n"])
    mod = lambda: pl.BlockSpec((1, 1, d), lambda i, f: (grp(i), 0, 0))
    kern = functools.partial(_ffn_kernel, tm=tm, bn=dims["bn"], n=dims["n"], c=dims["c"], final_norm=final_norm)
    return pl.pallas_call(
        kern,
        out_shape=jax.ShapeDtypeStruct((n_rows, d), F32),
        grid=(n_rows // tm, nf),
        in_specs=[
            pl.BlockSpec((tm, d), lambda i, f: (i, 0)),
            pl.BlockSpec((halo, d), lambda i, f: (jnp.maximum(i * per_halo - 1, 0), 0)),
            pl.BlockSpec((halo, d), lambda i, f: (jnp.minimum((i + 1) * per_halo, n_halo_blocks - 1), 0)),
            pl.BlockSpec((None, 1, d), lambda i, f: (l, 0, 0)),
            mod(), mod(), mod(),
            pl.BlockSpec((None, d, tf), lambda i, f: (l, 0, f)),
            pl.BlockSpec((None, d, tf), lambda i, f: (l, 0, nf + f)),
            pl.BlockSpec((None, conv_w.shape[1], tf), lambda i, f: (l, 0, f)),
            pl.BlockSpec((None, conv_w.shape[1], tf), lambda i, f: (l, 0, nf + f)),
            pl.BlockSpec((None, 1, tf), lambda i, f: (l, 0, f)),
            pl.BlockSpec((None, 1, tf), lambda i, f: (l, 0, nf + f)),
            pl.BlockSpec((None, tf, d), lambda i, f: (l, f, 0)),
            pl.BlockSpec((1, d), lambda i, f: (0, 0)),
        ],
        out_specs=pl.BlockSpec((tm, d), lambda i, f: (i, 0)),
        scratch_shapes=[pltpu.VMEM((tm + 2 * halo, d), BF16)],
        compiler_params=_cparams(("parallel", "arbitrary"), 52),
        name="conv_ffn",
    )(xall, xall, xall, gamma, shift, scale, gate2, w_up, w_up, conv_w, conv_w, conv_b, conv_b, w_down,
      norm_final)


def _rope_tables(n, b, c):
    t = jnp.arange(n, dtype=jnp.int32)
    row = (t // GRID_W).astype(F32)
    col = (t % GRID_W).astype(F32)
    n_freq = MLA_ROPE_DIM // 4
    inv_freq = ROPE_THETA ** (-jnp.arange(n_freq, dtype=F32) / n_freq)
    ang = jnp.concatenate([row[:, None] * inv_freq, col[:, None] * inv_freq], axis=-1)
    cos, sin = jnp.cos(ang), jnp.sin(ang)
    zeros = jnp.zeros((n, LANE - MLA_ROPE_DIM), F32)
    ct = jnp.concatenate([cos, cos, zeros], axis=-1)
    st = jnp.concatenate([-sin, sin, zeros], axis=-1)
    ct_ctx = jnp.concatenate([jnp.ones((b * c, MLA_ROPE_DIM), F32), jnp.zeros((b * c, LANE - MLA_ROPE_DIM), F32)], -1)
    ct = jnp.concatenate([jnp.tile(ct, (b, 1)), ct_ctx], axis=0)
    st = jnp.concatenate([jnp.tile(st, (b, 1)), jnp.zeros((b * c, LANE), F32)], axis=0)
    return ct, st


def _swap_halves(w):
    half = w.shape[-1] // 2
    return jnp.concatenate([w[..., half:], w[..., :half]], axis=-1)


def kernel(x, c, ctx, c_ctx, w_mod, b_mod, norm_mix, norm_ffn, w_in, na_rpb, lru_conv_w, lru_conv_b, lru_w_a,
           lru_b_a, lru_w_x, lru_b_x, lru_lam, mla_q_norm, mla_kv_norm, mla_w_q_up, mla_w_kv_up, w_branch, w_out,
           ffn_w_up, ffn_conv_w, ffn_conv_b, ffn_w_down, norm_final):
    b, n, d = x.shape
    ctx_len = ctx.shape[1]
    depth = w_mod.shape[0]
    na_heads = na_rpb.shape[1]
    na_width = na_heads * NA_HEAD_DIM
    lru_width = lru_conv_w.shape[2]
    q_rank = mla_q_norm.shape[1]
    kv_rank = mla_kv_norm.shape[1]
    mla_heads = mla_w_q_up.shape[2] // MLA_QK_DIM
    bn, bc = b * n, b * ctx_len
    assert na_width == lru_width == mla_heads * MLA_V_DIM == w_branch.shape[2]
    assert n % (NA_KROWS * GRID_W) == 0 and n % ctx_len == 0

    off_lru_x = 3 * na_width
    off_lru_g = off_lru_x + lru_width
    off_mla = off_lru_g + lru_width
    mla_used = q_rank + kv_rank + 2 * MLA_ROPE_DIM
    mla_group = -(-mla_used // LANE) * LANE
    while off_mla % mla_group:
        mla_group += LANE
    off_gate = off_mla + mla_group
    dims = dict(b=b, n=n, c=ctx_len, d=d, bn=bn, na_heads=na_heads, lru_width=lru_width, q_rank=q_rank,
                kv_rank=kv_rank, mla_heads=mla_heads, off_lru_x=off_lru_x, off_lru_g=off_lru_g, off_mla=off_mla,
                mla_group=mla_group, off_gate=off_gate, tile_rows=int(np.gcd(n, bc)))

    o_kr = off_mla + q_rank + kv_rank
    w_in_p = jnp.concatenate([
        w_in[:, :, :o_kr + MLA_ROPE_DIM],
        _swap_halves(w_in[:, :, o_kr:o_kr + MLA_ROPE_DIM]),
        jnp.zeros((depth, d, mla_group - mla_used), F32),
        w_in[:, :, o_kr + MLA_ROPE_DIM:],
    ], axis=-1).astype(BF16)
    wq = mla_w_q_up.reshape(depth, q_rank, mla_heads, MLA_QK_DIM)
    wq_rope = wq[..., MLA_NOPE_DIM:]
    wq_p = jnp.concatenate([wq[..., :MLA_NOPE_DIM], wq_rope, _swap_halves(wq_rope)], axis=-1)
    wq_p = wq_p.reshape(depth, q_rank, mla_heads * MLA_HEAD_PAD).astype(BF16)
    wkv = mla_w_kv_up.reshape(depth, kv_rank, mla_heads, MLA_NOPE_DIM + MLA_V_DIM)
    wkv_p = jnp.concatenate([wkv[..., :MLA_NOPE_DIM].reshape(depth, kv_rank, -1),
                             wkv[..., MLA_NOPE_DIM:].reshape(depth, kv_rank, -1)], axis=-1).astype(BF16)
    w_branch_b = w_branch.astype(BF16)
    w_out_b = w_out.astype(BF16)
    w_up_b = ffn_w_up.astype(BF16)
    w_down_b = ffn_w_down.astype(BF16)
    lru_w_a_b = lru_w_a.astype(BF16)
    lru_w_x_b = lru_w_x.astype(BF16)
    ct, st = _rope_tables(n, b, ctx_len)

    n_groups = 1 + b
    pad_rows = -n_groups % SUBLANE_F32
    cond = jnp.concatenate([c_ctx[None, :], c, jnp.zeros((pad_rows, d), F32)], axis=0)
    mod = _modulation(cond, w_mod, b_mod)[:, :n_groups].reshape(depth, n_groups, N_MOD, 1, d)

    r3 = lambda a: a.reshape(a.shape[0], 1, a.shape[1])
    r4 = lambda a: a.reshape(a.shape[0], a.shape[1], 1, a.shape[2])
    norm_mix3, norm_ffn3 = r3(norm_mix), r3(norm_ffn)
    lru_conv_b3 = r3(lru_conv_b)
    lru_b_a4, lru_b_x4, lru_lam4 = r4(lru_b_a), r4(lru_b_x), r4(lru_lam)
    q_norm3, kv_norm3 = r3(mla_q_norm), r3(mla_kv_norm)
    ffn_conv_b3 = r3(ffn_conv_b)
    norm_final2 = norm_final.reshape(1, d)

    xall = jnp.concatenate([x.reshape(bn, d), ctx.reshape(bc, d)], axis=0)
    for l in range(depth):
        last = l == depth - 1
        n_rows = bn if last else bn + bc
        sh1, sc1, g1, sh2, sc2, g2 = (mod[l, :, k] for k in range(N_MOD))
        z = _in_proj(xall, norm_mix3, sh1, sc1, w_in_p, l, dims)
        bias = _na_bias_tables(na_rpb[l], n // GRID_W)
        out_a = _na_attention(z, bias, dims)
        lru_args = (lru_conv_w, lru_conv_b3, lru_w_a_b, lru_b_a4, lru_w_x_b, lru_b_x4, lru_lam4)
        h_fwd = _lru_scan(z, l, 0, *lru_args, None, dims)
        out_b = _lru_scan(z, l, 1, *lru_args, h_fwd, dims)
        q_m, k_m, v_m = _mla_proj(z, l, q_norm3, kv_norm3, wq_p, wkv_p, ct, st, dims)
        out_c = _mla_attention(q_m, k_m, v_m, dims)
        if not last:
            out_a = _ctx_attention(z, z, z, out_a, dims, na_heads, NA_HEAD_DIM, 0, na_heads, 2 * na_heads,
                                   NA_HEAD_DIM ** -0.5, "na_ctx_attention")
            out_c = _ctx_attention(q_m, k_m, v_m, out_c, dims, mla_heads, MLA_HEAD_PAD, 0, 0, 0, 1.0,
                                   "mla_ctx_attention")
        y = _merge(out_a, out_b, out_c, w_branch_b, z, l, n_rows, dims)
        xall = _out_proj(y, w_out_b, xall, g1, l, dims)
        xall = _ffn(xall, norm_ffn3, sh2, sc2, g2, w_up_b, ffn_conv_w, ffn_conv_b3, w_down_b, norm_final2, l,
                    n_rows, last, dims)
    return xall.reshape(b, n, d)
```

```python
import functools

import numpy as np
import jax
import jax.numpy as jnp
from jax import lax
from jax.experimental import pallas as pl
from jax.experimental.pallas import tpu as pltpu

F32 = jnp.float32
BF16 = jnp.bfloat16

GRID_W = 64
NORM_EPS = 1e-6
NEG_INF = -1e30
N_MOD = 6
NA_HEAD_DIM = 128
NA_WIN_H = 8
NA_WIN_W = 16
LRU_C = 8.0
MLA_NOPE_DIM = 128
MLA_ROPE_DIM = 64
MLA_V_DIM = 128
MLA_QK_DIM = MLA_NOPE_DIM + MLA_ROPE_DIM
ROPE_THETA = 10000.0

LANE = 128
SUBLANE_F32 = 8
SUBLANE_BF16 = 16
MLA_HEAD_PAD = 256
NA_QROWS = 8
NA_KROWS = 16
MIB = 1024 * 1024


def _cparams(semantics, vmem_mib):
    return pltpu.CompilerParams(dimension_semantics=semantics, vmem_limit_bytes=vmem_mib * MIB)


def _tile(n, target, mult):
    best = None
    for t in range(mult, min(n, target) + 1, mult):
        if n % t == 0:
            best = t
    assert best is not None, (n, target, mult)
    return best


def _group_of_tile(i, tm, bn, n):
    return jnp.where(i * tm >= bn, 0, 1 + (i * tm) // n)


def _rmsnorm_f32(x, gamma):
    ms = jnp.mean(x * x, axis=-1, keepdims=True)
    return x * lax.rsqrt(ms + NORM_EPS) * gamma


def _segment_edges(row0, tm, bn, n, c):
    r = row0 + lax.broadcasted_iota(jnp.int32, (tm, 1), 0)
    is_lat = r < bn
    seg = jnp.where(is_lat, n, c)
    pos = jnp.where(is_lat, r % n, (r - bn) % c)
    return pos == 0, pos == seg - 1


def _shift_rows(z, prev_row, next_row, first, last):
    tm = z.shape[0]
    t = lax.broadcasted_iota(jnp.int32, (tm, 1), 0)
    zm1 = jnp.where(t == 0, prev_row, pltpu.roll(z, 1, 0))
    zm1 = jnp.where(first, 0.0, zm1)
    zp1 = jnp.where(t == tm - 1, next_row, pltpu.roll(z, tm - 1, 0))
    zp1 = jnp.where(last, 0.0, zp1)
    return zm1, zp1


def _mod_kernel(c_ref, w_ref, b_ref, o_ref):
    cc = c_ref[...]
    s = cc * jax.nn.sigmoid(cc)
    o_ref[...] = jnp.dot(s, w_ref[...], preferred_element_type=F32,
                         precision=lax.Precision.HIGHEST) + b_ref[...]


def _modulation(cond, w_mod, b_mod):
    depth, d, nm = w_mod.shape
    rows = cond.shape[0]
    tn = _tile(nm, 1024, LANE)
    return pl.pallas_call(
        _mod_kernel,
        out_shape=jax.ShapeDtypeStruct((depth, rows, nm), F32),
        grid=(depth, nm // tn),
        in_specs=[
            pl.BlockSpec((rows, d), lambda l, j: (0, 0)),
            pl.BlockSpec((None, d, tn), lambda l, j: (l, 0, j)),
            pl.BlockSpec((None, 1, tn), lambda l, j: (l, 0, j)),
        ],
        out_specs=pl.BlockSpec((None, rows, tn), lambda l, j: (l, 0, j)),
        compiler_params=_cparams(("arbitrary", "arbitrary"), 40),
        name="adaln_mod",
    )(cond, w_mod, b_mod.reshape(depth, 1, nm))


def _in_proj_kernel(x_ref, gam_ref, sh_ref, sc_ref, w_ref, o_ref, h_ref):
    @pl.when(pl.program_id(1) == 0)
    def _():
        y = _rmsnorm_f32(x_ref[...], gam_ref[...])
        h_ref[...] = (y * (1.0 + sc_ref[0]) + sh_ref[0]).astype(BF16)

    o_ref[...] = jnp.dot(h_ref[...], w_ref[...], preferred_element_type=F32).astype(o_ref.dtype)


def _in_proj(xall, gamma, shift, scale, w, l, dims):
    tt, d = xall.shape
    npad = w.shape[2]
    tm = _tile(dims["tile_rows"], 1024, SUBLANE_BF16)
    tn = _tile(npad, 1024, LANE)
    grp = functools.partial(_group_of_tile, tm=tm, bn=dims["bn"], n=dims["n"])
    return pl.pallas_call(
        _in_proj_kernel,
        out_shape=jax.ShapeDtypeStruct((tt, npad), BF16),
        grid=(tt // tm, npad // tn),
        in_specs=[
            pl.BlockSpec((tm, d), lambda i, j: (i, 0)),
            pl.BlockSpec((None, 1, d), lambda i, j: (l, 0, 0)),
            pl.BlockSpec((1, 1, d), lambda i, j: (grp(i), 0, 0)),
            pl.BlockSpec((1, 1, d), lambda i, j: (grp(i), 0, 0)),
            pl.BlockSpec((None, d, tn), lambda i, j: (l, 0, j)),
        ],
        out_specs=pl.BlockSpec((tm, tn), lambda i, j: (i, j)),
        scratch_shapes=[pltpu.VMEM((tm, d), BF16)],
        compiler_params=_cparams(("parallel", "arbitrary"), 48),
        name="in_proj",
    )(xall, gamma, shift, scale, w)


def _na_kernel(q_ref, k_ref, v_ref, kc_ref, vc_ref, bias_ref, o_ref, *, grid_rows, scale):
    g = pl.program_id(2)
    k0 = jnp.clip(NA_QROWS * g - NA_WIN_H // 2, 0, grid_rows - NA_KROWS) * GRID_W
    k0 = pl.multiple_of(k0, GRID_W * 4)
    nk = NA_KROWS * GRID_W
    q = q_ref[...]
    kw = k_ref[pl.ds(k0, nk), :]
    vw = v_ref[pl.ds(k0, nk), :]
    nt = (((1,), (1,)), ((), ()))
    s = lax.dot_general(q, kw, nt, preferred_element_type=F32) * scale + bias_ref[...]
    sc = lax.dot_general(q, kc_ref[...], nt, preferred_element_type=F32) * scale
    m = jnp.maximum(jnp.max(s, axis=-1, keepdims=True), jnp.max(sc, axis=-1, keepdims=True))
    p = jnp.exp(s - m)
    pc = jnp.exp(sc - m)
    denom = jnp.sum(p, axis=-1, keepdims=True) + jnp.sum(pc, axis=-1, keepdims=True)
    o = jnp.dot(p.astype(BF16), vw, preferred_element_type=F32)
    o = o + jnp.dot(pc.astype(BF16), vc_ref[...], preferred_element_type=F32)
    o_ref[...] = (o * (1.0 / denom)).astype(o_ref.dtype)


def _na_bias_tables(rpb, grid_rows):
    heads = rpb.shape[0]
    n_groups = grid_rows // NA_QROWS
    variants = sorted({0, min(1, n_groups - 1), n_groups - 1})
    cidx = np.arange(GRID_W)
    c_start = np.clip(cidx - NA_WIN_W // 2, 0, GRID_W - NA_WIN_W)
    in_win = (cidx[None, :] >= c_start[:, None]) & (cidx[None, :] < c_start[:, None] + NA_WIN_W)
    col_idx = np.clip(cidx[None, :] - cidx[:, None], -(NA_WIN_W - 1), NA_WIN_W - 1) + (NA_WIN_W - 1)
    n_rel = 2 * NA_WIN_W - 1
    expand = (np.arange(n_rel)[:, None, None] == col_idx[None]) & in_win[None]
    expand = jnp.asarray(expand.reshape(n_rel, GRID_W * GRID_W), F32)
    blocks = jnp.einsum("hrd,dx->hrx", rpb.astype(F32), expand, precision=lax.Precision.HIGHEST)
    blocks = blocks + jnp.asarray(np.where(in_win, 0.0, NEG_INF).reshape(-1), F32)
    blocks = blocks.reshape(heads, 2 * NA_WIN_H - 1, GRID_W, GRID_W)
    masked = jnp.full((heads, GRID_W, GRID_W), NEG_INF, F32)
    kh = min(NA_WIN_H, grid_rows)
    tables = []
    for g in variants:
        k0 = int(np.clip(NA_QROWS * g - NA_WIN_H // 2, 0, grid_rows - NA_KROWS))
        per_query_row = []
        for qi in range(NA_QROWS):
            r = NA_QROWS * g + qi
            first_key_row = int(np.clip(r - kh // 2, 0, grid_rows - kh))
            row_blocks = []
            for kj in range(NA_KROWS):
                kr = k0 + kj
                inside = first_key_row <= kr < first_key_row + kh
                row_blocks.append(blocks[:, kr - r + NA_WIN_H - 1] if inside else masked)
            per_query_row.append(jnp.concatenate(row_blocks, axis=-1))
        tables.append(jnp.concatenate(per_query_row, axis=1))
    return jnp.stack(tables, axis=1)


def _na_attention(z, bias, dims):
    tt = z.shape[0]
    b, n, c = dims["b"], dims["n"], dims["c"]
    heads = dims["na_heads"]
    grid_rows = n // GRID_W
    n_groups = grid_rows // NA_QROWS
    tq = NA_QROWS * GRID_W
    n_var = bias.shape[1]
    ctx_blk0 = dims["bn"] // c

    def variant(g):
        if n_var == 1:
            return 0
        if n_var == 2:
            return jnp.where(g == 0, 0, 1)
        return jnp.where(g == 0, 0, jnp.where(g == n_groups - 1, 2, 1))

    kern = functools.partial(_na_kernel, grid_rows=grid_rows, scale=NA_HEAD_DIM ** -0.5)
    return pl.pallas_call(
        kern,
        out_shape=jax.ShapeDtypeStruct((tt, heads * NA_HEAD_DIM), BF16),
        grid=(b, heads, n_groups),
        in_specs=[
            pl.BlockSpec((tq, NA_HEAD_DIM), lambda bi, h, g: (bi * n_groups + g, h)),
            pl.BlockSpec((n, NA_HEAD_DIM), lambda bi, h, g: (bi, heads + h)),
            pl.BlockSpec((n, NA_HEAD_DIM), lambda bi, h, g: (bi, 2 * heads + h)),
            pl.BlockSpec((c, NA_HEAD_DIM), lambda bi, h, g: (ctx_blk0 + bi, heads + h)),
            pl.BlockSpec((c, NA_HEAD_DIM), lambda bi, h, g: (ctx_blk0 + bi, 2 * heads + h)),
            pl.BlockSpec((None, None, tq, NA_KROWS * GRID_W), lambda bi, h, g: (h, variant(g), 0, 0)),
        ],
        out_specs=pl.BlockSpec((tq, NA_HEAD_DIM), lambda bi, h, g: (bi * n_groups + g, h)),
        compiler_params=_cparams(("parallel", "parallel", "arbitrary"), 40),
        name="na_attention",
    )(z, z, z, z, z, bias)


def _ctx_attn_kernel(q_ref, k_ref, v_ref, prev_ref, o_ref, *, scale):
    del prev_ref
    nt = (((1,), (1,)), ((), ()))
    s = lax.dot_general(q_ref[...], k_ref[...], nt, preferred_element_type=F32) * scale
    m = jnp.max(s, axis=-1, keepdims=True)
    p = jnp.exp(s - m)
    denom = jnp.sum(p, axis=-1, keepdims=True)
    o = jnp.dot(p.astype(BF16), v_ref[...], preferred_element_type=F32)
    o_ref[...] = (o * (1.0 / denom)).astype(o_ref.dtype)


def _ctx_attention(q_arr, k_arr, v_arr, out_buf, dims, heads, dq, q_col0, k_col0, v_col0, v_step, scale, name):
    b, c = dims["b"], dims["c"]
    blk0 = dims["bn"] // c
    dv = out_buf.shape[1] // heads
    return pl.pallas_call(
        functools.partial(_ctx_attn_kernel, scale=scale),
        out_shape=jax.ShapeDtypeStruct(out_buf.shape, out_buf.dtype),
        grid=(b, heads),
        in_specs=[
            pl.BlockSpec((c, dq), lambda bi, h: (blk0 + bi, q_col0 + h)),
            pl.BlockSpec((c, dq), lambda bi, h: (blk0 + bi, k_col0 + h)),
            pl.BlockSpec((c, dv), lambda bi, h: (blk0 + bi, v_col0 + v_step * h)),
            pl.BlockSpec(memory_space=pl.ANY),
        ],
        out_specs=pl.BlockSpec((c, dv), lambda bi, h: (blk0 + bi, h)),
        input_output_aliases={3: 0},
        compiler_params=_cparams(("parallel", "parallel"), 32),
        name=name,
    )(q_arr, k_arr, v_arr, out_buf)


def _lru_kernel(*refs, reverse, t_chunk, n_ctx_chunks, n_lat_chunks, n_blocks, block_dim):
    if reverse:
        (x_ref, xp_ref, xn_ref, cw_ref, cb_ref, wa_ref, ba_ref, wx_ref, bx_ref, lam_ref,
         hf_ref, g_ref, o_ref, carry_ref) = refs
    else:
        (x_ref, xp_ref, xn_ref, cw_ref, cb_ref, wa_ref, ba_ref, wx_ref, bx_ref, lam_ref,
         o_ref, carry_ref) = refs
    j = pl.program_id(1)

    @pl.when(j == 0)
    def _():
        carry_ref[...] = jnp.zeros_like(carry_ref)

    is_ctx = j < n_ctx_chunks
    pos = jnp.where(is_ctx, j, j - n_ctx_chunks)
    n_seg = jnp.where(is_ctx, n_ctx_chunks, n_lat_chunks)
    chunk = (n_seg - 1 - pos) if reverse else pos
    first = chunk == 0
    last = chunk == n_seg - 1

    xm = x_ref[...].astype(F32)
    xp = xp_ref[...].astype(F32)
    xn = xn_ref[...].astype(F32)
    hp = xp.shape[0]
    pm1 = jnp.where(first, 0.0, xp[hp - 1:hp, :])
    pm2 = jnp.where(first, 0.0, xp[hp - 2:hp - 1, :])
    nn0 = jnp.where(last, 0.0, xn[0:1, :])
    t = lax.broadcasted_iota(jnp.int32, (t_chunk, 1), 0)
    xm1 = jnp.where(t == 0, pm1, pltpu.roll(xm, 1, 0))
    xm2 = jnp.where(t == 0, pm2, jnp.where(t == 1, pm1, pltpu.roll(xm, 2, 0)))
    xp1 = jnp.where(t == t_chunk - 1, nn0, pltpu.roll(xm, t_chunk - 1, 0))
    cw = cw_ref[...]
    u = cw[0:1] * xm2 + cw[1:2] * xm1 + cw[2:3] * xm + cw[3:4] * xp1 + cb_ref[...]

    ub = u.astype(BF16)
    ga, gx = [], []
    for kb in range(n_blocks):
        blk = ub[:, kb * block_dim:(kb + 1) * block_dim]
        ga.append(jnp.dot(blk, wa_ref[kb], preferred_element_type=F32))
        gx.append(jnp.dot(blk, wx_ref[kb], preferred_element_type=F32))
    gate_a = jnp.concatenate(ga, axis=1) + ba_ref[...]
    gate_x = jnp.concatenate(gx, axis=1) + bx_ref[...]
    r = jax.nn.sigmoid(gate_a)
    i_gate = jax.nn.sigmoid(gate_x)
    nlam = -lam_ref[...]
    softplus = jnp.maximum(nlam, 0.0) + jnp.log1p(jnp.exp(-jnp.abs(nlam)))
    log_a = -LRU_C * r * softplus
    a = jnp.exp(log_a)
    bv = jnp.sqrt(-jnp.tanh(log_a) * (a * a + 1.0)) * (i_gate * u)

    k = 1
    while k < t_chunk:
        if reverse:
            keep = t < t_chunk - k
            a_s = pltpu.roll(a, t_chunk - k, 0)
            b_s = pltpu.roll(bv, t_chunk - k, 0)
        else:
            keep = t >= k
            a_s = pltpu.roll(a, k, 0)
            b_s = pltpu.roll(bv, k, 0)
        bv = jnp.where(keep, a * b_s + bv, bv)
        a = jnp.where(keep, a * a_s, a)
        k *= 2
    h = a * carry_ref[0:1, :] + bv
    if reverse:
        carry_ref[0:1, :] = h[0:1, :]
        g = g_ref[...].astype(F32)
        o_ref[...] = (jax.nn.gelu(g) * (hf_ref[...] + h)).astype(o_ref.dtype)
    else:
        carry_ref[0:1, :] = h[t_chunk - 1:t_chunk, :]
        o_ref[...] = h


def _lru_scan(z, l, direction, conv_w, conv_b, w_a, b_a, w_x, b_x, lam, h_fwd, dims):
    tt = z.shape[0]
    b, n, c, lw = dims["b"], dims["n"], dims["c"], dims["lru_width"]
    n_blocks, block_dim = w_a.shape[2], w_a.shape[3]
    t_chunk = _tile(c, 256, SUBLANE_BF16)
    halo = SUBLANE_BF16
    n_ctx_chunks, n_lat_chunks = c // t_chunk, n // t_chunk
    ctx_chunk0 = dims["bn"] // t_chunk
    col_x = dims["off_lru_x"] // lw
    col_g = dims["off_lru_g"] // lw
    reverse = direction == 1
    per_halo = t_chunk // halo
    n_halo_blocks = tt // halo

    def chunk_block(bi, j):
        is_ctx = j < n_ctx_chunks
        pos = jnp.where(is_ctx, j, j - n_ctx_chunks)
        if reverse:
            pos = jnp.where(is_ctx, n_ctx_chunks - 1 - pos, n_lat_chunks - 1 - pos)
        return jnp.where(is_ctx, ctx_chunk0 + bi * n_ctx_chunks + pos, bi * n_lat_chunks + pos)

    def prev_block(bi, j):
        return jnp.maximum(chunk_block(bi, j) * per_halo - 1, 0)

    def next_block(bi, j):
        return jnp.minimum((chunk_block(bi, j) + 1) * per_halo, n_halo_blocks - 1)

    vec = lambda: pl.BlockSpec((None, None, 1, lw), lambda bi, j: (l, direction, 0, 0))
    gate_w = lambda: pl.BlockSpec((None, None, n_blocks, block_dim, block_dim),
                                  lambda bi, j: (l, direction, 0, 0, 0))
    in_specs = [
        pl.BlockSpec((t_chunk, lw), lambda bi, j: (chunk_block(bi, j), col_x)),
        pl.BlockSpec((halo, lw), lambda bi, j: (prev_block(bi, j), col_x)),
        pl.BlockSpec((halo, lw), lambda bi, j: (next_block(bi, j), col_x)),
        pl.BlockSpec((None, conv_w.shape[1], lw), lambda bi, j: (l, 0, 0)),
        pl.BlockSpec((None, 1, lw), lambda bi, j: (l, 0, 0)),
        gate_w(), vec(), gate_w(), vec(), vec(),
    ]
    args = [z, z, z, conv_w, conv_b, w_a, b_a, w_x, b_x, lam]
    if reverse:
        in_specs += [
            pl.BlockSpec((t_chunk, lw), lambda bi, j: (chunk_block(bi, j), 0)),
            pl.BlockSpec((t_chunk, lw), lambda bi, j: (chunk_block(bi, j), col_g)),
        ]
        args += [h_fwd, z]
        out_dtype = BF16
    else:
        out_dtype = F32
    kern = functools.partial(_lru_kernel, reverse=reverse, t_chunk=t_chunk, n_ctx_chunks=n_ctx_chunks,
                             n_lat_chunks=n_lat_chunks, n_blocks=n_blocks, block_dim=block_dim)
    return pl.pallas_call(
        kern,
        out_shape=jax.ShapeDtypeStruct((tt, lw), out_dtype),
        grid=(b, n_ctx_chunks + n_lat_chunks),
        in_specs=in_specs,
        out_specs=pl.BlockSpec((t_chunk, lw), lambda bi, j: (chunk_block(bi, j), 0)),
        scratch_shapes=[pltpu.VMEM((SUBLANE_F32, lw), F32)],
        compiler_params=_cparams(("parallel", "arbitrary"), 40),
        name="lru_bwd" if reverse else "lru_fwd",
    )(*args)


def _mla_proj_kernel(z_ref, qn_ref, kvn_ref, wq_ref, wkv_ref, ct_ref, st_ref, q_ref, k_ref, v_ref,
                     *, q_rank, kv_rank, heads, q_scale):
    z = z_ref[...]
    cq = z[:, :q_rank].astype(F32)
    ckv = z[:, q_rank:q_rank + kv_rank].astype(F32)
    krp = z[:, q_rank + kv_rank:q_rank + kv_rank + LANE].astype(F32)
    ct = ct_ref[...]
    st = st_ref[...]

    def rope(tile):
        return tile * ct + pltpu.roll(tile, LANE // 2, 1) * st

    qn = _rmsnorm_f32(cq, qn_ref[...]).astype(BF16)
    q = jnp.dot(qn, wq_ref[...], preferred_element_type=F32)
    kvn = _rmsnorm_f32(ckv, kvn_ref[...]).astype(BF16)
    kv = jnp.dot(kvn, wkv_ref[...], preferred_element_type=F32)
    kr = rope(krp).astype(BF16)
    for h in range(heads):
        c0 = h * MLA_HEAD_PAD
        q_ref[:, c0:c0 + LANE] = (q[:, c0:c0 + LANE] * q_scale).astype(BF16)
        q_ref[:, c0 + LANE:c0 + 2 * LANE] = (rope(q[:, c0 + LANE:c0 + 2 * LANE]) * q_scale).astype(BF16)
        k_ref[:, c0:c0 + LANE] = kv[:, h * LANE:(h + 1) * LANE].astype(BF16)
        k_ref[:, c0 + LANE:c0 + 2 * LANE] = kr
        v_ref[:, c0:c0 + LANE] = kv[:, (heads + h) * LANE:(heads + h + 1) * LANE].astype(BF16)
        v_ref[:, c0 + LANE:c0 + 2 * LANE] = jnp.ones((z.shape[0], LANE), BF16)


def _mla_proj(z, l, q_norm, kv_norm, wq, wkv, ct, st, dims):
    tt = z.shape[0]
    heads, q_rank, kv_rank = dims["mla_heads"], dims["q_rank"], dims["kv_rank"]
    mg = dims["mla_group"]
    tm = _tile(dims["tile_rows"], 512, SUBLANE_BF16)
    col = dims["off_mla"] // mg
    kern = functools.partial(_mla_proj_kernel, q_rank=q_rank, kv_rank=kv_rank, heads=heads,
                             q_scale=MLA_QK_DIM ** -0.5)
    return pl.pallas_call(
        kern,
        out_shape=(jax.ShapeDtypeStruct((tt, heads * MLA_HEAD_PAD), BF16),
                   jax.ShapeDtypeStruct((tt, heads * MLA_HEAD_PAD), BF16),
                   jax.ShapeDtypeStruct((tt, heads * MLA_HEAD_PAD), BF16)),
        grid=(tt // tm,),
        in_specs=[
            pl.BlockSpec((tm, mg), lambda i: (i, col)),
            pl.BlockSpec((None, 1, q_rank), lambda i: (l, 0, 0)),
            pl.BlockSpec((None, 1, kv_rank), lambda i: (l, 0, 0)),
            pl.BlockSpec((None, q_rank, heads * MLA_HEAD_PAD), lambda i: (l, 0, 0)),
            pl.BlockSpec((None, kv_rank, heads * 2 * LANE), lambda i: (l, 0, 0)),
            pl.BlockSpec((tm, LANE), lambda i: (i, 0)),
            pl.BlockSpec((tm, LANE), lambda i: (i, 0)),
        ],
        out_specs=(pl.BlockSpec((tm, heads * MLA_HEAD_PAD), lambda i: (i, 0)),
                   pl.BlockSpec((tm, heads * MLA_HEAD_PAD), lambda i: (i, 0)),
                   pl.BlockSpec((tm, heads * MLA_HEAD_PAD), lambda i: (i, 0))),
        compiler_params=_cparams(("parallel",), 48),
        name="mla_proj",
    )(z, q_norm, kv_norm, wq, wkv, ct, st)


def _mla_flash_kernel(q_ref, kt_ref, v_ref, ktc_ref, vc_ref, o_ref, *, sub, tk):
    n_chunks = kt_ref.shape[1] // tk
    chunks = [None] + list(range(n_chunks))

    def scores(q, ch):
        kt = ktc_ref[...] if ch is None else kt_ref[:, ch * tk:(ch + 1) * tk]
        return jnp.dot(q, kt, preferred_element_type=F32)

    def values(ch):
        return vc_ref[...] if ch is None else v_ref[ch * tk:(ch + 1) * tk, :]

    for a in range(q_ref.shape[0] // sub):
        rows = slice(a * sub, (a + 1) * sub)
        q = q_ref[rows, :]
        s_next = scores(q, chunks[0])
        m = acc = None
        for idx, ch in enumerate(chunks):
            s = s_next
            if idx + 1 < len(chunks):
                s_next = scores(q, chunks[idx + 1])
            m_new = jnp.max(s, axis=-1, keepdims=True)
            if m is not None:
                m_new = jnp.maximum(m, m_new)
            p = jnp.exp(s - m_new).astype(BF16)
            pv = jnp.dot(p, values(ch), preferred_element_type=F32)
            acc = pv if m is None else jnp.exp(m - m_new) * acc + pv
            m = m_new
        o_ref[rows, :] = (acc[:, :MLA_V_DIM] * (1.0 / acc[:, MLA_V_DIM:])).astype(o_ref.dtype)


def _mla_attention(q, kt, v, dims):
    tt = q.shape[0]
    b, n, c, heads = dims["b"], dims["n"], dims["c"], dims["mla_heads"]
    tq = _tile(n, 1024, SUBLANE_BF16)
    sub = _tile(tq, 512, SUBLANE_BF16)
    ctx_blk0 = dims["bn"] // c
    nq = n // tq
    vw = 2 * MLA_V_DIM
    return pl.pallas_call(
        functools.partial(_mla_flash_kernel, sub=sub, tk=_tile(n, 2048, LANE)),
        out_shape=jax.ShapeDtypeStruct((tt, heads * MLA_V_DIM), BF16),
        grid=(b, heads, nq),
        in_specs=[
            pl.BlockSpec((tq, MLA_HEAD_PAD), lambda bi, h, i: (bi * nq + i, h)),
            pl.BlockSpec((None, MLA_HEAD_PAD, n), lambda bi, h, i: (h, 0, bi)),
            pl.BlockSpec((n, vw), lambda bi, h, i: (bi, h)),
            pl.BlockSpec((None, MLA_HEAD_PAD, c), lambda bi, h, i: (h, 0, ctx_blk0 + bi)),
            pl.BlockSpec((c, vw), lambda bi, h, i: (ctx_blk0 + bi, h)),
        ],
        out_specs=pl.BlockSpec((tq, MLA_V_DIM), lambda bi, h, i: (bi * nq + i, h)),
        compiler_params=_cparams(("parallel", "parallel", "arbitrary"), 48),
        name="mla_attention",
    )(q, kt, v, kt, v)


def _merge_kernel(ba_ref, bb_ref, bc_ref, wb_ref, ga_ref, gb_ref, gc_ref, o_ref):
    y = None
    for br, gate, i in ((ba_ref, ga_ref, 0), (bb_ref, gb_ref, 1), (bc_ref, gc_ref, 2)):
        t = jnp.dot(br[...], wb_ref[i], preferred_element_type=F32)
        t = jax.nn.sigmoid(gate[...].astype(F32)) * t
        y = t if y is None else y + t
    o_ref[...] = y.astype(o_ref.dtype)


def _merge(br_a, br_b, br_c, w_branch, z, l, n_rows, dims):
    d = dims["d"]
    bw = br_a.shape[1]
    tm = _tile(dims["tile_rows"], 1024, SUBLANE_BF16)
    tn = _tile(d, 512, LANE)
    gate_col0 = dims["off_gate"] // tn
    per_gate = d // tn
    branch = lambda: pl.BlockSpec((tm, bw), lambda i, j: (i, 0))
    gate = lambda k: pl.BlockSpec((tm, tn), lambda i, j: (i, gate_col0 + k * per_gate + j))
    return pl.pallas_call(
        _merge_kernel,
        out_shape=jax.ShapeDtypeStruct((n_rows, d), BF16),
        grid=(n_rows // tm, d // tn),
        in_specs=[branch(), branch(), branch(),
                  pl.BlockSpec((None, 3, bw, tn), lambda i, j: (l, 0, 0, j)),
                  gate(0), gate(1), gate(2)],
        out_specs=pl.BlockSpec((tm, tn), lambda i, j: (i, j)),
        compiler_params=_cparams(("parallel", "arbitrary"), 48),
        name="merge",
    )(br_a, br_b, br_c, w_branch, z, z, z)


def _out_proj_kernel(y_ref, w_ref, x_ref, g_ref, o_ref):
    t = jnp.dot(y_ref[...], w_ref[...], preferred_element_type=F32)
    o_ref[...] = x_ref[...] + g_ref[0] * t


def _out_proj(y, w_out, xall, gate, l, dims):
    n_rows, d = y.shape
    tm = _tile(dims["tile_rows"], 1024, SUBLANE_BF16)
    tn = _tile(d, 512, LANE)
    grp = functools.partial(_group_of_tile, tm=tm, bn=dims["bn"], n=dims["n"])
    return pl.pallas_call(
        _out_proj_kernel,
        out_shape=jax.ShapeDtypeStruct((n_rows, d), F32),
        grid=(n_rows // tm, d // tn),
        in_specs=[
            pl.BlockSpec((tm, d), lambda i, j: (i, 0)),
            pl.BlockSpec((None, d, tn), lambda i, j: (l, 0, j)),
            pl.BlockSpec((tm, tn), lambda i, j: (i, j)),
            pl.BlockSpec((1, 1, tn), lambda i, j: (grp(i), 0, j)),
        ],
        out_specs=pl.BlockSpec((tm, tn), lambda i, j: (i, j)),
        compiler_params=_cparams(("parallel", "arbitrary"), 48),
        name="out_proj",
    )(y, w_out, xall, gate)


def _ffn_kernel(x_ref, xp_ref, xn_ref, gam_ref, sh_ref, sc_ref, g2_ref, wv_ref, wg_ref, cwv_ref, cwg_ref,
                cbv_ref, cbg_ref, wd_ref, nf_ref, o_ref, h_ref, *, tm, bn, n, c, final_norm):
    i = pl.program_id(0)
    f = pl.program_id(1)
    halo = SUBLANE_F32

    def norm_mod(x):
        y = _rmsnorm_f32(x, gam_ref[...])
        return (y * (1.0 + sc_ref[0]) + sh_ref[0]).astype(BF16)

    @pl.when(f == 0)
    def _():
        h_ref[0:tm, :] = norm_mod(x_ref[...])
        h_ref[tm:tm + 2 * halo, :] = norm_mod(jnp.concatenate([xp_ref[...], xn_ref[...]], axis=0))
        o_ref[...] = jnp.zeros_like(o_ref)

    hv = h_ref[...]
    first, last = _segment_edges(i * tm, tm, bn, n, c)

    def conv(w_ref, cw_ref, cb_ref):
        zz = jnp.dot(hv, w_ref[...], preferred_element_type=F32)
        zm = zz[0:tm]
        zm1, zp1 = _shift_rows(zm, zz[tm + halo - 1:tm + halo], zz[tm + halo:tm + halo + 1], first, last)
        cw = cw_ref[...]
        return cw[0:1] * zm1 + cw[1:2] * zm + cw[2:3] * zp1 + cb_ref[...]

    val = conv(wv_ref, cwv_ref, cbv_ref)
    gate = conv(wg_ref, cwg_ref, cbg_ref)
    act = (gate * jax.nn.sigmoid(gate) * val).astype(BF16)
    o_ref[...] += jnp.dot(act, wd_ref[...], preferred_element_type=F32)

    @pl.when(f == pl.num_programs(1) - 1)
    def _():
        xo = x_ref[...] + g2_ref[0] * o_ref[...]
        if final_norm:
            xo = _rmsnorm_f32(xo, nf_ref[...])
        o_ref[...] = xo


def _ffn(xall, gamma, shift, scale, gate2, w_up, conv_w, conv_b, w_down, norm_final, l, n_rows, final_norm, dims):
    tt, d = xall.shape
    ff = w_down.shape[1]
    tm = _tile(dims["tile_rows"], 512, SUBLANE_BF16)
    tf = _tile(ff, 512, LANE)
    nf = ff // tf
    halo = SUBLANE_F32
    per_halo = tm // halo
    n_halo_blocks = tt // halo
    grp = functools.partial(_group_of_tile, tm=tm, bn=dims["bn"], n=dims["n"])
    mod = lambda: pl.BlockSpec((1, 1, d), lambda i, f: (grp(i), 0, 0))
    kern = functools.partial(_ffn_kernel, tm=tm, bn=dims["bn"], n=dims["n"], c=dims["c"], final_norm=final_norm)
    return pl.pallas_call(
        kern,
        out_shape=jax.ShapeDtypeStruct((n_rows, d), F32),
        grid=(n_rows // tm, nf),
        in_specs=[
            pl.BlockSpec((tm, d), lambda i, f: (i, 0)),
            pl.BlockSpec((halo, d), lambda i, f: (jnp.maximum(i * per_halo - 1, 0), 0)),
            pl.BlockSpec((halo, d), lambda i, f: (jnp.minimum((i + 1) * per_halo, n_halo_blocks - 1), 0)),
            pl.BlockSpec((None, 1, d), lambda i, f: (l, 0, 0)),
            mod(), mod(), mod(),
            pl.BlockSpec((None, d, tf), lambda i, f: (l, 0, f)),
            pl.BlockSpec((None, d, tf), lambda i, f: (l, 0, nf + f)),
            pl.BlockSpec((None, conv_w.shape[1], tf), lambda i, f: (l, 0, f)),
            pl.BlockSpec((None, conv_w.shape[1], tf), lambda i, f: (l, 0, nf + f)),
            pl.BlockSpec((None, 1, tf), lambda i, f: (l, 0, f)),
            pl.BlockSpec((None, 1, tf), lambda i, f: (l, 0, nf + f)),
            pl.BlockSpec((None, tf, d), lambda i, f: (l, f, 0)),
            pl.BlockSpec((1, d), lambda i, f: (0, 0)),
        ],
        out_specs=pl.BlockSpec((tm, d), lambda i, f: (i, 0)),
        scratch_shapes=[pltpu.VMEM((tm + 2 * halo, d), BF16)],
        compiler_params=_cparams(("parallel", "arbitrary"), 52),
        name="conv_ffn",
    )(xall, xall, xall, gamma, shift, scale, gate2, w_up, w_up, conv_w, conv_w, conv_b, conv_b, w_down,
      norm_final)


def _rope_tables(n, b, c):
    t = jnp.arange(n, dtype=jnp.int32)
    row = (t // GRID_W).astype(F32)
    col = (t % GRID_W).astype(F32)
    n_freq = MLA_ROPE_DIM // 4
    inv_freq = ROPE_THETA ** (-jnp.arange(n_freq, dtype=F32) / n_freq)
    ang = jnp.concatenate([row[:, None] * inv_freq, col[:, None] * inv_freq], axis=-1)
    cos, sin = jnp.cos(ang), jnp.sin(ang)
    zeros = jnp.zeros((n, LANE - MLA_ROPE_DIM), F32)
    ct = jnp.concatenate([cos, cos, zeros], axis=-1)
    st = jnp.concatenate([-sin, sin, zeros], axis=-1)
    ct_ctx = jnp.concatenate([jnp.ones((b * c, MLA_ROPE_DIM), F32), jnp.zeros((b * c, LANE - MLA_ROPE_DIM), F32)], -1)
    ct = jnp.concatenate([jnp.tile(ct, (b, 1)), ct_ctx], axis=0)
    st = jnp.concatenate([jnp.tile(st, (b, 1)), jnp.zeros((b * c, LANE), F32)], axis=0)
    return ct, st


def _swap_halves(w):
    half = w.shape[-1] // 2
    return jnp.concatenate([w[..., half:], w[..., :half]], axis=-1)


def kernel(x, c, ctx, c_ctx, w_mod, b_mod, norm_mix, norm_ffn, w_in, na_rpb, lru_conv_w, lru_conv_b, lru_w_a,
           lru_b_a, lru_w_x, lru_b_x, lru_lam, mla_q_norm, mla_kv_norm, mla_w_q_up, mla_w_kv_up, w_branch, w_out,
           ffn_w_up, ffn_conv_w, ffn_conv_b, ffn_w_down, norm_final):
    b, n, d = x.shape
    ctx_len = ctx.shape[1]
    depth = w_mod.shape[0]
    na_heads = na_rpb.shape[1]
    na_width = na_heads * NA_HEAD_DIM
    lru_width = lru_conv_w.shape[2]
    q_rank = mla_q_norm.shape[1]
    kv_rank = mla_kv_norm.shape[1]
    mla_heads = mla_w_q_up.shape[2] // MLA_QK_DIM
    bn, bc = b * n, b * ctx_len
    assert na_width == lru_width == mla_heads * MLA_V_DIM == w_branch.shape[2]
    assert n % (NA_KROWS * GRID_W) == 0 and n % ctx_len == 0

    off_lru_x = 3 * na_width
    off_lru_g = off_lru_x + lru_width
    off_mla = off_lru_g + lru_width
    mla_used = q_rank + kv_rank + 2 * MLA_ROPE_DIM
    mla_group = -(-mla_used // LANE) * LANE
    while off_mla % mla_group:
        mla_group += LANE
    off_gate = off_mla + mla_group
    dims = dict(b=b, n=n, c=ctx_len, d=d, bn=bn, na_heads=na_heads, lru_width=lru_width, q_rank=q_rank,
                kv_rank=kv_rank, mla_heads=mla_heads, off_lru_x=off_lru_x, off_lru_g=off_lru_g, off_mla=off_mla,
                mla_group=mla_group, off_gate=off_gate, tile_rows=int(np.gcd(n, bc)))

    o_kr = off_mla + q_rank + kv_rank
    w_in_p = jnp.concatenate([
        w_in[:, :, :o_kr + MLA_ROPE_DIM],
        _swap_halves(w_in[:, :, o_kr:o_kr + MLA_ROPE_DIM]),
        jnp.zeros((depth, d, mla_group - mla_used), F32),
        w_in[:, :, o_kr + MLA_ROPE_DIM:],
    ], axis=-1).astype(BF16)
    wq = mla_w_q_up.reshape(depth, q_rank, mla_heads, MLA_QK_DIM)
    wq_rope = wq[..., MLA_NOPE_DIM:]
    wq_p = jnp.concatenate([wq[..., :MLA_NOPE_DIM], wq_rope, _swap_halves(wq_rope)], axis=-1)
    wq_p = wq_p.reshape(depth, q_rank, mla_heads * MLA_HEAD_PAD).astype(BF16)
    wkv = mla_w_kv_up.reshape(depth, kv_rank, mla_heads, MLA_NOPE_DIM + MLA_V_DIM)
    wkv_p = jnp.concatenate([wkv[..., :MLA_NOPE_DIM].reshape(depth, kv_rank, -1),
                             wkv[..., MLA_NOPE_DIM:].reshape(depth, kv_rank, -1)], axis=-1).astype(BF16)
    w_branch_b = w_branch.astype(BF16)
    w_out_b = w_out.astype(BF16)
    w_up_b = ffn_w_up.astype(BF16)
    w_down_b = ffn_w_down.astype(BF16)
    lru_w_a_b = lru_w_a.astype(BF16)
    lru_w_x_b = lru_w_x.astype(BF16)
    ct, st = _rope_tables(n, b, ctx_len)

    n_groups = 1 + b
    pad_rows = -n_groups % SUBLANE_F32
    cond = jnp.concatenate([c_ctx[None, :], c, jnp.zeros((pad_rows, d), F32)], axis=0)
    mod = _modulation(cond, w_mod, b_mod)[:, :n_groups].reshape(depth, n_groups, N_MOD, 1, d)

    r3 = lambda a: a.reshape(a.shape[0], 1, a.shape[1])
    r4 = lambda a: a.reshape(a.shape[0], a.shape[1], 1, a.shape[2])
    norm_mix3, norm_ffn3 = r3(norm_mix), r3(norm_ffn)
    lru_conv_b3 = r3(lru_conv_b)
    lru_b_a4, lru_b_x4, lru_lam4 = r4(lru_b_a), r4(lru_b_x), r4(lru_lam)
    q_norm3, kv_norm3 = r3(mla_q_norm), r3(mla_kv_norm)
    ffn_conv_b3 = r3(ffn_conv_b)
    norm_final2 = norm_final.reshape(1, d)

    xall = jnp.concatenate([x.reshape(bn, d), ctx.reshape(bc, d)], axis=0)
    for l in range(depth):
        last = l == depth - 1
        n_rows = bn if last else bn + bc
        sh1, sc1, g1, sh2, sc2, g2 = (mod[l, :, k] for k in range(N_MOD))
        z = _in_proj(xall, norm_mix3, sh1, sc1, w_in_p, l, dims)
        bias = _na_bias_tables(na_rpb[l], n // GRID_W)
        out_a = _na_attention(z, bias, dims)
        lru_args = (lru_conv_w, lru_conv_b3, lru_w_a_b, lru_b_a4, lru_w_x_b, lru_b_x4, lru_lam4)
        h_fwd = _lru_scan(z, l, 0, *lru_args, None, dims)
        out_b = _lru_scan(z, l, 1, *lru_args, h_fwd, dims)
        q_m, k_m, v_m = _mla_proj(z, l, q_norm3, kv_norm3, wq_p, wkv_p, ct, st, dims)
        kt_m = k_m.reshape(bn + bc, mla_heads, MLA_HEAD_PAD).transpose(1, 2, 0)
        out_c = _mla_attention(q_m, kt_m, v_m, dims)
        if not last:
            out_a = _ctx_attention(z, z, z, out_a, dims, na_heads, NA_HEAD_DIM, 0, na_heads, 2 * na_heads, 1,
                                   NA_HEAD_DIM ** -0.5, "na_ctx_attention")
            out_c = _ctx_attention(q_m, k_m, v_m, out_c, dims, mla_heads, MLA_HEAD_PAD, 0, 0, 0, 2, 1.0,
                                   "mla_ctx_attention")
        y = _merge(out_a, out_b, out_c, w_branch_b, z, l, n_rows, dims)
        xall = _out_proj(y, w_out_b, xall, g1, l, dims)
        xall = _ffn(xall, norm_ffn3, sh2, sc2, g2, w_up_b, ffn_conv_w, ffn_conv_b3, w_down_b, norm_final2, l,
                    n_rows, last, dims)
    return xall.reshape(b, n, d)
```

```python
import functools

import numpy as np
import jax
import jax.numpy as jnp
from jax import lax
from jax.experimental import pallas as pl
from jax.experimental.pallas import tpu as pltpu

F32 = jnp.float32
BF16 = jnp.bfloat16

GRID_W = 64
NORM_EPS = 1e-6
NEG_INF = -1e30
N_MOD = 6
NA_HEAD_DIM = 128
NA_WIN_H = 8
NA_WIN_W = 16
LRU_C = 8.0
MLA_NOPE_DIM = 128
MLA_ROPE_DIM = 64
MLA_V_DIM = 128
MLA_QK_DIM = MLA_NOPE_DIM + MLA_ROPE_DIM
ROPE_THETA = 10000.0

LANE = 128
SUBLANE_F32 = 8
SUBLANE_BF16 = 16
MLA_HEAD_PAD = 256
NA_QROWS = 8
NA_KROWS = 16
MIB = 1024 * 1024


def _cparams(semantics, vmem_mib):
    return pltpu.CompilerParams(dimension_semantics=semantics, vmem_limit_bytes=vmem_mib * MIB)


def _tile(n, target, mult):
    best = None
    for t in range(mult, min(n, target) + 1, mult):
        if n % t == 0:
            best = t
    assert best is not None, (n, target, mult)
    return best


def _group_of_tile(i, tm, bn, n):
    return jnp.where(i * tm >= bn, 0, 1 + (i * tm) // n)


def _rmsnorm_f32(x, gamma):
    ms = jnp.mean(x * x, axis=-1, keepdims=True)
    return x * lax.rsqrt(ms + NORM_EPS) * gamma


def _mod_kernel(c_ref, w_ref, b_ref, o_ref):
    cc = c_ref[...]
    s = cc * jax.nn.sigmoid(cc)
    o_ref[...] = jnp.dot(s, w_ref[...], preferred_element_type=F32,
                         precision=lax.Precision.HIGHEST) + b_ref[...]


def _modulation(cond, w_mod, b_mod):
    depth, d, nm = w_mod.shape
    rows = cond.shape[0]
    tn = _tile(nm, 1024, LANE)
    return pl.pallas_call(
        _mod_kernel,
        out_shape=jax.ShapeDtypeStruct((depth, rows, nm), F32),
        grid=(depth, nm // tn),
        in_specs=[
            pl.BlockSpec((rows, d), lambda l, j: (0, 0)),
            pl.BlockSpec((None, d, tn), lambda l, j: (l, 0, j)),
            pl.BlockSpec((None, 1, tn), lambda l, j: (l, 0, j)),
        ],
        out_specs=pl.BlockSpec((None, rows, tn), lambda l, j: (l, 0, j)),
        compiler_params=_cparams(("arbitrary", "arbitrary"), 40),
        name="adaln_mod",
    )(cond, w_mod, b_mod.reshape(depth, 1, nm))


def _in_proj_kernel(x_ref, gam_ref, sh_ref, sc_ref, w_ref, o_ref, h_ref):
    @pl.when(pl.program_id(1) == 0)
    def _():
        y = _rmsnorm_f32(x_ref[...], gam_ref[...])
        h_ref[...] = (y * (1.0 + sc_ref[0]) + sh_ref[0]).astype(BF16)

    o_ref[...] = jnp.dot(h_ref[...], w_ref[...], preferred_element_type=F32).astype(o_ref.dtype)


def _in_proj(xall, gamma, shift, scale, w, l, dims):
    tt, d = xall.shape
    npad = w.shape[2]
    tm = _tile(dims["tile_rows"], 1024, SUBLANE_BF16)
    tn = _tile(npad, 1024, LANE)
    grp = functools.partial(_group_of_tile, tm=tm, bn=dims["bn"], n=dims["n"])
    return pl.pallas_call(
        _in_proj_kernel,
        out_shape=jax.ShapeDtypeStruct((tt, npad), BF16),
        grid=(tt // tm, npad // tn),
        in_specs=[
            pl.BlockSpec((tm, d), lambda i, j: (i, 0)),
            pl.BlockSpec((None, 1, d), lambda i, j: (l, 0, 0)),
            pl.BlockSpec((1, 1, d), lambda i, j: (grp(i), 0, 0)),
            pl.BlockSpec((1, 1, d), lambda i, j: (grp(i), 0, 0)),
            pl.BlockSpec((None, d, tn), lambda i, j: (l, 0, j)),
        ],
        out_specs=pl.BlockSpec((tm, tn), lambda i, j: (i, j)),
        scratch_shapes=[pltpu.VMEM((tm, d), BF16)],
        compiler_params=_cparams(("parallel", "arbitrary"), 48),
        name="in_proj",
    )(xall, gamma, shift, scale, w)


def _na_variants(n_groups):
    return sorted({0, min(1, n_groups - 1), n_groups - 1})


def _na_kernel(q_ref, k_ref, v_ref, kc_ref, vc_ref, bias_ref, o_ref, *, grid_rows, scale):
    n_groups = grid_rows // NA_QROWS
    variants = _na_variants(n_groups)
    tq = NA_QROWS * GRID_W
    nk = NA_KROWS * GRID_W
    nt = (((1,), (1,)), ((), ()))

    def key_start(g):
        return int(np.clip(NA_QROWS * g - NA_WIN_H // 2, 0, grid_rows - NA_KROWS)) * GRID_W

    def scores(g):
        q = q_ref[g * tq:(g + 1) * tq, :]
        k0 = key_start(g)
        var = variants.index(g) if g in variants else 1
        s = lax.dot_general(q, k_ref[k0:k0 + nk, :], nt, preferred_element_type=F32) * scale + bias_ref[var]
        sc = lax.dot_general(q, kc_ref[...], nt, preferred_element_type=F32) * scale
        return s, sc

    nxt = scores(0)
    for g in range(n_groups):
        s, sc = nxt
        if g + 1 < n_groups:
            nxt = scores(g + 1)
        k0 = key_start(g)
        m = jnp.maximum(jnp.max(s, axis=-1, keepdims=True), jnp.max(sc, axis=-1, keepdims=True))
        p = jnp.exp(s - m)
        pc = jnp.exp(sc - m)
        denom = jnp.sum(p, axis=-1, keepdims=True) + jnp.sum(pc, axis=-1, keepdims=True)
        o = jnp.dot(p.astype(BF16), v_ref[k0:k0 + nk, :], preferred_element_type=F32)
        o = o + jnp.dot(pc.astype(BF16), vc_ref[...], preferred_element_type=F32)
        o_ref[g * tq:(g + 1) * tq, :] = (o * (1.0 / denom)).astype(o_ref.dtype)


def _na_bias_tables(rpb, grid_rows):
    lead = rpb.shape[:2]
    n_groups = grid_rows // NA_QROWS
    variants = _na_variants(n_groups)
    cidx = np.arange(GRID_W)
    c_start = np.clip(cidx - NA_WIN_W // 2, 0, GRID_W - NA_WIN_W)
    in_win = (cidx[None, :] >= c_start[:, None]) & (cidx[None, :] < c_start[:, None] + NA_WIN_W)
    col_idx = np.clip(cidx[None, :] - cidx[:, None], -(NA_WIN_W - 1), NA_WIN_W - 1) + (NA_WIN_W - 1)
    n_rel = 2 * NA_WIN_W - 1
    expand = (np.arange(n_rel)[:, None, None] == col_idx[None]) & in_win[None]
    expand = jnp.asarray(expand.reshape(n_rel, GRID_W * GRID_W), F32)
    blocks = jnp.einsum("lhrd,dx->lhrx", rpb.astype(F32), expand, precision=lax.Precision.HIGHEST)
    blocks = blocks + jnp.asarray(np.where(in_win, 0.0, NEG_INF).reshape(-1), F32)
    blocks = blocks.reshape(*lead, 2 * NA_WIN_H - 1, GRID_W, GRID_W)
    masked = jnp.full((*lead, GRID_W, GRID_W), NEG_INF, F32)
    kh = min(NA_WIN_H, grid_rows)
    tables = []
    for g in variants:
        k0 = int(np.clip(NA_QROWS * g - NA_WIN_H // 2, 0, grid_rows - NA_KROWS))
        per_query_row = []
        for qi in range(NA_QROWS):
            r = NA_QROWS * g + qi
            first_key_row = int(np.clip(r - kh // 2, 0, grid_rows - kh))
            row_blocks = []
            for kj in range(NA_KROWS):
                kr = k0 + kj
                inside = first_key_row <= kr < first_key_row + kh
                row_blocks.append(blocks[:, :, kr - r + NA_WIN_H - 1] if inside else masked)
            per_query_row.append(jnp.concatenate(row_blocks, axis=-1))
        tables.append(jnp.concatenate(per_query_row, axis=2))
    return jnp.stack(tables, axis=2)


def _na_attention(z, bias, l, dims):
    tt = z.shape[0]
    b, n, c = dims["b"], dims["n"], dims["c"]
    heads = dims["na_heads"]
    n_var = bias.shape[2]
    ctx_blk0 = dims["bn"] // c
    kern = functools.partial(_na_kernel, grid_rows=n // GRID_W, scale=NA_HEAD_DIM ** -0.5)
    return pl.pallas_call(
        kern,
        out_shape=jax.ShapeDtypeStruct((tt, heads * NA_HEAD_DIM), BF16),
        grid=(heads, b),
        in_specs=[
            pl.BlockSpec((n, NA_HEAD_DIM), lambda h, bi: (bi, h)),
            pl.BlockSpec((n, NA_HEAD_DIM), lambda h, bi: (bi, heads + h)),
            pl.BlockSpec((n, NA_HEAD_DIM), lambda h, bi: (bi, 2 * heads + h)),
            pl.BlockSpec((c, NA_HEAD_DIM), lambda h, bi: (ctx_blk0 + bi, heads + h)),
            pl.BlockSpec((c, NA_HEAD_DIM), lambda h, bi: (ctx_blk0 + bi, 2 * heads + h)),
            pl.BlockSpec((None, None, n_var, NA_QROWS * GRID_W, NA_KROWS * GRID_W),
                         lambda h, bi: (l, h, 0, 0, 0)),
        ],
        out_specs=pl.BlockSpec((n, NA_HEAD_DIM), lambda h, bi: (bi, h)),
        compiler_params=_cparams(("parallel", "parallel"), 48),
        name="na_attention",
    )(z, z, z, z, z, bias)


def _ctx_attn_kernel(q_ref, k_ref, v_ref, prev_ref, o_ref, *, scale):
    del prev_ref
    nt = (((1,), (1,)), ((), ()))
    s = lax.dot_general(q_ref[...], k_ref[...], nt, preferred_element_type=F32) * scale
    m = jnp.max(s, axis=-1, keepdims=True)
    p = jnp.exp(s - m)
    denom = jnp.sum(p, axis=-1, keepdims=True)
    o = jnp.dot(p.astype(BF16), v_ref[...], preferred_element_type=F32)
    o_ref[...] = (o * (1.0 / denom)).astype(o_ref.dtype)


def _ctx_attention(q_arr, k_arr, v_arr, out_buf, dims, heads, dq, q_col0, k_col0, v_col0, v_step, scale, name):
    b, c = dims["b"], dims["c"]
    blk0 = dims["bn"] // c
    dv = out_buf.shape[1] // heads
    return pl.pallas_call(
        functools.partial(_ctx_attn_kernel, scale=scale),
        out_shape=jax.ShapeDtypeStruct(out_buf.shape, out_buf.dtype),
        grid=(b, heads),
        in_specs=[
            pl.BlockSpec((c, dq), lambda bi, h: (blk0 + bi, q_col0 + h)),
            pl.BlockSpec((c, dq), lambda bi, h: (blk0 + bi, k_col0 + h)),
            pl.BlockSpec((c, dv), lambda bi, h: (blk0 + bi, v_col0 + v_step * h)),
            pl.BlockSpec(memory_space=pl.ANY),
        ],
        out_specs=pl.BlockSpec((c, dv), lambda bi, h: (blk0 + bi, h)),
        input_output_aliases={3: 0},
        compiler_params=_cparams(("parallel", "parallel"), 32),
        name=name,
    )(q_arr, k_arr, v_arr, out_buf)


def _lru_kernel(*refs, reverse, t_chunk, n_ctx_chunks, n_lat_chunks, n_blocks, block_dim):
    if reverse:
        (x_ref, xp_ref, xn_ref, cw_ref, cb_ref, wa_ref, ba_ref, wx_ref, bx_ref, lam_ref,
         hf_ref, g_ref, o_ref, carry_ref) = refs
    else:
        (x_ref, xp_ref, xn_ref, cw_ref, cb_ref, wa_ref, ba_ref, wx_ref, bx_ref, lam_ref,
         o_ref, carry_ref) = refs
    j = pl.program_id(1)

    @pl.when(j == 0)
    def _():
        carry_ref[...] = jnp.zeros_like(carry_ref)

    is_ctx = j < n_ctx_chunks
    pos = jnp.where(is_ctx, j, j - n_ctx_chunks)
    n_seg = jnp.where(is_ctx, n_ctx_chunks, n_lat_chunks)
    chunk = (n_seg - 1 - pos) if reverse else pos
    first = chunk == 0
    last = chunk == n_seg - 1

    xm = x_ref[...].astype(F32)
    xp = xp_ref[...].astype(F32)
    xn = xn_ref[...].astype(F32)
    hp = xp.shape[0]
    pm1 = jnp.where(first, 0.0, xp[hp - 1:hp, :])
    pm2 = jnp.where(first, 0.0, xp[hp - 2:hp - 1, :])
    nn0 = jnp.where(last, 0.0, xn[0:1, :])
    t = lax.broadcasted_iota(jnp.int32, (t_chunk, 1), 0)
    xm1 = jnp.where(t == 0, pm1, pltpu.roll(xm, 1, 0))
    xm2 = jnp.where(t == 0, pm2, jnp.where(t == 1, pm1, pltpu.roll(xm, 2, 0)))
    xp1 = jnp.where(t == t_chunk - 1, nn0, pltpu.roll(xm, t_chunk - 1, 0))
    cw = cw_ref[...]
    u = cw[0:1] * xm2 + cw[1:2] * xm1 + cw[2:3] * xm + cw[3:4] * xp1 + cb_ref[...]

    ub = u.astype(BF16)
    ga, gx = [], []
    for kb in range(n_blocks):
        blk = ub[:, kb * block_dim:(kb + 1) * block_dim]
        ga.append(jnp.dot(blk, wa_ref[kb], preferred_element_type=F32))
        gx.append(jnp.dot(blk, wx_ref[kb], preferred_element_type=F32))
    gate_a = jnp.concatenate(ga, axis=1) + ba_ref[...]
    gate_x = jnp.concatenate(gx, axis=1) + bx_ref[...]
    r = jax.nn.sigmoid(gate_a)
    i_gate = jax.nn.sigmoid(gate_x)
    nlam = -lam_ref[...]
    softplus = jnp.maximum(nlam, 0.0) + jnp.log1p(jnp.exp(-jnp.abs(nlam)))
    log_a = -LRU_C * r * softplus
    a = jnp.exp(log_a)
    bv = jnp.sqrt(-jnp.tanh(log_a) * (a * a + 1.0)) * (i_gate * u)

    grp = SUBLANE_F32
    sub = lax.broadcasted_iota(jnp.int32, (grp, 1), 0)
    order = range(t_chunk // grp - 1, -1, -1) if reverse else range(t_chunk // grp)
    carry = carry_ref[0:1, :]
    h_groups = {}
    for gi in order:
        a_g = a[gi * grp:(gi + 1) * grp, :]
        b_g = bv[gi * grp:(gi + 1) * grp, :]
        k = 1
        while k < grp:
            keep = (sub < grp - k) if reverse else (sub >= k)
            shift = grp - k if reverse else k
            b_g = jnp.where(keep, a_g * pltpu.roll(b_g, shift, 0) + b_g, b_g)
            a_g = jnp.where(keep, a_g * pltpu.roll(a_g, shift, 0), a_g)
            k *= 2
        h_g = a_g * carry + b_g
        carry = h_g[0:1, :] if reverse else h_g[grp - 1:grp, :]
        h_groups[gi] = h_g
    h = jnp.concatenate([h_groups[gi] for gi in range(t_chunk // grp)], axis=0)
    carry_ref[0:1, :] = carry
    if reverse:
        g = g_ref[...].astype(F32)
        o_ref[...] = (jax.nn.gelu(g) * (hf_ref[...] + h)).astype(o_ref.dtype)
    else:
        o_ref[...] = h


def _lru_scan(z, l, direction, conv_w, conv_b, w_a, b_a, w_x, b_x, lam, h_fwd, dims):
    tt = z.shape[0]
    b, n, c, lw = dims["b"], dims["n"], dims["c"], dims["lru_width"]
    n_blocks, block_dim = w_a.shape[2], w_a.shape[3]
    t_chunk = _tile(c, 256, SUBLANE_BF16)
    halo = SUBLANE_BF16
    n_ctx_chunks, n_lat_chunks = c // t_chunk, n // t_chunk
    ctx_chunk0 = dims["bn"] // t_chunk
    col_x = dims["off_lru_x"] // lw
    col_g = dims["off_lru_g"] // lw
    reverse = direction == 1
    per_halo = t_chunk // halo
    n_halo_blocks = tt // halo

    def chunk_block(bi, j):
        is_ctx = j < n_ctx_chunks
        pos = jnp.where(is_ctx, j, j - n_ctx_chunks)
        if reverse:
            pos = jnp.where(is_ctx, n_ctx_chunks - 1 - pos, n_lat_chunks - 1 - pos)
        return jnp.where(is_ctx, ctx_chunk0 + bi * n_ctx_chunks + pos, bi * n_lat_chunks + pos)

    def prev_block(bi, j):
        return jnp.maximum(chunk_block(bi, j) * per_halo - 1, 0)

    def next_block(bi, j):
        return jnp.minimum((chunk_block(bi, j) + 1) * per_halo, n_halo_blocks - 1)

    vec = lambda: pl.BlockSpec((None, None, 1, lw), lambda bi, j: (l, direction, 0, 0))
    gate_w = lambda: pl.BlockSpec((None, None, n_blocks, block_dim, block_dim),
                                  lambda bi, j: (l, direction, 0, 0, 0))
    in_specs = [
        pl.BlockSpec((t_chunk, lw), lambda bi, j: (chunk_block(bi, j), col_x)),
        pl.BlockSpec((halo, lw), lambda bi, j: (prev_block(bi, j), col_x)),
        pl.BlockSpec((halo, lw), lambda bi, j: (next_block(bi, j), col_x)),
        pl.BlockSpec((None, conv_w.shape[1], lw), lambda bi, j: (l, 0, 0)),
        pl.BlockSpec((None, 1, lw), lambda bi, j: (l, 0, 0)),
        gate_w(), vec(), gate_w(), vec(), vec(),
    ]
    args = [z, z, z, conv_w, conv_b, w_a, b_a, w_x, b_x, lam]
    if reverse:
        in_specs += [
            pl.BlockSpec((t_chunk, lw), lambda bi, j: (chunk_block(bi, j), 0)),
            pl.BlockSpec((t_chunk, lw), lambda bi, j: (chunk_block(bi, j), col_g)),
        ]
        args += [h_fwd, z]
        out_dtype = BF16
    else:
        out_dtype = F32
    kern = functools.partial(_lru_kernel, reverse=reverse, t_chunk=t_chunk, n_ctx_chunks=n_ctx_chunks,
                             n_lat_chunks=n_lat_chunks, n_blocks=n_blocks, block_dim=block_dim)
    return pl.pallas_call(
        kern,
        out_shape=jax.ShapeDtypeStruct((tt, lw), out_dtype),
        grid=(b, n_ctx_chunks + n_lat_chunks),
        in_specs=in_specs,
        out_specs=pl.BlockSpec((t_chunk, lw), lambda bi, j: (chunk_block(bi, j), 0)),
        scratch_shapes=[pltpu.VMEM((SUBLANE_F32, lw), F32)],
        compiler_params=_cparams(("parallel", "arbitrary"), 40),
        name="lru_bwd" if reverse else "lru_fwd",
    )(*args)


def _mla_proj_kernel(z_ref, qn_ref, kvn_ref, wq_ref, wkv_ref, ct_ref, st_ref, q_ref, k_ref, v_ref,
                     *, q_rank, kv_rank, heads, q_scale):
    z = z_ref[...]
    cq = z[:, :q_rank].astype(F32)
    ckv = z[:, q_rank:q_rank + kv_rank].astype(F32)
    krp = z[:, q_rank + kv_rank:q_rank + kv_rank + LANE].astype(F32)
    ct = ct_ref[...]
    st = st_ref[...]

    def rope(tile):
        return tile * ct + pltpu.roll(tile, LANE // 2, 1) * st

    qn = _rmsnorm_f32(cq, qn_ref[...]).astype(BF16)
    q = jnp.dot(qn, wq_ref[...], preferred_element_type=F32)
    kvn = _rmsnorm_f32(ckv, kvn_ref[...]).astype(BF16)
    kv = jnp.dot(kvn, wkv_ref[...], preferred_element_type=F32)
    kr = rope(krp).astype(BF16)
    for h in range(heads):
        c0 = h * MLA_HEAD_PAD
        q_ref[:, c0:c0 + LANE] = (q[:, c0:c0 + LANE] * q_scale).astype(BF16)
        q_ref[:, c0 + LANE:c0 + 2 * LANE] = (rope(q[:, c0 + LANE:c0 + 2 * LANE]) * q_scale).astype(BF16)
        k_ref[:, c0:c0 + LANE] = kv[:, h * LANE:(h + 1) * LANE].astype(BF16)
        k_ref[:, c0 + LANE:c0 + 2 * LANE] = kr
        v_ref[:, c0:c0 + LANE] = kv[:, (heads + h) * LANE:(heads + h + 1) * LANE].astype(BF16)
        v_ref[:, c0 + LANE:c0 + 2 * LANE] = jnp.ones((z.shape[0], LANE), BF16)


def _mla_proj(z, l, q_norm, kv_norm, wq, wkv, ct, st, dims):
    tt = z.shape[0]
    heads, q_rank, kv_rank = dims["mla_heads"], dims["q_rank"], dims["kv_rank"]
    mg = dims["mla_group"]
    tm = _tile(dims["tile_rows"], 512, SUBLANE_BF16)
    col = dims["off_mla"] // mg
    kern = functools.partial(_mla_proj_kernel, q_rank=q_rank, kv_rank=kv_rank, heads=heads,
                             q_scale=MLA_QK_DIM ** -0.5)
    return pl.pallas_call(
        kern,
        out_shape=(jax.ShapeDtypeStruct((tt, heads * MLA_HEAD_PAD), BF16),
                   jax.ShapeDtypeStruct((tt, heads * MLA_HEAD_PAD), BF16),
                   jax.ShapeDtypeStruct((tt, heads * MLA_HEAD_PAD), BF16)),
        grid=(tt // tm,),
        in_specs=[
            pl.BlockSpec((tm, mg), lambda i: (i, col)),
            pl.BlockSpec((None, 1, q_rank), lambda i: (l, 0, 0)),
            pl.BlockSpec((None, 1, kv_rank), lambda i: (l, 0, 0)),
            pl.BlockSpec((None, q_rank, heads * MLA_HEAD_PAD), lambda i: (l, 0, 0)),
            pl.BlockSpec((None, kv_rank, heads * 2 * LANE), lambda i: (l, 0, 0)),
            pl.BlockSpec((tm, LANE), lambda i: (i, 0)),
            pl.BlockSpec((tm, LANE), lambda i: (i, 0)),
        ],
        out_specs=(pl.BlockSpec((tm, heads * MLA_HEAD_PAD), lambda i: (i, 0)),
                   pl.BlockSpec((tm, heads * MLA_HEAD_PAD), lambda i: (i, 0)),
                   pl.BlockSpec((tm, heads * MLA_HEAD_PAD), lambda i: (i, 0))),
        compiler_params=_cparams(("parallel",), 48),
        name="mla_proj",
    )(z, q_norm, kv_norm, wq, wkv, ct, st)


def _mla_flash_kernel(q_ref, kt_ref, v_ref, ktc_ref, vc_ref, o_ref, *, sub, tk):
    n_chunks = kt_ref.shape[1] // tk
    chunks = [None] + list(range(n_chunks))

    def scores(q, ch):
        kt = ktc_ref[...] if ch is None else kt_ref[:, ch * tk:(ch + 1) * tk]
        return jnp.dot(q, kt, preferred_element_type=F32)

    def values(ch):
        return vc_ref[...] if ch is None else v_ref[ch * tk:(ch + 1) * tk, :]

    for a in range(q_ref.shape[0] // sub):
        rows = slice(a * sub, (a + 1) * sub)
        q = q_ref[rows, :]
        s_next = scores(q, chunks[0])
        m = acc = None
        for idx, ch in enumerate(chunks):
            s = s_next
            if idx + 1 < len(chunks):
                s_next = scores(q, chunks[idx + 1])
            m_new = jnp.max(s, axis=-1, keepdims=True)
            if m is not None:
                m_new = jnp.maximum(m, m_new)
            p = jnp.exp(s - m_new).astype(BF16)
            pv = jnp.dot(p, values(ch), preferred_element_type=F32)
            acc = pv if m is None else jnp.exp(m - m_new) * acc + pv
            m = m_new
        o_ref[rows, :] = (acc[:, :MLA_V_DIM] * (1.0 / acc[:, MLA_V_DIM:])).astype(o_ref.dtype)


def _mla_attention(q, kt, v, dims):
    tt = q.shape[0]
    b, n, c, heads = dims["b"], dims["n"], dims["c"], dims["mla_heads"]
    tq = _tile(n, 1024, SUBLANE_BF16)
    sub = _tile(tq, 512, SUBLANE_BF16)
    ctx_blk0 = dims["bn"] // c
    nq = n // tq
    vw = 2 * MLA_V_DIM
    return pl.pallas_call(
        functools.partial(_mla_flash_kernel, sub=sub, tk=_tile(n, 2048, LANE)),
        out_shape=jax.ShapeDtypeStruct((tt, heads * MLA_V_DIM), BF16),
        grid=(b, heads, nq),
        in_specs=[
            pl.BlockSpec((tq, MLA_HEAD_PAD), lambda bi, h, i: (bi * nq + i, h)),
            pl.BlockSpec((None, MLA_HEAD_PAD, n), lambda bi, h, i: (h, 0, bi)),
            pl.BlockSpec((n, vw), lambda bi, h, i: (bi, h)),
            pl.BlockSpec((None, MLA_HEAD_PAD, c), lambda bi, h, i: (h, 0, ctx_blk0 + bi)),
            pl.BlockSpec((c, vw), lambda bi, h, i: (ctx_blk0 + bi, h)),
        ],
        out_specs=pl.BlockSpec((tq, MLA_V_DIM), lambda bi, h, i: (bi * nq + i, h)),
        compiler_params=_cparams(("parallel", "parallel", "arbitrary"), 48),
        name="mla_attention",
    )(q, kt, v, kt, v)


def _merge_kernel(ba_ref, bb_ref, bc_ref, wb_ref, ga_ref, gb_ref, gc_ref, o_ref):
    y = None
    for br, gate, i in ((ba_ref, ga_ref, 0), (bb_ref, gb_ref, 1), (bc_ref, gc_ref, 2)):
        t = jnp.dot(br[...], wb_ref[i], preferred_element_type=F32)
        t = jax.nn.sigmoid(gate[...].astype(F32)) * t
        y = t if y is None else y + t
    o_ref[...] = y.astype(o_ref.dtype)


def _merge(br_a, br_b, br_c, w_branch, z, l, n_rows, dims):
    d = dims["d"]
    bw = br_a.shape[1]
    tm = _tile(dims["tile_rows"], 1024, SUBLANE_BF16)
    tn = _tile(d, 512, LANE)
    gate_col0 = dims["off_gate"] // tn
    per_gate = d // tn
    branch = lambda: pl.BlockSpec((tm, bw), lambda i, j: (i, 0))
    gate = lambda k: pl.BlockSpec((tm, tn), lambda i, j: (i, gate_col0 + k * per_gate + j))
    return pl.pallas_call(
        _merge_kernel,
        out_shape=jax.ShapeDtypeStruct((n_rows, d), BF16),
        grid=(n_rows // tm, d // tn),
        in_specs=[branch(), branch(), branch(),
                  pl.BlockSpec((None, 3, bw, tn), lambda i, j: (l, 0, 0, j)),
                  gate(0), gate(1), gate(2)],
        out_specs=pl.BlockSpec((tm, tn), lambda i, j: (i, j)),
        compiler_params=_cparams(("parallel", "arbitrary"), 48),
        name="merge",
    )(br_a, br_b, br_c, w_branch, z, z, z)


def _out_proj_kernel(y_ref, w_ref, x_ref, g_ref, o_ref):
    t = jnp.dot(y_ref[...], w_ref[...], preferred_element_type=F32)
    o_ref[...] = x_ref[...] + g_ref[0] * t


def _out_proj(y, w_out, xall, gate, l, dims):
    n_rows, d = y.shape
    tm = _tile(dims["tile_rows"], 1024, SUBLANE_BF16)
    tn = _tile(d, 512, LANE)
    grp = functools.partial(_group_of_tile, tm=tm, bn=dims["bn"], n=dims["n"])
    return pl.pallas_call(
        _out_proj_kernel,
        out_shape=jax.ShapeDtypeStruct((n_rows, d), F32),
        grid=(n_rows // tm, d // tn),
        in_specs=[
            pl.BlockSpec((tm, d), lambda i, j: (i, 0)),
            pl.BlockSpec((None, d, tn), lambda i, j: (l, 0, j)),
            pl.BlockSpec((tm, tn), lambda i, j: (i, j)),
            pl.BlockSpec((1, 1, tn), lambda i, j: (grp(i), 0, j)),
        ],
        out_specs=pl.BlockSpec((tm, tn), lambda i, j: (i, j)),
        compiler_params=_cparams(("parallel", "arbitrary"), 48),
        name="out_proj",
    )(y, w_out, xall, gate)


def _ffn_kernel(x_ref, xp_ref, xn_ref, gam_ref, sh_ref, sc_ref, g2_ref, wv_ref, wg_ref, cwv_ref, cwg_ref,
                cbv_ref, cbg_ref, wd_ref, nf_ref, *rest, tm, tiles_per_seq, final_norm):
    o_ref, h_ref = rest[-2:]
    i = pl.program_id(0)
    f = pl.program_id(1)
    halo = SUBLANE_F32

    def norm_mod(x):
        y = _rmsnorm_f32(x, gam_ref[...])
        return y * (1.0 + sc_ref[0]) + sh_ref[0]

    @pl.when(f == 0)
    def _():
        h_ref[0:tm, :] = norm_mod(x_ref[...]).astype(BF16)
        hh = norm_mod(jnp.concatenate([xn_ref[...], xp_ref[...]], axis=0))
        pos = i % tiles_per_seq
        keep_next = (pos != tiles_per_seq - 1).astype(F32)
        keep_prev = (pos != 0).astype(F32)
        is_next = lax.broadcasted_iota(jnp.int32, (2 * halo, 1), 0) < halo
        h_ref[tm:tm + 2 * halo, :] = (hh * jnp.where(is_next, keep_next, keep_prev)).astype(BF16)
        o_ref[...] = jnp.zeros_like(o_ref)

    hv = h_ref[...]
    rows = tm + 2 * halo

    def conv(w_ref, cw_ref, cb_ref):
        zz = jnp.dot(hv, w_ref[...], preferred_element_type=F32)
        cw = cw_ref[...]
        zm1 = pltpu.roll(zz, 1, 0)[0:tm]
        zp1 = pltpu.roll(zz, rows - 1, 0)[0:tm]
        return cw[0:1] * zm1 + cw[1:2] * zz[0:tm] + cw[2:3] * zp1 + cb_ref[...]

    val = conv(wv_ref, cwv_ref, cbv_ref)
    gate = conv(wg_ref, cwg_ref, cbg_ref)
    act = (gate * jax.nn.sigmoid(gate) * val).astype(BF16)
    o_ref[...] += jnp.dot(act, wd_ref[...], preferred_element_type=F32)

    @pl.when(f == pl.num_programs(1) - 1)
    def _():
        xo = x_ref[...] + g2_ref[0] * o_ref[...]
        if final_norm:
            xo = _rmsnorm_f32(xo, nf_ref[...])
        o_ref[...] = xo


def _ffn(xall, out_buf, gamma, shift, scale, gate2, w_up, conv_w, conv_b, w_down, norm_final, l, row0, n_rows,
         seq_len, out_rows, final_norm, dims):
    tt, d = xall.shape
    ff = w_down.shape[1]
    tm = _tile(seq_len, 512, SUBLANE_BF16)
    tf = _tile(ff, 512, LANE)
    nf = ff // tf
    halo = SUBLANE_F32
    per_halo = tm // halo
    n_halo_blocks = tt // halo
    tile0 = row0 // tm
    assert row0 % tm == 0 and tm % halo == 0
    grp = lambda i: _group_of_tile(tile0 + i, tm=tm, bn=dims["bn"], n=dims["n"])
    mod = lambda: pl.BlockSpec((1, 1, d), lambda i, f: (grp(i), 0, 0))
    kern = functools.partial(_ffn_kernel, tm=tm, tiles_per_seq=seq_len // tm, final_norm=final_norm)
    args = [xall, xall, xall, gamma, shift, scale, gate2, w_up, w_up, conv_w, conv_w, conv_b, conv_b, w_down,
            norm_final]
    extra_specs, aliases = [], {}
    if out_buf is not None:
        aliases = {len(args): 0}
        args.append(out_buf)
        extra_specs = [pl.BlockSpec(memory_space=pl.ANY)]
    return pl.pallas_call(
        kern,
        out_shape=jax.ShapeDtypeStruct((out_rows, d), F32),
        grid=(n_rows // tm, nf),
        in_specs=[
            pl.BlockSpec((tm, d), lambda i, f: (tile0 + i, 0)),
            pl.BlockSpec((halo, d), lambda i, f: (jnp.maximum((tile0 + i) * per_halo - 1, 0), 0)),
            pl.BlockSpec((halo, d), lambda i, f: (jnp.minimum((tile0 + i + 1) * per_halo, n_halo_blocks - 1), 0)),
            pl.BlockSpec((None, 1, d), lambda i, f: (l, 0, 0)),
            mod(), mod(), mod(),
            pl.BlockSpec((None, d, tf), lambda i, f: (l, 0, f)),
            pl.BlockSpec((None, d, tf), lambda i, f: (l, 0, nf + f)),
            pl.BlockSpec((None, conv_w.shape[1], tf), lambda i, f: (l, 0, f)),
            pl.BlockSpec((None, conv_w.shape[1], tf), lambda i, f: (l, 0, nf + f)),
            pl.BlockSpec((None, 1, tf), lambda i, f: (l, 0, f)),
            pl.BlockSpec((None, 1, tf), lambda i, f: (l, 0, nf + f)),
            pl.BlockSpec((None, tf, d), lambda i, f: (l, f, 0)),
            pl.BlockSpec((1, d), lambda i, f: (0, 0)),
        ] + extra_specs,
        out_specs=pl.BlockSpec((tm, d), lambda i, f: (tile0 + i, 0)),
        scratch_shapes=[pltpu.VMEM((tm + 2 * halo, d), BF16)],
        input_output_aliases=aliases,
        compiler_params=_cparams(("parallel", "arbitrary"), 52),
        name="conv_ffn",
    )(*args)


def _rope_tables(n, b, c):
    t = jnp.arange(n, dtype=jnp.int32)
    row = (t // GRID_W).astype(F32)
    col = (t % GRID_W).astype(F32)
    n_freq = MLA_ROPE_DIM // 4
    inv_freq = ROPE_THETA ** (-jnp.arange(n_freq, dtype=F32) / n_freq)
    ang = jnp.concatenate([row[:, None] * inv_freq, col[:, None] * inv_freq], axis=-1)
    cos, sin = jnp.cos(ang), jnp.sin(ang)
    zeros = jnp.zeros((n, LANE - MLA_ROPE_DIM), F32)
    ct = jnp.concatenate([cos, cos, zeros], axis=-1)
    st = jnp.concatenate([-sin, sin, zeros], axis=-1)
    ct_ctx = jnp.concatenate([jnp.ones((b * c, MLA_ROPE_DIM), F32), jnp.zeros((b * c, LANE - MLA_ROPE_DIM), F32)], -1)
    ct = jnp.concatenate([jnp.tile(ct, (b, 1)), ct_ctx], axis=0)
    st = jnp.concatenate([jnp.tile(st, (b, 1)), jnp.zeros((b * c, LANE), F32)], axis=0)
    return ct, st


def _swap_halves(w):
    half = w.shape[-1] // 2
    return jnp.concatenate([w[..., half:], w[..., :half]], axis=-1)


def kernel(x, c, ctx, c_ctx, w_mod, b_mod, norm_mix, norm_ffn, w_in, na_rpb, lru_conv_w, lru_conv_b, lru_w_a,
           lru_b_a, lru_w_x, lru_b_x, lru_lam, mla_q_norm, mla_kv_norm, mla_w_q_up, mla_w_kv_up, w_branch, w_out,
           ffn_w_up, ffn_conv_w, ffn_conv_b, ffn_w_down, norm_final):
    b, n, d = x.shape
    ctx_len = ctx.shape[1]
    depth = w_mod.shape[0]
    na_heads = na_rpb.shape[1]
    na_width = na_heads * NA_HEAD_DIM
    lru_width = lru_conv_w.shape[2]
    q_rank = mla_q_norm.shape[1]
    kv_rank = mla_kv_norm.shape[1]
    mla_heads = mla_w_q_up.shape[2] // MLA_QK_DIM
    bn, bc = b * n, b * ctx_len
    assert na_width == lru_width == mla_heads * MLA_V_DIM == w_branch.shape[2]
    assert n % (NA_KROWS * GRID_W) == 0 and n % ctx_len == 0

    off_lru_x = 3 * na_width
    off_lru_g = off_lru_x + lru_width
    off_mla = off_lru_g + lru_width
    mla_used = q_rank + kv_rank + 2 * MLA_ROPE_DIM
    mla_group = -(-mla_used // LANE) * LANE
    while off_mla % mla_group:
        mla_group += LANE
    off_gate = off_mla + mla_group
    dims = dict(b=b, n=n, c=ctx_len, d=d, bn=bn, na_heads=na_heads, lru_width=lru_width, q_rank=q_rank,
                kv_rank=kv_rank, mla_heads=mla_heads, off_lru_x=off_lru_x, off_lru_g=off_lru_g, off_mla=off_mla,
                mla_group=mla_group, off_gate=off_gate, tile_rows=int(np.gcd(n, bc)))

    o_kr = off_mla + q_rank + kv_rank
    w_in_p = jnp.concatenate([
        w_in[:, :, :o_kr + MLA_ROPE_DIM],
        _swap_halves(w_in[:, :, o_kr:o_kr + MLA_ROPE_DIM]),
        jnp.zeros((depth, d, mla_group - mla_used), F32),
        w_in[:, :, o_kr + MLA_ROPE_DIM:],
    ], axis=-1).astype(BF16)
    wq = mla_w_q_up.reshape(depth, q_rank, mla_heads, MLA_QK_DIM)
    wq_rope = wq[..., MLA_NOPE_DIM:]
    wq_p = jnp.concatenate([wq[..., :MLA_NOPE_DIM], wq_rope, _swap_halves(wq_rope)], axis=-1)
    wq_p = wq_p.reshape(depth, q_rank, mla_heads * MLA_HEAD_PAD).astype(BF16)
    wkv = mla_w_kv_up.reshape(depth, kv_rank, mla_heads, MLA_NOPE_DIM + MLA_V_DIM)
    wkv_p = jnp.concatenate([wkv[..., :MLA_NOPE_DIM].reshape(depth, kv_rank, -1),
                             wkv[..., MLA_NOPE_DIM:].reshape(depth, kv_rank, -1)], axis=-1).astype(BF16)
    w_branch_b = w_branch.astype(BF16)
    w_out_b = w_out.astype(BF16)
    w_up_b = ffn_w_up.astype(BF16)
    w_down_b = ffn_w_down.astype(BF16)
    lru_w_a_b = lru_w_a.astype(BF16)
    lru_w_x_b = lru_w_x.astype(BF16)
    ct, st = _rope_tables(n, b, ctx_len)
    na_bias = _na_bias_tables(na_rpb, n // GRID_W)

    n_groups = 1 + b
    pad_rows = -n_groups % SUBLANE_F32
    cond = jnp.concatenate([c_ctx[None, :], c, jnp.zeros((pad_rows, d), F32)], axis=0)
    mod = _modulation(cond, w_mod, b_mod)[:, :n_groups].reshape(depth, n_groups, N_MOD, 1, d)

    r3 = lambda a: a.reshape(a.shape[0], 1, a.shape[1])
    r4 = lambda a: a.reshape(a.shape[0], a.shape[1], 1, a.shape[2])
    norm_mix3, norm_ffn3 = r3(norm_mix), r3(norm_ffn)
    lru_conv_b3 = r3(lru_conv_b)
    lru_b_a4, lru_b_x4, lru_lam4 = r4(lru_b_a), r4(lru_b_x), r4(lru_lam)
    q_norm3, kv_norm3 = r3(mla_q_norm), r3(mla_kv_norm)
    ffn_conv_b3 = r3(ffn_conv_b)
    norm_final2 = norm_final.reshape(1, d)

    xall = jnp.concatenate([x.reshape(bn, d), ctx.reshape(bc, d)], axis=0)
    for l in range(depth):
        last = l == depth - 1
        n_rows = bn if last else bn + bc
        sh1, sc1, g1, sh2, sc2, g2 = (mod[l, :, k] for k in range(N_MOD))
        z = _in_proj(xall, norm_mix3, sh1, sc1, w_in_p, l, dims)
        out_a = _na_attention(z, na_bias, l, dims)
        lru_args = (lru_conv_w, lru_conv_b3, lru_w_a_b, lru_b_a4, lru_w_x_b, lru_b_x4, lru_lam4)
        h_fwd = _lru_scan(z, l, 0, *lru_args, None, dims)
        out_b = _lru_scan(z, l, 1, *lru_args, h_fwd, dims)
        q_m, k_m, v_m = _mla_proj(z, l, q_norm3, kv_norm3, wq_p, wkv_p, ct, st, dims)
        kt_m = k_m.reshape(bn + bc, mla_heads, MLA_HEAD_PAD).transpose(1, 2, 0)
        out_c = _mla_attention(q_m, kt_m, v_m, dims)
        if not last:
            out_a = _ctx_attention(z, z, z, out_a, dims, na_heads, NA_HEAD_DIM, 0, na_heads, 2 * na_heads, 1,
                                   NA_HEAD_DIM ** -0.5, "na_ctx_attention")
            out_c = _ctx_attention(q_m, k_m, v_m, out_c, dims, mla_heads, MLA_HEAD_PAD, 0, 0, 0, 2, 1.0,
                                   "mla_ctx_attention")
        y = _merge(out_a, out_b, out_c, w_branch_b, z, l, n_rows, dims)
        xall = _out_proj(y, w_out_b, xall, g1, l, dims)
        ffn_args = (norm_ffn3, sh2, sc2, g2, w_up_b, ffn_conv_w, ffn_conv_b3, w_down_b, norm_final2, l)
        x_new = _ffn(xall, None, *ffn_args, 0, bn, n, n_rows, last, dims)
        if not last:
            x_new = _ffn(xall, x_new, *ffn_args, bn, bc, ctx_len, n_rows, False, dims)
        xall = x_new
    return xall.reshape(b, n, d)
```

```python
import functools

import numpy as np
import jax
import jax.numpy as jnp
from jax import lax
from jax.experimental import pallas as pl
from jax.experimental.pallas import tpu as pltpu

F32 = jnp.float32
BF16 = jnp.bfloat16

GRID_W = 64
NORM_EPS = 1e-6
NEG_INF = -1e30
N_MOD = 6
NA_HEAD_DIM = 128
NA_WIN_H = 8
NA_WIN_W = 16
LRU_C = 8.0
MLA_NOPE_DIM = 128
MLA_ROPE_DIM = 64
MLA_V_DIM = 128
MLA_QK_DIM = MLA_NOPE_DIM + MLA_ROPE_DIM
ROPE_THETA = 10000.0

LANE = 128
SUBLANE_F32 = 8
SUBLANE_BF16 = 16
MLA_HEAD_PAD = 256
NA_QROWS = 8
NA_KROWS = 16
MIB = 1024 * 1024


def _cparams(semantics, vmem_mib):
    return pltpu.CompilerParams(dimension_semantics=semantics, vmem_limit_bytes=vmem_mib * MIB)


def _tile(n, target, mult):
    best = None
    for t in range(mult, min(n, target) + 1, mult):
        if n % t == 0:
            best = t
    assert best is not None, (n, target, mult)
    return best


def _group_of_tile(i, tm, bn, n):
    return jnp.where(i * tm >= bn, 0, 1 + (i * tm) // n)


def _rmsnorm_f32(x, gamma):
    ms = jnp.mean(x * x, axis=-1, keepdims=True)
    return x * lax.rsqrt(ms + NORM_EPS) * gamma


def _mod_kernel(c_ref, w_ref, b_ref, o_ref):
    cc = c_ref[...]
    s = cc * jax.nn.sigmoid(cc)
    o_ref[...] = jnp.dot(s, w_ref[...], preferred_element_type=F32,
                         precision=lax.Precision.HIGHEST) + b_ref[...]


def _modulation(cond, w_mod, b_mod):
    depth, d, nm = w_mod.shape
    rows = cond.shape[0]
    tn = _tile(nm, 1024, LANE)
    return pl.pallas_call(
        _mod_kernel,
        out_shape=jax.ShapeDtypeStruct((depth, rows, nm), F32),
        grid=(depth, nm // tn),
        in_specs=[
            pl.BlockSpec((rows, d), lambda l, j: (0, 0)),
            pl.BlockSpec((None, d, tn), lambda l, j: (l, 0, j)),
            pl.BlockSpec((None, 1, tn), lambda l, j: (l, 0, j)),
        ],
        out_specs=pl.BlockSpec((None, rows, tn), lambda l, j: (l, 0, j)),
        compiler_params=_cparams(("arbitrary", "arbitrary"), 40),
        name="adaln_mod",
    )(cond, w_mod, b_mod.reshape(depth, 1, nm))


def _in_proj_kernel(x_ref, gam_ref, sh_ref, sc_ref, w_ref, o_ref, h_ref):
    @pl.when(pl.program_id(1) == 0)
    def _():
        y = _rmsnorm_f32(x_ref[...], gam_ref[...])
        h_ref[...] = (y * (1.0 + sc_ref[0]) + sh_ref[0]).astype(BF16)

    o_ref[...] = jnp.dot(h_ref[...], w_ref[...], preferred_element_type=F32).astype(o_ref.dtype)


def _in_proj(xall, gamma, shift, scale, w, l, dims):
    tt, d = xall.shape
    npad = w.shape[2]
    tm = _tile(dims["tile_rows"], 1024, SUBLANE_BF16)
    tn = _tile(npad, 1024, LANE)
    grp = functools.partial(_group_of_tile, tm=tm, bn=dims["bn"], n=dims["n"])
    return pl.pallas_call(
        _in_proj_kernel,
        out_shape=jax.ShapeDtypeStruct((tt, npad), BF16),
        grid=(tt // tm, npad // tn),
        in_specs=[
            pl.BlockSpec((tm, d), lambda i, j: (i, 0)),
            pl.BlockSpec((None, 1, d), lambda i, j: (l, 0, 0)),
            pl.BlockSpec((1, 1, d), lambda i, j: (grp(i), 0, 0)),
            pl.BlockSpec((1, 1, d), lambda i, j: (grp(i), 0, 0)),
            pl.BlockSpec((None, d, tn), lambda i, j: (l, 0, j)),
        ],
        out_specs=pl.BlockSpec((tm, tn), lambda i, j: (i, j)),
        scratch_shapes=[pltpu.VMEM((tm, d), BF16)],
        compiler_params=_cparams(("parallel", "arbitrary"), 48),
        name="in_proj",
    )(xall, gamma, shift, scale, w)


def _na_group_layout(g, grid_rows):
    kh = min(NA_WIN_H, grid_rows)
    k0 = int(np.clip(NA_QROWS * g - NA_WIN_H // 2, 0, grid_rows - NA_KROWS))
    layout = []
    for qi in range(NA_QROWS):
        r = NA_QROWS * g + qi
        first_key_row = int(np.clip(r - kh // 2, 0, grid_rows - kh))
        layout.append(tuple((k0 + kj) - r + NA_WIN_H - 1 if first_key_row <= k0 + kj < first_key_row + kh
                            else None for kj in range(NA_KROWS)))
    return k0, tuple(layout)


def _na_variants(grid_rows):
    layouts = [_na_group_layout(g, grid_rows)[1] for g in range(grid_rows // NA_QROWS)]
    distinct = sorted(set(layouts), key=layouts.index)
    return distinct, [distinct.index(lay) for lay in layouts]


def _na_kernel(q_ref, k_ref, v_ref, kc_ref, vc_ref, bias_ref, o_ref, *, grid_rows, scale):
    n_groups = grid_rows // NA_QROWS
    variant_of_group = _na_variants(grid_rows)[1]
    tq = NA_QROWS * GRID_W
    nk = NA_KROWS * GRID_W
    nt = (((1,), (1,)), ((), ()))

    def key_start(g):
        return _na_group_layout(g, grid_rows)[0] * GRID_W

    def scores(g):
        q = q_ref[g * tq:(g + 1) * tq, :]
        k0 = key_start(g)
        var = variant_of_group[g]
        s = lax.dot_general(q, k_ref[k0:k0 + nk, :], nt, preferred_element_type=F32) * scale + bias_ref[var]
        sc = lax.dot_general(q, kc_ref[...], nt, preferred_element_type=F32) * scale
        return s, sc

    nxt = scores(0)
    for g in range(n_groups):
        s, sc = nxt
        if g + 1 < n_groups:
            nxt = scores(g + 1)
        k0 = key_start(g)
        m = jnp.maximum(jnp.max(s, axis=-1, keepdims=True), jnp.max(sc, axis=-1, keepdims=True))
        p = jnp.exp(s - m)
        pc = jnp.exp(sc - m)
        denom = jnp.sum(p, axis=-1, keepdims=True) + jnp.sum(pc, axis=-1, keepdims=True)
        o = jnp.dot(p.astype(BF16), v_ref[k0:k0 + nk, :], preferred_element_type=F32)
        o = o + jnp.dot(pc.astype(BF16), vc_ref[...], preferred_element_type=F32)
        o_ref[g * tq:(g + 1) * tq, :] = (o * (1.0 / denom)).astype(o_ref.dtype)


def _na_bias_tables(rpb, grid_rows):
    lead = rpb.shape[:2]
    cidx = np.arange(GRID_W)
    c_start = np.clip(cidx - NA_WIN_W // 2, 0, GRID_W - NA_WIN_W)
    in_win = (cidx[None, :] >= c_start[:, None]) & (cidx[None, :] < c_start[:, None] + NA_WIN_W)
    col_idx = np.clip(cidx[None, :] - cidx[:, None], -(NA_WIN_W - 1), NA_WIN_W - 1) + (NA_WIN_W - 1)
    n_rel = 2 * NA_WIN_W - 1
    expand = (np.arange(n_rel)[:, None, None] == col_idx[None]) & in_win[None]
    expand = jnp.asarray(expand.reshape(n_rel, GRID_W * GRID_W), F32)
    blocks = jnp.einsum("lhrd,dx->lhrx", rpb.astype(F32), expand, precision=lax.Precision.HIGHEST)
    blocks = blocks + jnp.asarray(np.where(in_win, 0.0, NEG_INF).reshape(-1), F32)
    blocks = blocks.reshape(*lead, 2 * NA_WIN_H - 1, GRID_W, GRID_W)
    masked = jnp.full((*lead, GRID_W, GRID_W), NEG_INF, F32)
    tables = []
    for layout in _na_variants(grid_rows)[0]:
        per_query_row = [jnp.concatenate([masked if ri is None else blocks[:, :, ri] for ri in row], axis=-1)
                         for row in layout]
        tables.append(jnp.concatenate(per_query_row, axis=2))
    return jnp.stack(tables, axis=2)


def _na_attention(z, bias, l, dims):
    tt = z.shape[0]
    b, n, c = dims["b"], dims["n"], dims["c"]
    heads = dims["na_heads"]
    n_var = bias.shape[2]
    ctx_blk0 = dims["bn"] // c
    kern = functools.partial(_na_kernel, grid_rows=n // GRID_W, scale=NA_HEAD_DIM ** -0.5)
    return pl.pallas_call(
        kern,
        out_shape=jax.ShapeDtypeStruct((tt, heads * NA_HEAD_DIM), BF16),
        grid=(heads, b),
        in_specs=[
            pl.BlockSpec((n, NA_HEAD_DIM), lambda h, bi: (bi, h)),
            pl.BlockSpec((n, NA_HEAD_DIM), lambda h, bi: (bi, heads + h)),
            pl.BlockSpec((n, NA_HEAD_DIM), lambda h, bi: (bi, 2 * heads + h)),
            pl.BlockSpec((c, NA_HEAD_DIM), lambda h, bi: (ctx_blk0 + bi, heads + h)),
            pl.BlockSpec((c, NA_HEAD_DIM), lambda h, bi: (ctx_blk0 + bi, 2 * heads + h)),
            pl.BlockSpec((None, None, n_var, NA_QROWS * GRID_W, NA_KROWS * GRID_W),
                         lambda h, bi: (l, h, 0, 0, 0)),
        ],
        out_specs=pl.BlockSpec((n, NA_HEAD_DIM), lambda h, bi: (bi, h)),
        compiler_params=_cparams(("parallel", "parallel"), 48),
        name="na_attention",
    )(z, z, z, z, z, bias)


def _ctx_attn_kernel(q_ref, k_ref, v_ref, prev_ref, o_ref, *, scale):
    del prev_ref
    nt = (((1,), (1,)), ((), ()))
    s = lax.dot_general(q_ref[...], k_ref[...], nt, preferred_element_type=F32) * scale
    m = jnp.max(s, axis=-1, keepdims=True)
    p = jnp.exp(s - m)
    denom = jnp.sum(p, axis=-1, keepdims=True)
    o = jnp.dot(p.astype(BF16), v_ref[...], preferred_element_type=F32)
    o_ref[...] = (o * (1.0 / denom)).astype(o_ref.dtype)


def _ctx_attention(q_arr, k_arr, v_arr, out_buf, dims, heads, dq, q_col0, k_col0, v_col0, v_step, scale, name):
    b, c = dims["b"], dims["c"]
    blk0 = dims["bn"] // c
    dv = out_buf.shape[1] // heads
    return pl.pallas_call(
        functools.partial(_ctx_attn_kernel, scale=scale),
        out_shape=jax.ShapeDtypeStruct(out_buf.shape, out_buf.dtype),
        grid=(b, heads),
        in_specs=[
            pl.BlockSpec((c, dq), lambda bi, h: (blk0 + bi, q_col0 + h)),
            pl.BlockSpec((c, dq), lambda bi, h: (blk0 + bi, k_col0 + h)),
            pl.BlockSpec((c, dv), lambda bi, h: (blk0 + bi, v_col0 + v_step * h)),
            pl.BlockSpec(memory_space=pl.ANY),
        ],
        out_specs=pl.BlockSpec((c, dv), lambda bi, h: (blk0 + bi, h)),
        input_output_aliases={3: 0},
        compiler_params=_cparams(("parallel", "parallel"), 32),
        name=name,
    )(q_arr, k_arr, v_arr, out_buf)


def _lru_kernel(*refs, reverse, t_chunk, n_ctx_chunks, n_lat_chunks, n_blocks, block_dim):
    if reverse:
        u_ref, wa_ref, ba_ref, wx_ref, bx_ref, lam_ref, hf_ref, g_ref, o_ref, carry_ref = refs
    else:
        (x_ref, xp_ref, xn_ref, cw_ref, cb_ref, wa_ref, ba_ref, wx_ref, bx_ref, lam_ref,
         o_ref, u_ref, carry_ref) = refs
    j = pl.program_id(1)

    @pl.when(j == 0)
    def _():
        carry_ref[...] = jnp.zeros_like(carry_ref)

    if reverse:
        u = u_ref[...]
    else:
        is_ctx = j < n_ctx_chunks
        chunk = jnp.where(is_ctx, j, j - n_ctx_chunks)
        first = chunk == 0
        last = chunk == jnp.where(is_ctx, n_ctx_chunks, n_lat_chunks) - 1
        xm = x_ref[...].astype(F32)
        xp = xp_ref[...].astype(F32)
        xn = xn_ref[...].astype(F32)
        hp = xp.shape[0]
        pm1 = jnp.where(first, 0.0, xp[hp - 1:hp, :])
        pm2 = jnp.where(first, 0.0, xp[hp - 2:hp - 1, :])
        nn0 = jnp.where(last, 0.0, xn[0:1, :])
        t = lax.broadcasted_iota(jnp.int32, (t_chunk, 1), 0)
        xm1 = jnp.where(t == 0, pm1, pltpu.roll(xm, 1, 0))
        xm2 = jnp.where(t == 0, pm2, jnp.where(t == 1, pm1, pltpu.roll(xm, 2, 0)))
        xp1 = jnp.where(t == t_chunk - 1, nn0, pltpu.roll(xm, t_chunk - 1, 0))
        cw = cw_ref[...]
        u = cw[0:1] * xm2 + cw[1:2] * xm1 + cw[2:3] * xm + cw[3:4] * xp1 + cb_ref[...]
        u_ref[...] = u

    ub = u.astype(BF16)
    ga, gx = [], []
    for kb in range(n_blocks):
        blk = ub[:, kb * block_dim:(kb + 1) * block_dim]
        ga.append(jnp.dot(blk, wa_ref[kb], preferred_element_type=F32))
        gx.append(jnp.dot(blk, wx_ref[kb], preferred_element_type=F32))
    gate_a = jnp.concatenate(ga, axis=1) + ba_ref[...]
    gate_x = jnp.concatenate(gx, axis=1) + bx_ref[...]
    r = jax.nn.sigmoid(gate_a)
    i_gate = jax.nn.sigmoid(gate_x)
    nlam = -lam_ref[...]
    softplus = jnp.maximum(nlam, 0.0) + jnp.log1p(jnp.exp(-jnp.abs(nlam)))
    log_a = -LRU_C * r * softplus
    a = jnp.exp(log_a)
    bv = jnp.sqrt(-jnp.tanh(log_a) * (a * a + 1.0)) * (i_gate * u)

    grp = SUBLANE_F32
    sub = lax.broadcasted_iota(jnp.int32, (grp, 1), 0)
    order = range(t_chunk // grp - 1, -1, -1) if reverse else range(t_chunk // grp)
    carry = carry_ref[0:1, :]
    h_groups = {}
    for gi in order:
        a_g = a[gi * grp:(gi + 1) * grp, :]
        b_g = bv[gi * grp:(gi + 1) * grp, :]
        k = 1
        while k < grp:
            keep = (sub < grp - k) if reverse else (sub >= k)
            shift = grp - k if reverse else k
            b_g = jnp.where(keep, a_g * pltpu.roll(b_g, shift, 0) + b_g, b_g)
            a_g = jnp.where(keep, a_g * pltpu.roll(a_g, shift, 0), a_g)
            k *= 2
        h_g = a_g * carry + b_g
        carry = h_g[0:1, :] if reverse else h_g[grp - 1:grp, :]
        h_groups[gi] = h_g
    h = jnp.concatenate([h_groups[gi] for gi in range(t_chunk // grp)], axis=0)
    carry_ref[0:1, :] = carry
    if reverse:
        g = g_ref[...].astype(F32)
        o_ref[...] = (jax.nn.gelu(g) * (hf_ref[...] + h)).astype(o_ref.dtype)
    else:
        o_ref[...] = h


def _lru_scan(z, l, direction, conv_w, conv_b, w_a, b_a, w_x, b_x, lam, fwd_out, dims):
    tt = z.shape[0]
    b, n, c, lw = dims["b"], dims["n"], dims["c"], dims["lru_width"]
    n_blocks, block_dim = w_a.shape[2], w_a.shape[3]
    t_chunk = _tile(c, 256, SUBLANE_BF16)
    halo = SUBLANE_BF16
    n_ctx_chunks, n_lat_chunks = c // t_chunk, n // t_chunk
    ctx_chunk0 = dims["bn"] // t_chunk
    col_x = dims["off_lru_x"] // lw
    col_g = dims["off_lru_g"] // lw
    reverse = direction == 1
    per_halo = t_chunk // halo
    n_halo_blocks = tt // halo

    def chunk_block(bi, j):
        is_ctx = j < n_ctx_chunks
        pos = jnp.where(is_ctx, j, j - n_ctx_chunks)
        if reverse:
            pos = jnp.where(is_ctx, n_ctx_chunks - 1 - pos, n_lat_chunks - 1 - pos)
        return jnp.where(is_ctx, ctx_chunk0 + bi * n_ctx_chunks + pos, bi * n_lat_chunks + pos)

    def prev_block(bi, j):
        return jnp.maximum(chunk_block(bi, j) * per_halo - 1, 0)

    def next_block(bi, j):
        return jnp.minimum((chunk_block(bi, j) + 1) * per_halo, n_halo_blocks - 1)

    vec = lambda: pl.BlockSpec((None, None, 1, lw), lambda bi, j: (l, direction, 0, 0))
    gate_w = lambda: pl.BlockSpec((None, None, n_blocks, block_dim, block_dim),
                                  lambda bi, j: (l, direction, 0, 0, 0))
    chunk_spec = lambda col: pl.BlockSpec((t_chunk, lw), lambda bi, j: (chunk_block(bi, j), col))
    gate_specs = [gate_w(), vec(), gate_w(), vec(), vec()]
    gate_args = [w_a, b_a, w_x, b_x, lam]
    if reverse:
        h_fwd, u = fwd_out
        in_specs = [chunk_spec(0)] + gate_specs + [chunk_spec(0), chunk_spec(col_g)]
        args = [u] + gate_args + [h_fwd, z]
        out_shape = jax.ShapeDtypeStruct((tt, lw), BF16)
        out_specs = chunk_spec(0)
    else:
        in_specs = [
            chunk_spec(col_x),
            pl.BlockSpec((halo, lw), lambda bi, j: (prev_block(bi, j), col_x)),
            pl.BlockSpec((halo, lw), lambda bi, j: (next_block(bi, j), col_x)),
            pl.BlockSpec((None, conv_w.shape[1], lw), lambda bi, j: (l, 0, 0)),
            pl.BlockSpec((None, 1, lw), lambda bi, j: (l, 0, 0)),
        ] + gate_specs
        args = [z, z, z, conv_w, conv_b] + gate_args
        out_shape = (jax.ShapeDtypeStruct((tt, lw), F32), jax.ShapeDtypeStruct((tt, lw), F32))
        out_specs = (chunk_spec(0), chunk_spec(0))
    kern = functools.partial(_lru_kernel, reverse=reverse, t_chunk=t_chunk, n_ctx_chunks=n_ctx_chunks,
                             n_lat_chunks=n_lat_chunks, n_blocks=n_blocks, block_dim=block_dim)
    return pl.pallas_call(
        kern,
        out_shape=out_shape,
        grid=(b, n_ctx_chunks + n_lat_chunks),
        in_specs=in_specs,
        out_specs=out_specs,
        scratch_shapes=[pltpu.VMEM((SUBLANE_F32, lw), F32)],
        compiler_params=_cparams(("parallel", "arbitrary"), 40),
        name="lru_bwd" if reverse else "lru_fwd",
    )(*args)


def _mla_proj_kernel(z_ref, qn_ref, kvn_ref, wq_ref, wkv_ref, ct_ref, st_ref, q_ref, k_ref, v_ref,
                     *, q_rank, kv_rank, heads, q_scale):
    z = z_ref[...]
    cq = z[:, :q_rank].astype(F32)
    ckv = z[:, q_rank:q_rank + kv_rank].astype(F32)
    krp = z[:, q_rank + kv_rank:q_rank + kv_rank + LANE].astype(F32)
    ct = ct_ref[...]
    st = st_ref[...]

    def rope(tile):
        return tile * ct + pltpu.roll(tile, LANE // 2, 1) * st

    qn = _rmsnorm_f32(cq, qn_ref[...]).astype(BF16)
    q = jnp.dot(qn, wq_ref[...], preferred_element_type=F32)
    kvn = _rmsnorm_f32(ckv, kvn_ref[...]).astype(BF16)
    kv = jnp.dot(kvn, wkv_ref[...], preferred_element_type=F32)
    kr = rope(krp).astype(BF16)
    for h in range(heads):
        c0 = h * MLA_HEAD_PAD
        q_ref[:, c0:c0 + LANE] = (q[:, c0:c0 + LANE] * q_scale).astype(BF16)
        q_ref[:, c0 + LANE:c0 + 2 * LANE] = (rope(q[:, c0 + LANE:c0 + 2 * LANE]) * q_scale).astype(BF16)
        k_ref[:, c0:c0 + LANE] = kv[:, h * LANE:(h + 1) * LANE].astype(BF16)
        k_ref[:, c0 + LANE:c0 + 2 * LANE] = kr
        v_ref[:, c0:c0 + LANE] = kv[:, (heads + h) * LANE:(heads + h + 1) * LANE].astype(BF16)
        v_ref[:, c0 + LANE:c0 + 2 * LANE] = jnp.ones((z.shape[0], LANE), BF16)


def _mla_proj(z, l, q_norm, kv_norm, wq, wkv, ct, st, dims):
    tt = z.shape[0]
    heads, q_rank, kv_rank = dims["mla_heads"], dims["q_rank"], dims["kv_rank"]
    mg = dims["mla_group"]
    tm = _tile(dims["tile_rows"], 512, SUBLANE_BF16)
    col = dims["off_mla"] // mg
    kern = functools.partial(_mla_proj_kernel, q_rank=q_rank, kv_rank=kv_rank, heads=heads,
                             q_scale=MLA_QK_DIM ** -0.5)
    return pl.pallas_call(
        kern,
        out_shape=(jax.ShapeDtypeStruct((tt, heads * MLA_HEAD_PAD), BF16),
                   jax.ShapeDtypeStruct((tt, heads * MLA_HEAD_PAD), BF16),
                   jax.ShapeDtypeStruct((tt, heads * MLA_HEAD_PAD), BF16)),
        grid=(tt // tm,),
        in_specs=[
            pl.BlockSpec((tm, mg), lambda i: (i, col)),
            pl.BlockSpec((None, 1, q_rank), lambda i: (l, 0, 0)),
            pl.BlockSpec((None, 1, kv_rank), lambda i: (l, 0, 0)),
            pl.BlockSpec((None, q_rank, heads * MLA_HEAD_PAD), lambda i: (l, 0, 0)),
            pl.BlockSpec((None, kv_rank, heads * 2 * LANE), lambda i: (l, 0, 0)),
            pl.BlockSpec((tm, LANE), lambda i: (i, 0)),
            pl.BlockSpec((tm, LANE), lambda i: (i, 0)),
        ],
        out_specs=(pl.BlockSpec((tm, heads * MLA_HEAD_PAD), lambda i: (i, 0)),
                   pl.BlockSpec((tm, heads * MLA_HEAD_PAD), lambda i: (i, 0)),
                   pl.BlockSpec((tm, heads * MLA_HEAD_PAD), lambda i: (i, 0))),
        compiler_params=_cparams(("parallel",), 48),
        name="mla_proj",
    )(z, q_norm, kv_norm, wq, wkv, ct, st)


def _mla_flash_kernel(q_ref, kt_ref, v_ref, ktc_ref, vc_ref, o_ref, *, sub, tk):
    n_chunks = kt_ref.shape[1] // tk
    chunks = [None] + list(range(n_chunks))

    def scores(q, ch):
        kt = ktc_ref[...] if ch is None else kt_ref[:, ch * tk:(ch + 1) * tk]
        return jnp.dot(q, kt, preferred_element_type=F32)

    def values(ch):
        return vc_ref[...] if ch is None else v_ref[ch * tk:(ch + 1) * tk, :]

    for a in range(q_ref.shape[0] // sub):
        rows = slice(a * sub, (a + 1) * sub)
        q = q_ref[rows, :]
        s_next = scores(q, chunks[0])
        m = acc = None
        for idx, ch in enumerate(chunks):
            s = s_next
            if idx + 1 < len(chunks):
                s_next = scores(q, chunks[idx + 1])
            m_new = jnp.max(s, axis=-1, keepdims=True)
            if m is not None:
                m_new = jnp.maximum(m, m_new)
            p = jnp.exp(s - m_new).astype(BF16)
            pv = jnp.dot(p, values(ch), preferred_element_type=F32)
            acc = pv if m is None else jnp.exp(m - m_new) * acc + pv
            m = m_new
        o_ref[rows, :] = (acc[:, :MLA_V_DIM] * (1.0 / acc[:, MLA_V_DIM:])).astype(o_ref.dtype)


def _mla_attention(q, kt, v, dims):
    tt = q.shape[0]
    b, n, c, heads = dims["b"], dims["n"], dims["c"], dims["mla_heads"]
    tq = _tile(n, 1024, SUBLANE_BF16)
    sub = _tile(tq, 512, SUBLANE_BF16)
    ctx_blk0 = dims["bn"] // c
    nq = n // tq
    vw = 2 * MLA_V_DIM
    return pl.pallas_call(
        functools.partial(_mla_flash_kernel, sub=sub, tk=_tile(n, 2048, LANE)),
        out_shape=jax.ShapeDtypeStruct((tt, heads * MLA_V_DIM), BF16),
        grid=(b, heads, nq),
        in_specs=[
            pl.BlockSpec((tq, MLA_HEAD_PAD), lambda bi, h, i: (bi * nq + i, h)),
            pl.BlockSpec((None, MLA_HEAD_PAD, n), lambda bi, h, i: (h, 0, bi)),
            pl.BlockSpec((n, vw), lambda bi, h, i: (bi, h)),
            pl.BlockSpec((None, MLA_HEAD_PAD, c), lambda bi, h, i: (h, 0, ctx_blk0 + bi)),
            pl.BlockSpec((c, vw), lambda bi, h, i: (ctx_blk0 + bi, h)),
        ],
        out_specs=pl.BlockSpec((tq, MLA_V_DIM), lambda bi, h, i: (bi * nq + i, h)),
        compiler_params=_cparams(("parallel", "parallel", "arbitrary"), 48),
        name="mla_attention",
    )(q, kt, v, kt, v)


def _merge_out_kernel(ba_ref, bb_ref, bc_ref, wb_ref, ga_ref, gb_ref, gc_ref, wo_ref, x_ref, g1_ref, o_ref):
    j = pl.program_id(1)

    @pl.when(j == 0)
    def _():
        o_ref[...] = jnp.zeros_like(o_ref)

    y = None
    for br, gate, i in ((ba_ref, ga_ref, 0), (bb_ref, gb_ref, 1), (bc_ref, gc_ref, 2)):
        t = jnp.dot(br[...], wb_ref[i], preferred_element_type=F32)
        t = jax.nn.sigmoid(gate[...].astype(F32)) * t
        y = t if y is None else y + t
    o_ref[...] += jnp.dot(y.astype(BF16), wo_ref[...], preferred_element_type=F32)

    @pl.when(j == pl.num_programs(1) - 1)
    def _():
        o_ref[...] = x_ref[...] + g1_ref[0] * o_ref[...]


def _merge_out(br_a, br_b, br_c, w_branch, z, w_out, xall, gate1, l, n_rows, dims):
    d = dims["d"]
    bw = br_a.shape[1]
    tm = _tile(dims["tile_rows"], 512, SUBLANE_BF16)
    tn = _tile(d, 512, LANE)
    gate_col0 = dims["off_gate"] // tn
    per_gate = d // tn
    grp = functools.partial(_group_of_tile, tm=tm, bn=dims["bn"], n=dims["n"])
    branch = lambda: pl.BlockSpec((tm, bw), lambda i, j: (i, 0))
    gate = lambda k: pl.BlockSpec((tm, tn), lambda i, j: (i, gate_col0 + k * per_gate + j))
    return pl.pallas_call(
        _merge_out_kernel,
        out_shape=jax.ShapeDtypeStruct((n_rows, d), F32),
        grid=(n_rows // tm, d // tn),
        in_specs=[branch(), branch(), branch(),
                  pl.BlockSpec((None, 3, bw, tn), lambda i, j: (l, 0, 0, j)),
                  gate(0), gate(1), gate(2),
                  pl.BlockSpec((None, tn, d), lambda i, j: (l, j, 0)),
                  pl.BlockSpec((tm, d), lambda i, j: (i, 0)),
                  pl.BlockSpec((1, 1, d), lambda i, j: (grp(i), 0, 0))],
        out_specs=pl.BlockSpec((tm, d), lambda i, j: (i, 0)),
        compiler_params=_cparams(("parallel", "arbitrary"), 48),
        name="merge_out",
    )(br_a, br_b, br_c, w_branch, z, z, z, w_out, xall, gate1)


def _ffn_kernel(x_ref, xp_ref, xn_ref, gam_ref, sh_ref, sc_ref, g2_ref, wv_ref, wg_ref, cwv_ref, cwg_ref,
                cbv_ref, cbg_ref, wd_ref, nf_ref, *rest, tm, tiles_per_seq, final_norm):
    o_ref, h_ref = rest[-2:]
    i = pl.program_id(0)
    f = pl.program_id(1)
    halo = SUBLANE_F32

    def norm_mod(x):
        y = _rmsnorm_f32(x, gam_ref[...])
        return y * (1.0 + sc_ref[0]) + sh_ref[0]

    @pl.when(f == 0)
    def _():
        h_ref[0:tm, :] = norm_mod(x_ref[...]).astype(BF16)
        hh = norm_mod(jnp.concatenate([xn_ref[...], xp_ref[...]], axis=0))
        pos = i % tiles_per_seq
        keep_next = (pos != tiles_per_seq - 1).astype(F32)
        keep_prev = (pos != 0).astype(F32)
        is_next = lax.broadcasted_iota(jnp.int32, (2 * halo, 1), 0) < halo
        h_ref[tm:tm + 2 * halo, :] = (hh * jnp.where(is_next, keep_next, keep_prev)).astype(BF16)
        o_ref[...] = jnp.zeros_like(o_ref)

    hv = h_ref[...]
    rows = tm + 2 * halo

    def conv(w_ref, cw_ref, cb_ref):
        zz = jnp.dot(hv, w_ref[...], preferred_element_type=F32)
        cw = cw_ref[...]
        zm1 = pltpu.roll(zz, 1, 0)[0:tm]
        zp1 = pltpu.roll(zz, rows - 1, 0)[0:tm]
        return cw[0:1] * zm1 + cw[1:2] * zz[0:tm] + cw[2:3] * zp1 + cb_ref[...]

    val = conv(wv_ref, cwv_ref, cbv_ref)
    gate = conv(wg_ref, cwg_ref, cbg_ref)
    act = (gate * jax.nn.sigmoid(gate) * val).astype(BF16)
    o_ref[...] += jnp.dot(act, wd_ref[...], preferred_element_type=F32)

    @pl.when(f == pl.num_programs(1) - 1)
    def _():
        xo = x_ref[...] + g2_ref[0] * o_ref[...]
        if final_norm:
            xo = _rmsnorm_f32(xo, nf_ref[...])
        o_ref[...] = xo


def _ffn(xall, out_buf, gamma, shift, scale, gate2, w_up, conv_w, conv_b, w_down, norm_final, l, row0, n_rows,
         seq_len, out_rows, final_norm, dims):
    tt, d = xall.shape
    ff = w_down.shape[1]
    tm = _tile(seq_len, 512, SUBLANE_BF16)
    tf = _tile(ff, 512, LANE)
    nf = ff // tf
    halo = SUBLANE_F32
    per_halo = tm // halo
    n_halo_blocks = tt // halo
    tile0 = row0 // tm
    assert row0 % tm == 0 and tm % halo == 0
    grp = lambda i: _group_of_tile(tile0 + i, tm=tm, bn=dims["bn"], n=dims["n"])
    mod = lambda: pl.BlockSpec((1, 1, d), lambda i, f: (grp(i), 0, 0))
    kern = functools.partial(_ffn_kernel, tm=tm, tiles_per_seq=seq_len // tm, final_norm=final_norm)
    args = [xall, xall, xall, gamma, shift, scale, gate2, w_up, w_up, conv_w, conv_w, conv_b, conv_b, w_down,
            norm_final]
    extra_specs, aliases = [], {}
    if out_buf is not None:
        aliases = {len(args): 0}
        args.append(out_buf)
        extra_specs = [pl.BlockSpec(memory_space=pl.ANY)]
    return pl.pallas_call(
        kern,
        out_shape=jax.ShapeDtypeStruct((out_rows, d), F32),
        grid=(n_rows // tm, nf),
        in_specs=[
            pl.BlockSpec((tm, d), lambda i, f: (tile0 + i, 0)),
            pl.BlockSpec((halo, d), lambda i, f: (jnp.maximum((tile0 + i) * per_halo - 1, 0), 0)),
            pl.BlockSpec((halo, d), lambda i, f: (jnp.minimum((tile0 + i + 1) * per_halo, n_halo_blocks - 1), 0)),
            pl.BlockSpec((None, 1, d), lambda i, f: (l, 0, 0)),
            mod(), mod(), mod(),
            pl.BlockSpec((None, d, tf), lambda i, f: (l, 0, f)),
            pl.BlockSpec((None, d, tf), lambda i, f: (l, 0, nf + f)),
            pl.BlockSpec((None, conv_w.shape[1], tf), lambda i, f: (l, 0, f)),
            pl.BlockSpec((None, conv_w.shape[1], tf), lambda i, f: (l, 0, nf + f)),
            pl.BlockSpec((None, 1, tf), lambda i, f: (l, 0, f)),
            pl.BlockSpec((None, 1, tf), lambda i, f: (l, 0, nf + f)),
            pl.BlockSpec((None, tf, d), lambda i, f: (l, f, 0)),
            pl.BlockSpec((1, d), lambda i, f: (0, 0)),
        ] + extra_specs,
        out_specs=pl.BlockSpec((tm, d), lambda i, f: (tile0 + i, 0)),
        scratch_shapes=[pltpu.VMEM((tm + 2 * halo, d), BF16)],
        input_output_aliases=aliases,
        compiler_params=_cparams(("parallel", "arbitrary"), 52),
        name="conv_ffn",
    )(*args)


def _rope_tables(n, b, c):
    t = jnp.arange(n, dtype=jnp.int32)
    row = (t // GRID_W).astype(F32)
    col = (t % GRID_W).astype(F32)
    n_freq = MLA_ROPE_DIM // 4
    inv_freq = ROPE_THETA ** (-jnp.arange(n_freq, dtype=F32) / n_freq)
    ang = jnp.concatenate([row[:, None] * inv_freq, col[:, None] * inv_freq], axis=-1)
    cos, sin = jnp.cos(ang), jnp.sin(ang)
    zeros = jnp.zeros((n, LANE - MLA_ROPE_DIM), F32)
    ct = jnp.concatenate([cos, cos, zeros], axis=-1)
    st = jnp.concatenate([-sin, sin, zeros], axis=-1)
    ct_ctx = jnp.concatenate([jnp.ones((b * c, MLA_ROPE_DIM), F32), jnp.zeros((b * c, LANE - MLA_ROPE_DIM), F32)], -1)
    ct = jnp.concatenate([jnp.tile(ct, (b, 1)), ct_ctx], axis=0)
    st = jnp.concatenate([jnp.tile(st, (b, 1)), jnp.zeros((b * c, LANE), F32)], axis=0)
    return ct, st


def _swap_halves(w):
    half = w.shape[-1] // 2
    return jnp.concatenate([w[..., half:], w[..., :half]], axis=-1)


def kernel(x, c, ctx, c_ctx, w_mod, b_mod, norm_mix, norm_ffn, w_in, na_rpb, lru_conv_w, lru_conv_b, lru_w_a,
           lru_b_a, lru_w_x, lru_b_x, lru_lam, mla_q_norm, mla_kv_norm, mla_w_q_up, mla_w_kv_up, w_branch, w_out,
           ffn_w_up, ffn_conv_w, ffn_conv_b, ffn_w_down, norm_final):
    b, n, d = x.shape
    ctx_len = ctx.shape[1]
    depth = w_mod.shape[0]
    na_heads = na_rpb.shape[1]
    na_width = na_heads * NA_HEAD_DIM
    lru_width = lru_conv_w.shape[2]
    q_rank = mla_q_norm.shape[1]
    kv_rank = mla_kv_norm.shape[1]
    mla_heads = mla_w_q_up.shape[2] // MLA_QK_DIM
    bn, bc = b * n, b * ctx_len
    assert na_width == lru_width == mla_heads * MLA_V_DIM == w_branch.shape[2]
    assert n % (NA_KROWS * GRID_W) == 0 and n % ctx_len == 0

    off_lru_x = 3 * na_width
    off_lru_g = off_lru_x + lru_width
    off_mla = off_lru_g + lru_width
    mla_used = q_rank + kv_rank + 2 * MLA_ROPE_DIM
    mla_group = -(-mla_used // LANE) * LANE
    while off_mla % mla_group:
        mla_group += LANE
    off_gate = off_mla + mla_group
    dims = dict(b=b, n=n, c=ctx_len, d=d, bn=bn, na_heads=na_heads, lru_width=lru_width, q_rank=q_rank,
                kv_rank=kv_rank, mla_heads=mla_heads, off_lru_x=off_lru_x, off_lru_g=off_lru_g, off_mla=off_mla,
                mla_group=mla_group, off_gate=off_gate, tile_rows=int(np.gcd(n, bc)))

    o_kr = off_mla + q_rank + kv_rank
    w_in_p = jnp.concatenate([
        w_in[:, :, :o_kr + MLA_ROPE_DIM],
        _swap_halves(w_in[:, :, o_kr:o_kr + MLA_ROPE_DIM]),
        jnp.zeros((depth, d, mla_group - mla_used), F32),
        w_in[:, :, o_kr + MLA_ROPE_DIM:],
    ], axis=-1).astype(BF16)
    wq = mla_w_q_up.reshape(depth, q_rank, mla_heads, MLA_QK_DIM)
    wq_rope = wq[..., MLA_NOPE_DIM:]
    wq_p = jnp.concatenate([wq[..., :MLA_NOPE_DIM], wq_rope, _swap_halves(wq_rope)], axis=-1)
    wq_p = wq_p.reshape(depth, q_rank, mla_heads * MLA_HEAD_PAD).astype(BF16)
    wkv = mla_w_kv_up.reshape(depth, kv_rank, mla_heads, MLA_NOPE_DIM + MLA_V_DIM)
    wkv_p = jnp.concatenate([wkv[..., :MLA_NOPE_DIM].reshape(depth, kv_rank, -1),
                             wkv[..., MLA_NOPE_DIM:].reshape(depth, kv_rank, -1)], axis=-1).astype(BF16)
    w_branch_b = w_branch.astype(BF16)
    w_out_b = w_out.astype(BF16)
    w_up_b = ffn_w_up.astype(BF16)
    w_down_b = ffn_w_down.astype(BF16)
    lru_w_a_b = lru_w_a.astype(BF16)
    lru_w_x_b = lru_w_x.astype(BF16)
    ct, st = _rope_tables(n, b, ctx_len)
    na_bias = _na_bias_tables(na_rpb, n // GRID_W)

    n_groups = 1 + b
    pad_rows = -n_groups % SUBLANE_F32
    cond = jnp.concatenate([c_ctx[None, :], c, jnp.zeros((pad_rows, d), F32)], axis=0)
    mod = _modulation(cond, w_mod, b_mod)[:, :n_groups].reshape(depth, n_groups, N_MOD, 1, d)

    r3 = lambda a: a.reshape(a.shape[0], 1, a.shape[1])
    r4 = lambda a: a.reshape(a.shape[0], a.shape[1], 1, a.shape[2])
    norm_mix3, norm_ffn3 = r3(norm_mix), r3(norm_ffn)
    lru_conv_b3 = r3(lru_conv_b)
    lru_b_a4, lru_b_x4, lru_lam4 = r4(lru_b_a), r4(lru_b_x), r4(lru_lam)
    q_norm3, kv_norm3 = r3(mla_q_norm), r3(mla_kv_norm)
    ffn_conv_b3 = r3(ffn_conv_b)
    norm_final2 = norm_final.reshape(1, d)

    xall = jnp.concatenate([x.reshape(bn, d), ctx.reshape(bc, d)], axis=0)
    for l in range(depth):
        last = l == depth - 1
        n_rows = bn if last else bn + bc
        sh1, sc1, g1, sh2, sc2, g2 = (mod[l, :, k] for k in range(N_MOD))
        z = _in_proj(xall, norm_mix3, sh1, sc1, w_in_p, l, dims)
        out_a = _na_attention(z, na_bias, l, dims)
        lru_args = (lru_conv_w, lru_conv_b3, lru_w_a_b, lru_b_a4, lru_w_x_b, lru_b_x4, lru_lam4)
        lru_fwd = _lru_scan(z, l, 0, *lru_args, None, dims)
        out_b = _lru_scan(z, l, 1, *lru_args, lru_fwd, dims)
        q_m, k_m, v_m = _mla_proj(z, l, q_norm3, kv_norm3, wq_p, wkv_p, ct, st, dims)
        kt_m = k_m.reshape(bn + bc, mla_heads, MLA_HEAD_PAD).transpose(1, 2, 0)
        out_c = _mla_attention(q_m, kt_m, v_m, dims)
        if not last:
            out_a = _ctx_attention(z, z, z, out_a, dims, na_heads, NA_HEAD_DIM, 0, na_heads, 2 * na_heads, 1,
                                   NA_HEAD_DIM ** -0.5, "na_ctx_attention")
            out_c = _ctx_attention(q_m, k_m, v_m, out_c, dims, mla_heads, MLA_HEAD_PAD, 0, 0, 0, 2, 1.0,
                                   "mla_ctx_attention")
        xall = _merge_out(out_a, out_b, out_c, w_branch_b, z, w_out_b, xall, g1, l, n_rows, dims)
        ffn_args = (norm_ffn3, sh2, sc2, g2, w_up_b, ffn_conv_w, ffn_conv_b3, w_down_b, norm_final2, l)
        x_new = _ffn(xall, None, *ffn_args, 0, bn, n, n_rows, last, dims)
        if not last:
            x_new = _ffn(xall, x_new, *ffn_args, bn, bc, ctx_len, n_rows, False, dims)
        xall = x_new
    return xall.reshape(b, n, d)
```

```python
import functools

import numpy as np
import jax
import jax.numpy as jnp
from jax import lax
from jax.experimental import pallas as pl
from jax.experimental.pallas import tpu as pltpu

F32 = jnp.float32
BF16 = jnp.bfloat16

GRID_W = 64
NORM_EPS = 1e-6
NEG_INF = -1e30
N_MOD = 6
NA_HEAD_DIM = 128
NA_WIN_H = 8
NA_WIN_W = 16
LRU_C = 8.0
MLA_NOPE_DIM = 128
MLA_ROPE_DIM = 64
MLA_V_DIM = 128
MLA_QK_DIM = MLA_NOPE_DIM + MLA_ROPE_DIM
ROPE_THETA = 10000.0

LANE = 128
SUBLANE_F32 = 8
SUBLANE_BF16 = 16
MLA_HEAD_PAD = 256
NA_QROWS = 8
NA_KROWS = 16
MIB = 1024 * 1024


def _cparams(semantics, vmem_mib):
    return pltpu.CompilerParams(dimension_semantics=semantics, vmem_limit_bytes=vmem_mib * MIB)


def _tile(n, target, mult):
    best = None
    for t in range(mult, min(n, target) + 1, mult):
        if n % t == 0:
            best = t
    assert best is not None, (n, target, mult)
    return best


def _group_of_tile(i, tm, bn, n):
    return jnp.where(i * tm >= bn, 0, 1 + (i * tm) // n)


def _rmsnorm_f32(x, gamma):
    ms = jnp.mean(x * x, axis=-1, keepdims=True)
    return x * lax.rsqrt(ms + NORM_EPS) * gamma


def _mod_kernel(c_ref, w_ref, b_ref, o_ref):
    cc = c_ref[...]
    s = cc * jax.nn.sigmoid(cc)
    o_ref[...] = jnp.dot(s, w_ref[...], preferred_element_type=F32,
                         precision=lax.Precision.HIGHEST) + b_ref[...]


def _modulation(cond, w_mod, b_mod):
    depth, d, nm = w_mod.shape
    rows = cond.shape[0]
    tn = _tile(nm, 1024, LANE)
    return pl.pallas_call(
        _mod_kernel,
        out_shape=jax.ShapeDtypeStruct((depth, rows, nm), F32),
        grid=(depth, nm // tn),
        in_specs=[
            pl.BlockSpec((rows, d), lambda l, j: (0, 0)),
            pl.BlockSpec((None, d, tn), lambda l, j: (l, 0, j)),
            pl.BlockSpec((None, 1, tn), lambda l, j: (l, 0, j)),
        ],
        out_specs=pl.BlockSpec((None, rows, tn), lambda l, j: (l, 0, j)),
        compiler_params=_cparams(("arbitrary", "arbitrary"), 40),
        name="adaln_mod",
    )(cond, w_mod, b_mod.reshape(depth, 1, nm))


def _in_proj_kernel(x_ref, gam_ref, sh_ref, sc_ref, w_ref, o_ref, h_ref):
    @pl.when(pl.program_id(1) == 0)
    def _():
        y = _rmsnorm_f32(x_ref[...], gam_ref[...])
        h_ref[...] = (y * (1.0 + sc_ref[0]) + sh_ref[0]).astype(BF16)

    o_ref[...] = jnp.dot(h_ref[...], w_ref[...], preferred_element_type=F32).astype(o_ref.dtype)


def _in_proj(xall, gamma, shift, scale, w, l, dims):
    tt, d = xall.shape
    npad = w.shape[2]
    tm = _tile(dims["tile_rows"], 1024, SUBLANE_BF16)
    tn = _tile(npad, 1024, LANE)
    grp = functools.partial(_group_of_tile, tm=tm, bn=dims["bn"], n=dims["n"])
    return pl.pallas_call(
        _in_proj_kernel,
        out_shape=jax.ShapeDtypeStruct((tt, npad), BF16),
        grid=(tt // tm, npad // tn),
        in_specs=[
            pl.BlockSpec((tm, d), lambda i, j: (i, 0)),
            pl.BlockSpec((None, 1, d), lambda i, j: (l, 0, 0)),
            pl.BlockSpec((1, 1, d), lambda i, j: (grp(i), 0, 0)),
            pl.BlockSpec((1, 1, d), lambda i, j: (grp(i), 0, 0)),
            pl.BlockSpec((None, d, tn), lambda i, j: (l, 0, j)),
        ],
        out_specs=pl.BlockSpec((tm, tn), lambda i, j: (i, j)),
        scratch_shapes=[pltpu.VMEM((tm, d), BF16)],
        compiler_params=_cparams(("parallel", "arbitrary"), 48),
        name="in_proj",
    )(xall, gamma, shift, scale, w)


def _na_group_layout(g, grid_rows):
    kh = min(NA_WIN_H, grid_rows)
    k0 = int(np.clip(NA_QROWS * g - NA_WIN_H // 2, 0, grid_rows - NA_KROWS))
    layout = []
    for qi in range(NA_QROWS):
        r = NA_QROWS * g + qi
        first_key_row = int(np.clip(r - kh // 2, 0, grid_rows - kh))
        layout.append(tuple((k0 + kj) - r + NA_WIN_H - 1 if first_key_row <= k0 + kj < first_key_row + kh
                            else None for kj in range(NA_KROWS)))
    return k0, tuple(layout)


def _na_variants(grid_rows):
    layouts = [_na_group_layout(g, grid_rows)[1] for g in range(grid_rows // NA_QROWS)]
    distinct = sorted(set(layouts), key=layouts.index)
    return distinct, [distinct.index(lay) for lay in layouts]


def _na_kernel(q_ref, k_ref, v_ref, kc_ref, vc_ref, bias_ref, o_ref, *, grid_rows, scale):
    n_groups = grid_rows // NA_QROWS
    variant_of_group = _na_variants(grid_rows)[1]
    tq = NA_QROWS * GRID_W
    nk = NA_KROWS * GRID_W
    nt = (((1,), (1,)), ((), ()))

    def key_start(g):
        return _na_group_layout(g, grid_rows)[0] * GRID_W

    def scores(g):
        q = q_ref[g * tq:(g + 1) * tq, :]
        k0 = key_start(g)
        var = variant_of_group[g]
        s = lax.dot_general(q, k_ref[k0:k0 + nk, :], nt, preferred_element_type=F32) * scale + bias_ref[var]
        sc = lax.dot_general(q, kc_ref[...], nt, preferred_element_type=F32) * scale
        return s, sc

    nxt = scores(0)
    for g in range(n_groups):
        s, sc = nxt
        if g + 1 < n_groups:
            nxt = scores(g + 1)
        k0 = key_start(g)
        m = jnp.maximum(jnp.max(s, axis=-1, keepdims=True), jnp.max(sc, axis=-1, keepdims=True))
        p = jnp.exp(s - m)
        pc = jnp.exp(sc - m)
        denom = jnp.sum(p, axis=-1, keepdims=True) + jnp.sum(pc, axis=-1, keepdims=True)
        o = jnp.dot(p.astype(BF16), v_ref[k0:k0 + nk, :], preferred_element_type=F32)
        o = o + jnp.dot(pc.astype(BF16), vc_ref[...], preferred_element_type=F32)
        o_ref[g * tq:(g + 1) * tq, :] = (o * (1.0 / denom)).astype(o_ref.dtype)


def _na_bias_tables(rpb, grid_rows):
    lead = rpb.shape[:2]
    cidx = np.arange(GRID_W)
    c_start = np.clip(cidx - NA_WIN_W // 2, 0, GRID_W - NA_WIN_W)
    in_win = (cidx[None, :] >= c_start[:, None]) & (cidx[None, :] < c_start[:, None] + NA_WIN_W)
    col_idx = np.clip(cidx[None, :] - cidx[:, None], -(NA_WIN_W - 1), NA_WIN_W - 1) + (NA_WIN_W - 1)
    n_rel = 2 * NA_WIN_W - 1
    expand = (np.arange(n_rel)[:, None, None] == col_idx[None]) & in_win[None]
    expand = jnp.asarray(expand.reshape(n_rel, GRID_W * GRID_W), F32)
    blocks = jnp.einsum("lhrd,dx->lhrx", rpb.astype(F32), expand, precision=lax.Precision.HIGHEST)
    blocks = blocks + jnp.asarray(np.where(in_win, 0.0, NEG_INF).reshape(-1), F32)
    blocks = blocks.reshape(*lead, 2 * NA_WIN_H - 1, GRID_W, GRID_W)
    masked = jnp.full((*lead, GRID_W, GRID_W), NEG_INF, F32)
    tables = []
    for layout in _na_variants(grid_rows)[0]:
        per_query_row = [jnp.concatenate([masked if ri is None else blocks[:, :, ri] for ri in row], axis=-1)
                         for row in layout]
        tables.append(jnp.concatenate(per_query_row, axis=2))
    return jnp.stack(tables, axis=2)


def _na_attention(z, bias, l, dims):
    tt = z.shape[0]
    b, n, c = dims["b"], dims["n"], dims["c"]
    heads = dims["na_heads"]
    n_var = bias.shape[2]
    ctx_blk0 = dims["bn"] // c
    kern = functools.partial(_na_kernel, grid_rows=n // GRID_W, scale=NA_HEAD_DIM ** -0.5)
    return pl.pallas_call(
        kern,
        out_shape=jax.ShapeDtypeStruct((tt, heads * NA_HEAD_DIM), BF16),
        grid=(heads, b),
        in_specs=[
            pl.BlockSpec((n, NA_HEAD_DIM), lambda h, bi: (bi, h)),
            pl.BlockSpec((n, NA_HEAD_DIM), lambda h, bi: (bi, heads + h)),
            pl.BlockSpec((n, NA_HEAD_DIM), lambda h, bi: (bi, 2 * heads + h)),
            pl.BlockSpec((c, NA_HEAD_DIM), lambda h, bi: (ctx_blk0 + bi, heads + h)),
            pl.BlockSpec((c, NA_HEAD_DIM), lambda h, bi: (ctx_blk0 + bi, 2 * heads + h)),
            pl.BlockSpec((None, None, n_var, NA_QROWS * GRID_W, NA_KROWS * GRID_W),
                         lambda h, bi: (l, h, 0, 0, 0)),
        ],
        out_specs=pl.BlockSpec((n, NA_HEAD_DIM), lambda h, bi: (bi, h)),
        compiler_params=_cparams(("parallel", "parallel"), 48),
        name="na_attention",
    )(z, z, z, z, z, bias)


def _ctx_attn_kernel(q_ref, k_ref, v_ref, prev_ref, o_ref, *, scale):
    del prev_ref
    nt = (((1,), (1,)), ((), ()))
    s = lax.dot_general(q_ref[...], k_ref[...], nt, preferred_element_type=F32) * scale
    m = jnp.max(s, axis=-1, keepdims=True)
    p = jnp.exp(s - m)
    denom = jnp.sum(p, axis=-1, keepdims=True)
    o = jnp.dot(p.astype(BF16), v_ref[...], preferred_element_type=F32)
    o_ref[...] = (o * (1.0 / denom)).astype(o_ref.dtype)


def _ctx_attention(q_arr, k_arr, v_arr, out_buf, dims, heads, dq, q_col0, k_col0, v_col0, v_step, scale, name):
    b, c = dims["b"], dims["c"]
    blk0 = dims["bn"] // c
    dv = out_buf.shape[1] // heads
    return pl.pallas_call(
        functools.partial(_ctx_attn_kernel, scale=scale),
        out_shape=jax.ShapeDtypeStruct(out_buf.shape, out_buf.dtype),
        grid=(b, heads),
        in_specs=[
            pl.BlockSpec((c, dq), lambda bi, h: (blk0 + bi, q_col0 + h)),
            pl.BlockSpec((c, dq), lambda bi, h: (blk0 + bi, k_col0 + h)),
            pl.BlockSpec((c, dv), lambda bi, h: (blk0 + bi, v_col0 + v_step * h)),
            pl.BlockSpec(memory_space=pl.ANY),
        ],
        out_specs=pl.BlockSpec((c, dv), lambda bi, h: (blk0 + bi, h)),
        input_output_aliases={3: 0},
        compiler_params=_cparams(("parallel", "parallel"), 32),
        name=name,
    )(q_arr, k_arr, v_arr, out_buf)


def _lru_kernel(*refs, reverse, t_chunk, n_ctx_chunks, n_lat_chunks, n_blocks, block_dim):
    if reverse:
        u_ref, wa_ref, ba_ref, wx_ref, bx_ref, lam_ref, hf_ref, g_ref, o_ref, carry_ref = refs
    else:
        (x_ref, xp_ref, xn_ref, cw_ref, cb_ref, wa_ref, ba_ref, wx_ref, bx_ref, lam_ref,
         o_ref, u_ref, carry_ref) = refs
    j = pl.program_id(1)

    @pl.when(j == 0)
    def _():
        carry_ref[...] = jnp.zeros_like(carry_ref)

    if reverse:
        u = u_ref[...]
    else:
        is_ctx = j < n_ctx_chunks
        chunk = jnp.where(is_ctx, j, j - n_ctx_chunks)
        first = chunk == 0
        last = chunk == jnp.where(is_ctx, n_ctx_chunks, n_lat_chunks) - 1
        xm = x_ref[...].astype(F32)
        xp = xp_ref[...].astype(F32)
        xn = xn_ref[...].astype(F32)
        hp = xp.shape[0]
        pm1 = jnp.where(first, 0.0, xp[hp - 1:hp, :])
        pm2 = jnp.where(first, 0.0, xp[hp - 2:hp - 1, :])
        nn0 = jnp.where(last, 0.0, xn[0:1, :])
        t = lax.broadcasted_iota(jnp.int32, (t_chunk, 1), 0)
        xm1 = jnp.where(t == 0, pm1, pltpu.roll(xm, 1, 0))
        xm2 = jnp.where(t == 0, pm2, jnp.where(t == 1, pm1, pltpu.roll(xm, 2, 0)))
        xp1 = jnp.where(t == t_chunk - 1, nn0, pltpu.roll(xm, t_chunk - 1, 0))
        cw = cw_ref[...]
        u = cw[0:1] * xm2 + cw[1:2] * xm1 + cw[2:3] * xm + cw[3:4] * xp1 + cb_ref[...]
        u_ref[...] = u

    ub = u.astype(BF16)
    ga, gx = [], []
    for kb in range(n_blocks):
        blk = ub[:, kb * block_dim:(kb + 1) * block_dim]
        ga.append(jnp.dot(blk, wa_ref[kb], preferred_element_type=F32))
        gx.append(jnp.dot(blk, wx_ref[kb], preferred_element_type=F32))
    gate_a = jnp.concatenate(ga, axis=1) + ba_ref[...]
    gate_x = jnp.concatenate(gx, axis=1) + bx_ref[...]
    r = jax.nn.sigmoid(gate_a)
    i_gate = jax.nn.sigmoid(gate_x)
    nlam = -lam_ref[...]
    softplus = jnp.maximum(nlam, 0.0) + jnp.log1p(jnp.exp(-jnp.abs(nlam)))
    log_a = -LRU_C * r * softplus
    a = jnp.exp(log_a)
    bv = jnp.sqrt(-jnp.tanh(log_a) * (a * a + 1.0)) * (i_gate * u)

    grp = SUBLANE_F32
    sub = lax.broadcasted_iota(jnp.int32, (grp, 1), 0)
    order = range(t_chunk // grp - 1, -1, -1) if reverse else range(t_chunk // grp)
    carry = carry_ref[0:1, :]
    h_groups = {}
    for gi in order:
        a_g = a[gi * grp:(gi + 1) * grp, :]
        b_g = bv[gi * grp:(gi + 1) * grp, :]
        k = 1
        while k < grp:
            keep = (sub < grp - k) if reverse else (sub >= k)
            shift = grp - k if reverse else k
            b_g = jnp.where(keep, a_g * pltpu.roll(b_g, shift, 0) + b_g, b_g)
            a_g = jnp.where(keep, a_g * pltpu.roll(a_g, shift, 0), a_g)
            k *= 2
        h_g = a_g * carry + b_g
        carry = h_g[0:1, :] if reverse else h_g[grp - 1:grp, :]
        h_groups[gi] = h_g
    h = jnp.concatenate([h_groups[gi] for gi in range(t_chunk // grp)], axis=0)
    carry_ref[0:1, :] = carry
    if reverse:
        g = g_ref[...].astype(F32)
        o_ref[...] = (jax.nn.gelu(g) * (hf_ref[...] + h)).astype(o_ref.dtype)
    else:
        o_ref[...] = h


def _lru_scan(z, l, direction, conv_w, conv_b, w_a, b_a, w_x, b_x, lam, fwd_out, dims):
    tt = z.shape[0]
    b, n, c, lw = dims["b"], dims["n"], dims["c"], dims["lru_width"]
    n_blocks, block_dim = w_a.shape[2], w_a.shape[3]
    t_chunk = _tile(c, 256, SUBLANE_BF16)
    halo = SUBLANE_BF16
    n_ctx_chunks, n_lat_chunks = c // t_chunk, n // t_chunk
    ctx_chunk0 = dims["bn"] // t_chunk
    col_x = dims["off_lru_x"] // lw
    col_g = dims["off_lru_g"] // lw
    reverse = direction == 1
    per_halo = t_chunk // halo
    n_halo_blocks = tt // halo

    def chunk_block(bi, j):
        is_ctx = j < n_ctx_chunks
        pos = jnp.where(is_ctx, j, j - n_ctx_chunks)
        if reverse:
            pos = jnp.where(is_ctx, n_ctx_chunks - 1 - pos, n_lat_chunks - 1 - pos)
        return jnp.where(is_ctx, ctx_chunk0 + bi * n_ctx_chunks + pos, bi * n_lat_chunks + pos)

    def prev_block(bi, j):
        return jnp.maximum(chunk_block(bi, j) * per_halo - 1, 0)

    def next_block(bi, j):
        return jnp.minimum((chunk_block(bi, j) + 1) * per_halo, n_halo_blocks - 1)

    vec = lambda: pl.BlockSpec((None, None, 1, lw), lambda bi, j: (l, direction, 0, 0))
    gate_w = lambda: pl.BlockSpec((None, None, n_blocks, block_dim, block_dim),
                                  lambda bi, j: (l, direction, 0, 0, 0))
    chunk_spec = lambda col: pl.BlockSpec((t_chunk, lw), lambda bi, j: (chunk_block(bi, j), col))
    gate_specs = [gate_w(), vec(), gate_w(), vec(), vec()]
    gate_args = [w_a, b_a, w_x, b_x, lam]
    if reverse:
        h_fwd, u = fwd_out
        in_specs = [chunk_spec(0)] + gate_specs + [chunk_spec(0), chunk_spec(col_g)]
        args = [u] + gate_args + [h_fwd, z]
        out_shape = jax.ShapeDtypeStruct((tt, lw), BF16)
        out_specs = chunk_spec(0)
    else:
        in_specs = [
            chunk_spec(col_x),
            pl.BlockSpec((halo, lw), lambda bi, j: (prev_block(bi, j), col_x)),
            pl.BlockSpec((halo, lw), lambda bi, j: (next_block(bi, j), col_x)),
            pl.BlockSpec((None, conv_w.shape[1], lw), lambda bi, j: (l, 0, 0)),
            pl.BlockSpec((None, 1, lw), lambda bi, j: (l, 0, 0)),
        ] + gate_specs
        args = [z, z, z, conv_w, conv_b] + gate_args
        out_shape = (jax.ShapeDtypeStruct((tt, lw), F32), jax.ShapeDtypeStruct((tt, lw), F32))
        out_specs = (chunk_spec(0), chunk_spec(0))
    kern = functools.partial(_lru_kernel, reverse=reverse, t_chunk=t_chunk, n_ctx_chunks=n_ctx_chunks,
                             n_lat_chunks=n_lat_chunks, n_blocks=n_blocks, block_dim=block_dim)
    return pl.pallas_call(
        kern,
        out_shape=out_shape,
        grid=(b, n_ctx_chunks + n_lat_chunks),
        in_specs=in_specs,
        out_specs=out_specs,
        scratch_shapes=[pltpu.VMEM((SUBLANE_F32, lw), F32)],
        compiler_params=_cparams(("parallel", "arbitrary"), 40),
        name="lru_bwd" if reverse else "lru_fwd",
    )(*args)


def _mla_proj_kernel(z_ref, qn_ref, kvn_ref, wq_ref, wkv_ref, ct_ref, st_ref, q_ref, k_ref, v_ref,
                     *, q_rank, kv_rank, heads, q_scale):
    z = z_ref[...]
    cq = z[:, :q_rank].astype(F32)
    ckv = z[:, q_rank:q_rank + kv_rank].astype(F32)
    krp = z[:, q_rank + kv_rank:q_rank + kv_rank + LANE].astype(F32)
    ct = ct_ref[...]
    st = st_ref[...]

    def rope(tile):
        return tile * ct + pltpu.roll(tile, LANE // 2, 1) * st

    qn = _rmsnorm_f32(cq, qn_ref[...]).astype(BF16)
    q = jnp.dot(qn, wq_ref[...], preferred_element_type=F32)
    kvn = _rmsnorm_f32(ckv, kvn_ref[...]).astype(BF16)
    kv = jnp.dot(kvn, wkv_ref[...], preferred_element_type=F32)
    kr = rope(krp).astype(BF16)
    for h in range(heads):
        c0 = h * MLA_HEAD_PAD
        q_ref[:, c0:c0 + LANE] = (q[:, c0:c0 + LANE] * q_scale).astype(BF16)
        q_ref[:, c0 + LANE:c0 + 2 * LANE] = (rope(q[:, c0 + LANE:c0 + 2 * LANE]) * q_scale).astype(BF16)
        k_ref[:, c0:c0 + LANE] = kv[:, h * LANE:(h + 1) * LANE].astype(BF16)
        k_ref[:, c0 + LANE:c0 + 2 * LANE] = kr
        v_ref[:, c0:c0 + LANE] = kv[:, (heads + h) * LANE:(heads + h + 1) * LANE].astype(BF16)
        v_ref[:, c0 + LANE:c0 + 2 * LANE] = jnp.ones((z.shape[0], LANE), BF16)


def _mla_proj(z, l, q_norm, kv_norm, wq, wkv, ct, st, dims):
    tt = z.shape[0]
    heads, q_rank, kv_rank = dims["mla_heads"], dims["q_rank"], dims["kv_rank"]
    mg = dims["mla_group"]
    tm = _tile(dims["tile_rows"], 512, SUBLANE_BF16)
    col = dims["off_mla"] // mg
    kern = functools.partial(_mla_proj_kernel, q_rank=q_rank, kv_rank=kv_rank, heads=heads,
                             q_scale=MLA_QK_DIM ** -0.5)
    return pl.pallas_call(
        kern,
        out_shape=(jax.ShapeDtypeStruct((tt, heads * MLA_HEAD_PAD), BF16),
                   jax.ShapeDtypeStruct((tt, heads * MLA_HEAD_PAD), BF16),
                   jax.ShapeDtypeStruct((tt, heads * MLA_HEAD_PAD), BF16)),
        grid=(tt // tm,),
        in_specs=[
            pl.BlockSpec((tm, mg), lambda i: (i, col)),
            pl.BlockSpec((None, 1, q_rank), lambda i: (l, 0, 0)),
            pl.BlockSpec((None, 1, kv_rank), lambda i: (l, 0, 0)),
            pl.BlockSpec((None, q_rank, heads * MLA_HEAD_PAD), lambda i: (l, 0, 0)),
            pl.BlockSpec((None, kv_rank, heads * 2 * LANE), lambda i: (l, 0, 0)),
            pl.BlockSpec((tm, LANE), lambda i: (i, 0)),
            pl.BlockSpec((tm, LANE), lambda i: (i, 0)),
        ],
        out_specs=(pl.BlockSpec((tm, heads * MLA_HEAD_PAD), lambda i: (i, 0)),
                   pl.BlockSpec((tm, heads * MLA_HEAD_PAD), lambda i: (i, 0)),
                   pl.BlockSpec((tm, heads * MLA_HEAD_PAD), lambda i: (i, 0))),
        compiler_params=_cparams(("parallel",), 48),
        name="mla_proj",
    )(z, q_norm, kv_norm, wq, wkv, ct, st)


def _mla_flash_kernel(q_ref, kt_ref, v_ref, ktc_ref, vc_ref, o_ref, *, sub, tk):
    n_chunks = kt_ref.shape[1] // tk
    chunks = [None] + list(range(n_chunks))

    def scores(q, ch):
        kt = ktc_ref[...] if ch is None else kt_ref[:, ch * tk:(ch + 1) * tk]
        return jnp.dot(q, kt, preferred_element_type=F32)

    def values(ch):
        return vc_ref[...] if ch is None else v_ref[ch * tk:(ch + 1) * tk, :]

    for a in range(q_ref.shape[0] // sub):
        rows = slice(a * sub, (a + 1) * sub)
        q = q_ref[rows, :]
        s_next = scores(q, chunks[0])
        m = acc = None
        for idx, ch in enumerate(chunks):
            s = s_next
            if idx + 1 < len(chunks):
                s_next = scores(q, chunks[idx + 1])
            m_new = jnp.max(s, axis=-1, keepdims=True)
            if m is not None:
                m_new = jnp.maximum(m, m_new)
            p = jnp.exp(s - m_new).astype(BF16)
            pv = jnp.dot(p, values(ch), preferred_element_type=F32)
            acc = pv if m is None else jnp.exp(m - m_new) * acc + pv
            m = m_new
        o_ref[rows, :] = (acc[:, :MLA_V_DIM] * (1.0 / acc[:, MLA_V_DIM:])).astype(o_ref.dtype)


def _mla_attention(q, kt, v, dims):
    tt = q.shape[0]
    b, n, c, heads = dims["b"], dims["n"], dims["c"], dims["mla_heads"]
    tq = _tile(n, 1024, SUBLANE_BF16)
    sub = _tile(tq, 512, SUBLANE_BF16)
    ctx_blk0 = dims["bn"] // c
    nq = n // tq
    vw = 2 * MLA_V_DIM
    return pl.pallas_call(
        functools.partial(_mla_flash_kernel, sub=sub, tk=_tile(n, 2048, LANE)),
        out_shape=jax.ShapeDtypeStruct((tt, heads * MLA_V_DIM), BF16),
        grid=(b, heads, nq),
        in_specs=[
            pl.BlockSpec((tq, MLA_HEAD_PAD), lambda bi, h, i: (bi * nq + i, h)),
            pl.BlockSpec((None, MLA_HEAD_PAD, n), lambda bi, h, i: (h, 0, bi)),
            pl.BlockSpec((n, vw), lambda bi, h, i: (bi, h)),
            pl.BlockSpec((None, MLA_HEAD_PAD, c), lambda bi, h, i: (h, 0, ctx_blk0 + bi)),
            pl.BlockSpec((c, vw), lambda bi, h, i: (ctx_blk0 + bi, h)),
        ],
        out_specs=pl.BlockSpec((tq, MLA_V_DIM), lambda bi, h, i: (bi * nq + i, h)),
        compiler_params=_cparams(("parallel", "parallel", "arbitrary"), 48),
        name="mla_attention",
    )(q, kt, v, kt, v)


def _merge_kernel(ba_ref, bb_ref, bc_ref, wb_ref, ga_ref, gb_ref, gc_ref, o_ref):
    y = None
    for br, gate, i in ((ba_ref, ga_ref, 0), (bb_ref, gb_ref, 1), (bc_ref, gc_ref, 2)):
        t = jnp.dot(br[...], wb_ref[i], preferred_element_type=F32)
        t = jax.nn.sigmoid(gate[...].astype(F32)) * t
        y = t if y is None else y + t
    o_ref[...] = y.astype(o_ref.dtype)


def _merge(br_a, br_b, br_c, w_branch, z, l, n_rows, dims):
    d = dims["d"]
    bw = br_a.shape[1]
    tm = _tile(dims["tile_rows"], 1024, SUBLANE_BF16)
    tn = _tile(d, 512, LANE)
    gate_col0 = dims["off_gate"] // tn
    per_gate = d // tn
    branch = lambda: pl.BlockSpec((tm, bw), lambda i, j: (i, 0))
    gate = lambda k: pl.BlockSpec((tm, tn), lambda i, j: (i, gate_col0 + k * per_gate + j))
    return pl.pallas_call(
        _merge_kernel,
        out_shape=jax.ShapeDtypeStruct((n_rows, d), BF16),
        grid=(n_rows // tm, d // tn),
        in_specs=[branch(), branch(), branch(),
                  pl.BlockSpec((None, 3, bw, tn), lambda i, j: (l, 0, 0, j)),
                  gate(0), gate(1), gate(2)],
        out_specs=pl.BlockSpec((tm, tn), lambda i, j: (i, j)),
        compiler_params=_cparams(("parallel", "arbitrary"), 48),
        name="merge",
    )(br_a, br_b, br_c, w_branch, z, z, z)


def _out_proj_kernel(y_ref, w_ref, x_ref, g_ref, o_ref):
    t = jnp.dot(y_ref[...], w_ref[...], preferred_element_type=F32)
    o_ref[...] = x_ref[...] + g_ref[0] * t


def _out_proj(y, w_out, xall, gate, l, dims):
    n_rows, d = y.shape
    tm = _tile(dims["tile_rows"], 1024, SUBLANE_BF16)
    tn = _tile(d, 512, LANE)
    grp = functools.partial(_group_of_tile, tm=tm, bn=dims["bn"], n=dims["n"])
    return pl.pallas_call(
        _out_proj_kernel,
        out_shape=jax.ShapeDtypeStruct((n_rows, d), F32),
        grid=(n_rows // tm, d // tn),
        in_specs=[
            pl.BlockSpec((tm, d), lambda i, j: (i, 0)),
            pl.BlockSpec((None, d, tn), lambda i, j: (l, 0, j)),
            pl.BlockSpec((tm, tn), lambda i, j: (i, j)),
            pl.BlockSpec((1, 1, tn), lambda i, j: (grp(i), 0, j)),
        ],
        out_specs=pl.BlockSpec((tm, tn), lambda i, j: (i, j)),
        compiler_params=_cparams(("parallel", "arbitrary"), 48),
        name="out_proj",
    )(y, w_out, xall, gate)


def _ffn_kernel(x_ref, xp_ref, xn_ref, gam_ref, sh_ref, sc_ref, g2_ref, wv_ref, wg_ref, cwv_ref, cwg_ref,
                cbv_ref, cbg_ref, wd_ref, nf_ref, *rest, tm, tiles_per_seq, final_norm):
    o_ref, h_ref = rest[-2:]
    i = pl.program_id(0)
    f = pl.program_id(1)
    halo = SUBLANE_F32

    def norm_mod(x):
        y = _rmsnorm_f32(x, gam_ref[...])
        return y * (1.0 + sc_ref[0]) + sh_ref[0]

    @pl.when(f == 0)
    def _():
        h_ref[0:tm, :] = norm_mod(x_ref[...]).astype(BF16)
        hh = norm_mod(jnp.concatenate([xn_ref[...], xp_ref[...]], axis=0))
        pos = i % tiles_per_seq
        keep_next = (pos != tiles_per_seq - 1).astype(F32)
        keep_prev = (pos != 0).astype(F32)
        is_next = lax.broadcasted_iota(jnp.int32, (2 * halo, 1), 0) < halo
        h_ref[tm:tm + 2 * halo, :] = (hh * jnp.where(is_next, keep_next, keep_prev)).astype(BF16)
        o_ref[...] = jnp.zeros_like(o_ref)

    hv = h_ref[...]
    rows = tm + 2 * halo

    def conv(w_ref, cw_ref, cb_ref):
        zz = jnp.dot(hv, w_ref[...], preferred_element_type=F32)
        cw = cw_ref[...]
        zm1 = pltpu.roll(zz, 1, 0)[0:tm]
        zp1 = pltpu.roll(zz, rows - 1, 0)[0:tm]
        return cw[0:1] * zm1 + cw[1:2] * zz[0:tm] + cw[2:3] * zp1 + cb_ref[...]

    val = conv(wv_ref, cwv_ref, cbv_ref)
    gate = conv(wg_ref, cwg_ref, cbg_ref)
    act = (gate * jax.nn.sigmoid(gate) * val).astype(BF16)
    o_ref[...] += jnp.dot(act, wd_ref[...], preferred_element_type=F32)

    @pl.when(f == pl.num_programs(1) - 1)
    def _():
        xo = x_ref[...] + g2_ref[0] * o_ref[...]
        if final_norm:
            xo = _rmsnorm_f32(xo, nf_ref[...])
        o_ref[...] = xo


def _ffn(xall, out_buf, gamma, shift, scale, gate2, w_up, conv_w, conv_b, w_down, norm_final, l, row0, n_rows,
         seq_len, out_rows, final_norm, dims):
    tt, d = xall.shape
    ff = w_down.shape[1]
    tm = _tile(seq_len, 1024, SUBLANE_BF16)
    tf = _tile(ff, 512, LANE)
    nf = ff // tf
    halo = SUBLANE_F32
    per_halo = tm // halo
    n_halo_blocks = tt // halo
    tile0 = row0 // tm
    assert row0 % tm == 0 and tm % halo == 0
    grp = lambda i: _group_of_tile(tile0 + i, tm=tm, bn=dims["bn"], n=dims["n"])
    mod = lambda: pl.BlockSpec((1, 1, d), lambda i, f: (grp(i), 0, 0))
    kern = functools.partial(_ffn_kernel, tm=tm, tiles_per_seq=seq_len // tm, final_norm=final_norm)
    args = [xall, xall, xall, gamma, shift, scale, gate2, w_up, w_up, conv_w, conv_w, conv_b, conv_b, w_down,
            norm_final]
    extra_specs, aliases = [], {}
    if out_buf is not None:
        aliases = {len(args): 0}
        args.append(out_buf)
        extra_specs = [pl.BlockSpec(memory_space=pl.ANY)]
    return pl.pallas_call(
        kern,
        out_shape=jax.ShapeDtypeStruct((out_rows, d), F32),
        grid=(n_rows // tm, nf),
        in_specs=[
            pl.BlockSpec((tm, d), lambda i, f: (tile0 + i, 0), pipeline_mode=pl.Buffered(1)),
            pl.BlockSpec((halo, d), lambda i, f: (jnp.maximum((tile0 + i) * per_halo - 1, 0), 0)),
            pl.BlockSpec((halo, d), lambda i, f: (jnp.minimum((tile0 + i + 1) * per_halo, n_halo_blocks - 1), 0)),
            pl.BlockSpec((None, 1, d), lambda i, f: (l, 0, 0)),
            mod(), mod(), mod(),
            pl.BlockSpec((None, d, tf), lambda i, f: (l, 0, f)),
            pl.BlockSpec((None, d, tf), lambda i, f: (l, 0, nf + f)),
            pl.BlockSpec((None, conv_w.shape[1], tf), lambda i, f: (l, 0, f)),
            pl.BlockSpec((None, conv_w.shape[1], tf), lambda i, f: (l, 0, nf + f)),
            pl.BlockSpec((None, 1, tf), lambda i, f: (l, 0, f)),
            pl.BlockSpec((None, 1, tf), lambda i, f: (l, 0, nf + f)),
            pl.BlockSpec((None, tf, d), lambda i, f: (l, f, 0)),
            pl.BlockSpec((1, d), lambda i, f: (0, 0)),
        ] + extra_specs,
        out_specs=pl.BlockSpec((tm, d), lambda i, f: (tile0 + i, 0)),
        scratch_shapes=[pltpu.VMEM((tm + 2 * halo, d), BF16)],
        input_output_aliases=aliases,
        compiler_params=_cparams(("parallel", "arbitrary"), 58),
        name="conv_ffn",
    )(*args)


def _rope_tables(n, b, c):
    t = jnp.arange(n, dtype=jnp.int32)
    row = (t // GRID_W).astype(F32)
    col = (t % GRID_W).astype(F32)
    n_freq = MLA_ROPE_DIM // 4
    inv_freq = ROPE_THETA ** (-jnp.arange(n_freq, dtype=F32) / n_freq)
    ang = jnp.concatenate([row[:, None] * inv_freq, col[:, None] * inv_freq], axis=-1)
    cos, sin = jnp.cos(ang), jnp.sin(ang)
    zeros = jnp.zeros((n, LANE - MLA_ROPE_DIM), F32)
    ct = jnp.concatenate([cos, cos, zeros], axis=-1)
    st = jnp.concatenate([-sin, sin, zeros], axis=-1)
    ct_ctx = jnp.concatenate([jnp.ones((b * c, MLA_ROPE_DIM), F32), jnp.zeros((b * c, LANE - MLA_ROPE_DIM), F32)], -1)
    ct = jnp.concatenate([jnp.tile(ct, (b, 1)), ct_ctx], axis=0)
    st = jnp.concatenate([jnp.tile(st, (b, 1)), jnp.zeros((b * c, LANE), F32)], axis=0)
    return ct, st


def _swap_halves(w):
    half = w.shape[-1] // 2
    return jnp.concatenate([w[..., half:], w[..., :half]], axis=-1)


def kernel(x, c, ctx, c_ctx, w_mod, b_mod, norm_mix, norm_ffn, w_in, na_rpb, lru_conv_w, lru_conv_b, lru_w_a,
           lru_b_a, lru_w_x, lru_b_x, lru_lam, mla_q_norm, mla_kv_norm, mla_w_q_up, mla_w_kv_up, w_branch, w_out,
           ffn_w_up, ffn_conv_w, ffn_conv_b, ffn_w_down, norm_final):
    b, n, d = x.shape
    ctx_len = ctx.shape[1]
    depth = w_mod.shape[0]
    na_heads = na_rpb.shape[1]
    na_width = na_heads * NA_HEAD_DIM
    lru_width = lru_conv_w.shape[2]
    q_rank = mla_q_norm.shape[1]
    kv_rank = mla_kv_norm.shape[1]
    mla_heads = mla_w_q_up.shape[2] // MLA_QK_DIM
    bn, bc = b * n, b * ctx_len
    assert na_width == lru_width == mla_heads * MLA_V_DIM == w_branch.shape[2]
    assert n % (NA_KROWS * GRID_W) == 0 and n % ctx_len == 0

    off_lru_x = 3 * na_width
    off_lru_g = off_lru_x + lru_width
    off_mla = off_lru_g + lru_width
    mla_used = q_rank + kv_rank + 2 * MLA_ROPE_DIM
    mla_group = -(-mla_used // LANE) * LANE
    while off_mla % mla_group:
        mla_group += LANE
    off_gate = off_mla + mla_group
    dims = dict(b=b, n=n, c=ctx_len, d=d, bn=bn, na_heads=na_heads, lru_width=lru_width, q_rank=q_rank,
                kv_rank=kv_rank, mla_heads=mla_heads, off_lru_x=off_lru_x, off_lru_g=off_lru_g, off_mla=off_mla,
                mla_group=mla_group, off_gate=off_gate, tile_rows=int(np.gcd(n, bc)))

    o_kr = off_mla + q_rank + kv_rank
    w_in_p = jnp.concatenate([
        w_in[:, :, :o_kr + MLA_ROPE_DIM],
        _swap_halves(w_in[:, :, o_kr:o_kr + MLA_ROPE_DIM]),
        jnp.zeros((depth, d, mla_group - mla_used), F32),
        w_in[:, :, o_kr + MLA_ROPE_DIM:],
    ], axis=-1).astype(BF16)
    wq = mla_w_q_up.reshape(depth, q_rank, mla_heads, MLA_QK_DIM)
    wq_rope = wq[..., MLA_NOPE_DIM:]
    wq_p = jnp.concatenate([wq[..., :MLA_NOPE_DIM], wq_rope, _swap_halves(wq_rope)], axis=-1)
    wq_p = wq_p.reshape(depth, q_rank, mla_heads * MLA_HEAD_PAD).astype(BF16)
    wkv = mla_w_kv_up.reshape(depth, kv_rank, mla_heads, MLA_NOPE_DIM + MLA_V_DIM)
    wkv_p = jnp.concatenate([wkv[..., :MLA_NOPE_DIM].reshape(depth, kv_rank, -1),
                             wkv[..., MLA_NOPE_DIM:].reshape(depth, kv_rank, -1)], axis=-1).astype(BF16)
    w_branch_b = w_branch.astype(BF16)
    w_out_b = w_out.astype(BF16)
    w_up_b = ffn_w_up.astype(BF16)
    w_down_b = ffn_w_down.astype(BF16)
    lru_w_a_b = lru_w_a.astype(BF16)
    lru_w_x_b = lru_w_x.astype(BF16)
    ct, st = _rope_tables(n, b, ctx_len)
    na_bias = _na_bias_tables(na_rpb, n // GRID_W)

    n_groups = 1 + b
    pad_rows = -n_groups % SUBLANE_F32
    cond = jnp.concatenate([c_ctx[None, :], c, jnp.zeros((pad_rows, d), F32)], axis=0)
    mod = _modulation(cond, w_mod, b_mod)[:, :n_groups].reshape(depth, n_groups, N_MOD, 1, d)

    r3 = lambda a: a.reshape(a.shape[0], 1, a.shape[1])
    r4 = lambda a: a.reshape(a.shape[0], a.shape[1], 1, a.shape[2])
    norm_mix3, norm_ffn3 = r3(norm_mix), r3(norm_ffn)
    lru_conv_b3 = r3(lru_conv_b)
    lru_b_a4, lru_b_x4, lru_lam4 = r4(lru_b_a), r4(lru_b_x), r4(lru_lam)
    q_norm3, kv_norm3 = r3(mla_q_norm), r3(mla_kv_norm)
    ffn_conv_b3 = r3(ffn_conv_b)
    norm_final2 = norm_final.reshape(1, d)

    xall = jnp.concatenate([x.reshape(bn, d), ctx.reshape(bc, d)], axis=0)
    for l in range(depth):
        last = l == depth - 1
        n_rows = bn if last else bn + bc
        sh1, sc1, g1, sh2, sc2, g2 = (mod[l, :, k] for k in range(N_MOD))
        z = _in_proj(xall, norm_mix3, sh1, sc1, w_in_p, l, dims)
        out_a = _na_attention(z, na_bias, l, dims)
        lru_args = (lru_conv_w, lru_conv_b3, lru_w_a_b, lru_b_a4, lru_w_x_b, lru_b_x4, lru_lam4)
        lru_fwd = _lru_scan(z, l, 0, *lru_args, None, dims)
        out_b = _lru_scan(z, l, 1, *lru_args, lru_fwd, dims)
        q_m, k_m, v_m = _mla_proj(z, l, q_norm3, kv_norm3, wq_p, wkv_p, ct, st, dims)
        kt_m = k_m.reshape(bn + bc, mla_heads, MLA_HEAD_PAD).transpose(1, 2, 0)
        out_c = _mla_attention(q_m, kt_m, v_m, dims)
        if not last:
            out_a = _ctx_attention(z, z, z, out_a, dims, na_heads, NA_HEAD_DIM, 0, na_heads, 2 * na_heads, 1,
                                   NA_HEAD_DIM ** -0.5, "na_ctx_attention")
            out_c = _ctx_attention(q_m, k_m, v_m, out_c, dims, mla_heads, MLA_HEAD_PAD, 0, 0, 0, 2, 1.0,
                                   "mla_ctx_attention")
        y = _merge(out_a, out_b, out_c, w_branch_b, z, l, n_rows, dims)
        xall = _out_proj(y, w_out_b, xall, g1, l, dims)
        ffn_args = (norm_ffn3, sh2, sc2, g2, w_up_b, ffn_conv_w, ffn_conv_b3, w_down_b, norm_final2, l)
        x_new = _ffn(xall, None, *ffn_args, 0, bn, n, n_rows, last, dims)
        if not last:
            x_new = _ffn(xall, x_new, *ffn_args, bn, bc, ctx_len, n_rows, False, dims)
        xall = x_new
    return xall.reshape(b, n, d)
```

```python
import functools

import numpy as np
import jax
import jax.numpy as jnp
from jax import lax
from jax.experimental import pallas as pl
from jax.experimental.pallas import tpu as pltpu

F32 = jnp.float32
BF16 = jnp.bfloat16

GRID_W = 64
NORM_EPS = 1e-6
NEG_INF = -1e30
N_MOD = 6
NA_HEAD_DIM = 128
NA_WIN_H = 8
NA_WIN_W = 16
LRU_C = 8.0
MLA_NOPE_DIM = 128
MLA_ROPE_DIM = 64
MLA_V_DIM = 128
MLA_QK_DIM = MLA_NOPE_DIM + MLA_ROPE_DIM
ROPE_THETA = 10000.0

LANE = 128
SUBLANE_F32 = 8
SUBLANE_BF16 = 16
MLA_HEAD_PAD = 256
NA_QROWS = 8
NA_KROWS = 16
MIB = 1024 * 1024


def _cparams(semantics, vmem_mib):
    return pltpu.CompilerParams(dimension_semantics=semantics, vmem_limit_bytes=vmem_mib * MIB)


def _tile(n, target, mult):
    best = None
    for t in range(mult, min(n, target) + 1, mult):
        if n % t == 0:
            best = t
    assert best is not None, (n, target, mult)
    return best


def _group_of_tile(i, tm, bn, n):
    return jnp.where(i * tm >= bn, 0, 1 + (i * tm) // n)


def _rmsnorm_f32(x, gamma):
    ms = jnp.mean(x * x, axis=-1, keepdims=True)
    return x * lax.rsqrt(ms + NORM_EPS) * gamma


def _mod_kernel(c_ref, w_ref, b_ref, o_ref):
    cc = c_ref[...]
    s = cc * jax.nn.sigmoid(cc)
    o_ref[...] = jnp.dot(s, w_ref[...], preferred_element_type=F32,
                         precision=lax.Precision.HIGHEST) + b_ref[...]


def _modulation(cond, w_mod, b_mod):
    depth, d, nm = w_mod.shape
    rows = cond.shape[0]
    tn = _tile(nm, 1024, LANE)
    return pl.pallas_call(
        _mod_kernel,
        out_shape=jax.ShapeDtypeStruct((depth, rows, nm), F32),
        grid=(depth, nm // tn),
        in_specs=[
            pl.BlockSpec((rows, d), lambda l, j: (0, 0)),
            pl.BlockSpec((None, d, tn), lambda l, j: (l, 0, j)),
            pl.BlockSpec((None, 1, tn), lambda l, j: (l, 0, j)),
        ],
        out_specs=pl.BlockSpec((None, rows, tn), lambda l, j: (l, 0, j)),
        compiler_params=_cparams(("arbitrary", "arbitrary"), 40),
        name="adaln_mod",
    )(cond, w_mod, b_mod.reshape(depth, 1, nm))


def _in_proj_kernel(x_ref, gam_ref, sh_ref, sc_ref, w_ref, wg_ref, o_ref, h_ref, *, n_head_tiles):
    j = pl.program_id(1)

    @pl.when(j == 0)
    def _():
        y = _rmsnorm_f32(x_ref[...], gam_ref[...])
        h_ref[...] = (y * (1.0 + sc_ref[0]) + sh_ref[0]).astype(BF16)

    @pl.when(j < n_head_tiles)
    def _():
        o_ref[...] = jnp.dot(h_ref[...], w_ref[...], preferred_element_type=F32).astype(o_ref.dtype)

    @pl.when(j >= n_head_tiles)
    def _():
        o_ref[...] = jnp.dot(h_ref[...], wg_ref[...], preferred_element_type=F32).astype(o_ref.dtype)


def _in_proj(xall, gamma, shift, scale, w, w_gate, l, dims):
    tt, d = xall.shape
    n_head, n_gate = dims["off_gate"], w_gate.shape[2]
    tm = _tile(dims["tile_rows"], 1024, SUBLANE_BF16)
    tn = _tile(int(np.gcd(n_head, n_gate)), 1024, LANE)
    n_head_tiles = n_head // tn
    assert n_head <= w.shape[2]
    grp = functools.partial(_group_of_tile, tm=tm, bn=dims["bn"], n=dims["n"])
    return pl.pallas_call(
        functools.partial(_in_proj_kernel, n_head_tiles=n_head_tiles),
        out_shape=jax.ShapeDtypeStruct((tt, n_head + n_gate), BF16),
        grid=(tt // tm, (n_head + n_gate) // tn),
        in_specs=[
            pl.BlockSpec((tm, d), lambda i, j: (i, 0)),
            pl.BlockSpec((None, 1, d), lambda i, j: (l, 0, 0)),
            pl.BlockSpec((1, 1, d), lambda i, j: (grp(i), 0, 0)),
            pl.BlockSpec((1, 1, d), lambda i, j: (grp(i), 0, 0)),
            pl.BlockSpec((None, d, tn), lambda i, j: (l, 0, jnp.minimum(j, n_head_tiles - 1))),
            pl.BlockSpec((None, d, tn), lambda i, j: (l, 0, jnp.maximum(j - n_head_tiles, 0))),
        ],
        out_specs=pl.BlockSpec((tm, tn), lambda i, j: (i, j)),
        scratch_shapes=[pltpu.VMEM((tm, d), BF16)],
        compiler_params=_cparams(("parallel", "arbitrary"), 48),
        name="in_proj",
    )(xall, gamma, shift, scale, w, w_gate)


def _na_group_layout(g, grid_rows):
    kh = min(NA_WIN_H, grid_rows)
    k0 = int(np.clip(NA_QROWS * g - NA_WIN_H // 2, 0, grid_rows - NA_KROWS))
    layout = []
    for qi in range(NA_QROWS):
        r = NA_QROWS * g + qi
        first_key_row = int(np.clip(r - kh // 2, 0, grid_rows - kh))
        layout.append(tuple((k0 + kj) - r + NA_WIN_H - 1 if first_key_row <= k0 + kj < first_key_row + kh
                            else None for kj in range(NA_KROWS)))
    return k0, tuple(layout)


def _na_variants(grid_rows):
    layouts = [_na_group_layout(g, grid_rows)[1] for g in range(grid_rows // NA_QROWS)]
    distinct = sorted(set(layouts), key=layouts.index)
    return distinct, [distinct.index(lay) for lay in layouts]


def _na_kernel(q_ref, k_ref, v_ref, kc_ref, vc_ref, bias_ref, o_ref, *, grid_rows, scale):
    n_groups = grid_rows // NA_QROWS
    variant_of_group = _na_variants(grid_rows)[1]
    tq = NA_QROWS * GRID_W
    nk = NA_KROWS * GRID_W
    nt = (((1,), (1,)), ((), ()))

    def key_start(g):
        return _na_group_layout(g, grid_rows)[0] * GRID_W

    def scores(g):
        q = q_ref[g * tq:(g + 1) * tq, :]
        k0 = key_start(g)
        var = variant_of_group[g]
        s = lax.dot_general(q, k_ref[k0:k0 + nk, :], nt, preferred_element_type=F32) * scale + bias_ref[var]
        sc = lax.dot_general(q, kc_ref[...], nt, preferred_element_type=F32) * scale
        return s, sc

    nxt = scores(0)
    for g in range(n_groups):
        s, sc = nxt
        if g + 1 < n_groups:
            nxt = scores(g + 1)
        k0 = key_start(g)
        m = jnp.maximum(jnp.max(s, axis=-1, keepdims=True), jnp.max(sc, axis=-1, keepdims=True))
        p = jnp.exp(s - m)
        pc = jnp.exp(sc - m)
        denom = jnp.sum(p, axis=-1, keepdims=True) + jnp.sum(pc, axis=-1, keepdims=True)
        o = jnp.dot(p.astype(BF16), v_ref[k0:k0 + nk, :], preferred_element_type=F32)
        o = o + jnp.dot(pc.astype(BF16), vc_ref[...], preferred_element_type=F32)
        o_ref[g * tq:(g + 1) * tq, :] = (o * (1.0 / denom)).astype(o_ref.dtype)


def _na_bias_tables(rpb, grid_rows):
    lead = rpb.shape[:2]
    cidx = np.arange(GRID_W)
    c_start = np.clip(cidx - NA_WIN_W // 2, 0, GRID_W - NA_WIN_W)
    in_win = (cidx[None, :] >= c_start[:, None]) & (cidx[None, :] < c_start[:, None] + NA_WIN_W)
    col_idx = np.clip(cidx[None, :] - cidx[:, None], -(NA_WIN_W - 1), NA_WIN_W - 1) + (NA_WIN_W - 1)
    n_rel = 2 * NA_WIN_W - 1
    expand = (np.arange(n_rel)[:, None, None] == col_idx[None]) & in_win[None]
    expand = jnp.asarray(expand.reshape(n_rel, GRID_W * GRID_W), F32)
    blocks = jnp.einsum("lhrd,dx->lhrx", rpb.astype(F32), expand, precision=lax.Precision.HIGHEST)
    blocks = blocks + jnp.asarray(np.where(in_win, 0.0, NEG_INF).reshape(-1), F32)
    blocks = blocks.reshape(*lead, 2 * NA_WIN_H - 1, GRID_W, GRID_W)
    masked = jnp.full((*lead, GRID_W, GRID_W), NEG_INF, F32)
    tables = []
    for layout in _na_variants(grid_rows)[0]:
        per_query_row = [jnp.concatenate([masked if ri is None else blocks[:, :, ri] for ri in row], axis=-1)
                         for row in layout]
        tables.append(jnp.concatenate(per_query_row, axis=2))
    return jnp.stack(tables, axis=2)


def _na_attention(z, bias, l, dims):
    tt = z.shape[0]
    b, n, c = dims["b"], dims["n"], dims["c"]
    heads = dims["na_heads"]
    n_var = bias.shape[2]
    ctx_blk0 = dims["bn"] // c
    kern = functools.partial(_na_kernel, grid_rows=n // GRID_W, scale=NA_HEAD_DIM ** -0.5)
    return pl.pallas_call(
        kern,
        out_shape=jax.ShapeDtypeStruct((tt, heads * NA_HEAD_DIM), BF16),
        grid=(heads, b),
        in_specs=[
            pl.BlockSpec((n, NA_HEAD_DIM), lambda h, bi: (bi, h)),
            pl.BlockSpec((n, NA_HEAD_DIM), lambda h, bi: (bi, heads + h)),
            pl.BlockSpec((n, NA_HEAD_DIM), lambda h, bi: (bi, 2 * heads + h)),
            pl.BlockSpec((c, NA_HEAD_DIM), lambda h, bi: (ctx_blk0 + bi, heads + h)),
            pl.BlockSpec((c, NA_HEAD_DIM), lambda h, bi: (ctx_blk0 + bi, 2 * heads + h)),
            pl.BlockSpec((None, None, n_var, NA_QROWS * GRID_W, NA_KROWS * GRID_W),
                         lambda h, bi: (l, h, 0, 0, 0)),
        ],
        out_specs=pl.BlockSpec((n, NA_HEAD_DIM), lambda h, bi: (bi, h)),
        compiler_params=_cparams(("parallel", "parallel"), 48),
        name="na_attention",
    )(z, z, z, z, z, bias)


def _ctx_attn_kernel(q_ref, k_ref, v_ref, prev_ref, o_ref, *, scale):
    del prev_ref
    nt = (((1,), (1,)), ((), ()))
    s = lax.dot_general(q_ref[...], k_ref[...], nt, preferred_element_type=F32) * scale
    m = jnp.max(s, axis=-1, keepdims=True)
    p = jnp.exp(s - m)
    denom = jnp.sum(p, axis=-1, keepdims=True)
    o = jnp.dot(p.astype(BF16), v_ref[...], preferred_element_type=F32)
    o_ref[...] = (o * (1.0 / denom)).astype(o_ref.dtype)


def _ctx_attention(q_arr, k_arr, v_arr, out_buf, dims, heads, dq, q_col0, k_col0, v_col0, v_step, scale, name):
    b, c = dims["b"], dims["c"]
    blk0 = dims["bn"] // c
    dv = out_buf.shape[1] // heads
    return pl.pallas_call(
        functools.partial(_ctx_attn_kernel, scale=scale),
        out_shape=jax.ShapeDtypeStruct(out_buf.shape, out_buf.dtype),
        grid=(b, heads),
        in_specs=[
            pl.BlockSpec((c, dq), lambda bi, h: (blk0 + bi, q_col0 + h)),
            pl.BlockSpec((c, dq), lambda bi, h: (blk0 + bi, k_col0 + h)),
            pl.BlockSpec((c, dv), lambda bi, h: (blk0 + bi, v_col0 + v_step * h)),
            pl.BlockSpec(memory_space=pl.ANY),
        ],
        out_specs=pl.BlockSpec((c, dv), lambda bi, h: (blk0 + bi, h)),
        input_output_aliases={3: 0},
        compiler_params=_cparams(("parallel", "parallel"), 32),
        name=name,
    )(q_arr, k_arr, v_arr, out_buf)


def _lru_kernel(*refs, reverse, t_chunk, n_ctx_chunks, n_lat_chunks, n_blocks, block_dim):
    if reverse:
        u_ref, wa_ref, ba_ref, wx_ref, bx_ref, lam_ref, hf_ref, g_ref, o_ref, carry_ref = refs
    else:
        (x_ref, xp_ref, xn_ref, cw_ref, cb_ref, wa_ref, ba_ref, wx_ref, bx_ref, lam_ref,
         o_ref, u_ref, carry_ref) = refs
    j = pl.program_id(1)

    @pl.when(j == 0)
    def _():
        carry_ref[...] = jnp.zeros_like(carry_ref)

    if reverse:
        u = u_ref[...]
    else:
        is_ctx = j < n_ctx_chunks
        chunk = jnp.where(is_ctx, j, j - n_ctx_chunks)
        first = chunk == 0
        last = chunk == jnp.where(is_ctx, n_ctx_chunks, n_lat_chunks) - 1
        xm = x_ref[...].astype(F32)
        xp = xp_ref[...].astype(F32)
        xn = xn_ref[...].astype(F32)
        hp = xp.shape[0]
        pm1 = jnp.where(first, 0.0, xp[hp - 1:hp, :])
        pm2 = jnp.where(first, 0.0, xp[hp - 2:hp - 1, :])
        nn0 = jnp.where(last, 0.0, xn[0:1, :])
        t = lax.broadcasted_iota(jnp.int32, (t_chunk, 1), 0)
        xm1 = jnp.where(t == 0, pm1, pltpu.roll(xm, 1, 0))
        xm2 = jnp.where(t == 0, pm2, jnp.where(t == 1, pm1, pltpu.roll(xm, 2, 0)))
        xp1 = jnp.where(t == t_chunk - 1, nn0, pltpu.roll(xm, t_chunk - 1, 0))
        cw = cw_ref[...]
        u = cw[0:1] * xm2 + cw[1:2] * xm1 + cw[2:3] * xm + cw[3:4] * xp1 + cb_ref[...]
        u_ref[...] = u

    ub = u.astype(BF16)
    ga, gx = [], []
    for kb in range(n_blocks):
        blk = ub[:, kb * block_dim:(kb + 1) * block_dim]
        ga.append(jnp.dot(blk, wa_ref[kb], preferred_element_type=F32))
        gx.append(jnp.dot(blk, wx_ref[kb], preferred_element_type=F32))
    gate_a = jnp.concatenate(ga, axis=1) + ba_ref[...]
    gate_x = jnp.concatenate(gx, axis=1) + bx_ref[...]
    r = jax.nn.sigmoid(gate_a)
    i_gate = jax.nn.sigmoid(gate_x)
    nlam = -lam_ref[...]
    softplus = jnp.maximum(nlam, 0.0) + jnp.log1p(jnp.exp(-jnp.abs(nlam)))
    log_a = -LRU_C * r * softplus
    a = jnp.exp(log_a)
    bv = jnp.sqrt(-jnp.tanh(log_a) * (a * a + 1.0)) * (i_gate * u)

    grp = SUBLANE_F32
    sub = lax.broadcasted_iota(jnp.int32, (grp, 1), 0)
    order = range(t_chunk // grp - 1, -1, -1) if reverse else range(t_chunk // grp)
    carry = carry_ref[0:1, :]
    h_groups = {}
    for gi in order:
        a_g = a[gi * grp:(gi + 1) * grp, :]
        b_g = bv[gi * grp:(gi + 1) * grp, :]
        k = 1
        while k < grp:
            keep = (sub < grp - k) if reverse else (sub >= k)
            shift = grp - k if reverse else k
            b_g = jnp.where(keep, a_g * pltpu.roll(b_g, shift, 0) + b_g, b_g)
            a_g = jnp.where(keep, a_g * pltpu.roll(a_g, shift, 0), a_g)
            k *= 2
        h_g = a_g * carry + b_g
        carry = h_g[0:1, :] if reverse else h_g[grp - 1:grp, :]
        h_groups[gi] = h_g
    h = jnp.concatenate([h_groups[gi] for gi in range(t_chunk // grp)], axis=0)
    carry_ref[0:1, :] = carry
    if reverse:
        g = g_ref[...].astype(F32)
        o_ref[...] = (jax.nn.gelu(g) * (hf_ref[...] + h)).astype(o_ref.dtype)
    else:
        o_ref[...] = h


def _lru_scan(z, l, direction, conv_w, conv_b, w_a, b_a, w_x, b_x, lam, fwd_out, dims):
    tt = z.shape[0]
    b, n, c, lw = dims["b"], dims["n"], dims["c"], dims["lru_width"]
    n_blocks, block_dim = w_a.shape[2], w_a.shape[3]
    t_chunk = _tile(c, 256, SUBLANE_BF16)
    halo = SUBLANE_BF16
    n_ctx_chunks, n_lat_chunks = c // t_chunk, n // t_chunk
    ctx_chunk0 = dims["bn"] // t_chunk
    col_x = dims["off_lru_x"] // lw
    col_g = dims["off_lru_g"] // lw
    reverse = direction == 1
    per_halo = t_chunk // halo
    n_halo_blocks = tt // halo

    def chunk_block(bi, j):
        is_ctx = j < n_ctx_chunks
        pos = jnp.where(is_ctx, j, j - n_ctx_chunks)
        if reverse:
            pos = jnp.where(is_ctx, n_ctx_chunks - 1 - pos, n_lat_chunks - 1 - pos)
        return jnp.where(is_ctx, ctx_chunk0 + bi * n_ctx_chunks + pos, bi * n_lat_chunks + pos)

    def prev_block(bi, j):
        return jnp.maximum(chunk_block(bi, j) * per_halo - 1, 0)

    def next_block(bi, j):
        return jnp.minimum((chunk_block(bi, j) + 1) * per_halo, n_halo_blocks - 1)

    vec = lambda: pl.BlockSpec((None, None, 1, lw), lambda bi, j: (l, direction, 0, 0))
    gate_w = lambda: pl.BlockSpec((None, None, n_blocks, block_dim, block_dim),
                                  lambda bi, j: (l, direction, 0, 0, 0))
    chunk_spec = lambda col: pl.BlockSpec((t_chunk, lw), lambda bi, j: (chunk_block(bi, j), col))
    gate_specs = [gate_w(), vec(), gate_w(), vec(), vec()]
    gate_args = [w_a, b_a, w_x, b_x, lam]
    if reverse:
        h_fwd, u = fwd_out
        in_specs = [chunk_spec(0)] + gate_specs + [chunk_spec(0), chunk_spec(col_g)]
        args = [u] + gate_args + [h_fwd, z]
        out_shape = jax.ShapeDtypeStruct((tt, lw), BF16)
        out_specs = chunk_spec(0)
    else:
        in_specs = [
            chunk_spec(col_x),
            pl.BlockSpec((halo, lw), lambda bi, j: (prev_block(bi, j), col_x)),
            pl.BlockSpec((halo, lw), lambda bi, j: (next_block(bi, j), col_x)),
            pl.BlockSpec((None, conv_w.shape[1], lw), lambda bi, j: (l, 0, 0)),
            pl.BlockSpec((None, 1, lw), lambda bi, j: (l, 0, 0)),
        ] + gate_specs
        args = [z, z, z, conv_w, conv_b] + gate_args
        out_shape = (jax.ShapeDtypeStruct((tt, lw), F32), jax.ShapeDtypeStruct((tt, lw), F32))
        out_specs = (chunk_spec(0), chunk_spec(0))
    kern = functools.partial(_lru_kernel, reverse=reverse, t_chunk=t_chunk, n_ctx_chunks=n_ctx_chunks,
                             n_lat_chunks=n_lat_chunks, n_blocks=n_blocks, block_dim=block_dim)
    return pl.pallas_call(
        kern,
        out_shape=out_shape,
        grid=(b, n_ctx_chunks + n_lat_chunks),
        in_specs=in_specs,
        out_specs=out_specs,
        scratch_shapes=[pltpu.VMEM((SUBLANE_F32, lw), F32)],
        compiler_params=_cparams(("parallel", "arbitrary"), 40),
        name="lru_bwd" if reverse else "lru_fwd",
    )(*args)


def _mla_proj_kernel(z_ref, qn_ref, kvn_ref, wq_ref, wkv_ref, ct_ref, st_ref, q_ref, k_ref, v_ref,
                     *, q_rank, kv_rank, heads, q_scale):
    z = z_ref[...]
    cq = z[:, :q_rank].astype(F32)
    ckv = z[:, q_rank:q_rank + kv_rank].astype(F32)
    kr_tile = z[:, q_rank + kv_rank:q_rank + kv_rank + LANE].astype(F32)
    ct = ct_ref[...]
    st = st_ref[...]

    def rope(tile):
        return tile * ct + pltpu.roll(tile, LANE // 2, 1) * st

    quarter = MLA_ROPE_DIM // 2
    lane = lax.broadcasted_iota(jnp.int32, kr_tile.shape, 1)
    swapped = jnp.where(lane < MLA_ROPE_DIM + quarter, pltpu.roll(kr_tile, quarter, 1),
                        pltpu.roll(kr_tile, LANE - quarter, 1))
    krp = jnp.where(lane < MLA_ROPE_DIM, kr_tile, swapped)

    qn = _rmsnorm_f32(cq, qn_ref[...]).astype(BF16)
    q = jnp.dot(qn, wq_ref[...], preferred_element_type=F32)
    kvn = _rmsnorm_f32(ckv, kvn_ref[...]).astype(BF16)
    kv = jnp.dot(kvn, wkv_ref[...], preferred_element_type=F32)
    kr = rope(krp).astype(BF16)
    for h in range(heads):
        c0 = h * MLA_HEAD_PAD
        q_ref[:, c0:c0 + LANE] = (q[:, c0:c0 + LANE] * q_scale).astype(BF16)
        q_ref[:, c0 + LANE:c0 + 2 * LANE] = (rope(q[:, c0 + LANE:c0 + 2 * LANE]) * q_scale).astype(BF16)
        k_ref[:, c0:c0 + LANE] = kv[:, h * LANE:(h + 1) * LANE].astype(BF16)
        k_ref[:, c0 + LANE:c0 + 2 * LANE] = kr
        v_ref[:, c0:c0 + LANE] = kv[:, (heads + h) * LANE:(heads + h + 1) * LANE].astype(BF16)
        v_ref[:, c0 + LANE:c0 + 2 * LANE] = jnp.ones((z.shape[0], LANE), BF16)


def _mla_proj(z, l, q_norm, kv_norm, wq, wkv, ct, st, dims):
    tt = z.shape[0]
    heads, q_rank, kv_rank = dims["mla_heads"], dims["q_rank"], dims["kv_rank"]
    mg = dims["mla_group"]
    tm = _tile(dims["tile_rows"], 512, SUBLANE_BF16)
    col = dims["off_mla"] // mg
    kern = functools.partial(_mla_proj_kernel, q_rank=q_rank, kv_rank=kv_rank, heads=heads,
                             q_scale=MLA_QK_DIM ** -0.5)
    return pl.pallas_call(
        kern,
        out_shape=(jax.ShapeDtypeStruct((tt, heads * MLA_HEAD_PAD), BF16),
                   jax.ShapeDtypeStruct((tt, heads * MLA_HEAD_PAD), BF16),
                   jax.ShapeDtypeStruct((tt, heads * MLA_HEAD_PAD), BF16)),
        grid=(tt // tm,),
        in_specs=[
            pl.BlockSpec((tm, mg), lambda i: (i, col)),
            pl.BlockSpec((None, 1, q_rank), lambda i: (l, 0, 0)),
            pl.BlockSpec((None, 1, kv_rank), lambda i: (l, 0, 0)),
            pl.BlockSpec((None, q_rank, heads * MLA_HEAD_PAD), lambda i: (l, 0, 0)),
            pl.BlockSpec((None, kv_rank, heads * 2 * LANE), lambda i: (l, 0, 0)),
            pl.BlockSpec((tm, LANE), lambda i: (i, 0)),
            pl.BlockSpec((tm, LANE), lambda i: (i, 0)),
        ],
        out_specs=(pl.BlockSpec((tm, heads * MLA_HEAD_PAD), lambda i: (i, 0)),
                   pl.BlockSpec((tm, heads * MLA_HEAD_PAD), lambda i: (i, 0)),
                   pl.BlockSpec((tm, heads * MLA_HEAD_PAD), lambda i: (i, 0))),
        compiler_params=_cparams(("parallel",), 48),
        name="mla_proj",
    )(z, q_norm, kv_norm, wq, wkv, ct, st)


def _mla_flash_kernel(q_ref, kt_ref, v_ref, ktc_ref, vc_ref, o_ref, *, sub, tk):
    n_chunks = kt_ref.shape[1] // tk
    chunks = [None] + list(range(n_chunks))

    def scores(q, ch):
        kt = ktc_ref[...] if ch is None else kt_ref[:, ch * tk:(ch + 1) * tk]
        return jnp.dot(q, kt, preferred_element_type=F32)

    def values(ch):
        return vc_ref[...] if ch is None else v_ref[ch * tk:(ch + 1) * tk, :]

    for a in range(q_ref.shape[0] // sub):
        rows = slice(a * sub, (a + 1) * sub)
        q = q_ref[rows, :]
        s_next = scores(q, chunks[0])
        m = acc = None
        for idx, ch in enumerate(chunks):
            s = s_next
            if idx + 1 < len(chunks):
                s_next = scores(q, chunks[idx + 1])
            m_new = jnp.max(s, axis=-1, keepdims=True)
            if m is not None:
                m_new = jnp.maximum(m, m_new)
            p = jnp.exp(s - m_new).astype(BF16)
            pv = jnp.dot(p, values(ch), preferred_element_type=F32)
            acc = pv if m is None else jnp.exp(m - m_new) * acc + pv
            m = m_new
        o_ref[rows, :] = (acc[:, :MLA_V_DIM] * (1.0 / acc[:, MLA_V_DIM:])).astype(o_ref.dtype)


def _mla_attention(q, kt, v, dims):
    tt = q.shape[0]
    b, n, c, heads = dims["b"], dims["n"], dims["c"], dims["mla_heads"]
    tq = _tile(n, 1024, SUBLANE_BF16)
    sub = _tile(tq, 512, SUBLANE_BF16)
    ctx_blk0 = dims["bn"] // c
    nq = n // tq
    vw = 2 * MLA_V_DIM
    return pl.pallas_call(
        functools.partial(_mla_flash_kernel, sub=sub, tk=_tile(n, 2048, LANE)),
        out_shape=jax.ShapeDtypeStruct((tt, heads * MLA_V_DIM), BF16),
        grid=(b, heads, nq),
        in_specs=[
            pl.BlockSpec((tq, MLA_HEAD_PAD), lambda bi, h, i: (bi * nq + i, h)),
            pl.BlockSpec((None, MLA_HEAD_PAD, n), lambda bi, h, i: (h, 0, bi)),
            pl.BlockSpec((n, vw), lambda bi, h, i: (bi, h)),
            pl.BlockSpec((None, MLA_HEAD_PAD, c), lambda bi, h, i: (h, 0, ctx_blk0 + bi)),
            pl.BlockSpec((c, vw), lambda bi, h, i: (ctx_blk0 + bi, h)),
        ],
        out_specs=pl.BlockSpec((tq, MLA_V_DIM), lambda bi, h, i: (bi * nq + i, h)),
        compiler_params=_cparams(("parallel", "parallel", "arbitrary"), 48),
        name="mla_attention",
    )(q, kt, v, kt, v)


def _merge_kernel(ba_ref, bb_ref, bc_ref, wb_ref, ga_ref, gb_ref, gc_ref, o_ref):
    y = None
    for br, gate, i in ((ba_ref, ga_ref, 0), (bb_ref, gb_ref, 1), (bc_ref, gc_ref, 2)):
        t = jnp.dot(br[...], wb_ref[i], preferred_element_type=F32)
        t = jax.nn.sigmoid(gate[...].astype(F32)) * t
        y = t if y is None else y + t
    o_ref[...] = y.astype(o_ref.dtype)


def _merge(br_a, br_b, br_c, w_branch, z, l, n_rows, dims):
    d = dims["d"]
    bw = br_a.shape[1]
    tm = _tile(dims["tile_rows"], 1024, SUBLANE_BF16)
    tn = _tile(d, 512, LANE)
    gate_col0 = dims["off_gate"] // tn
    per_gate = d // tn
    branch = lambda: pl.BlockSpec((tm, bw), lambda i, j: (i, 0))
    gate = lambda k: pl.BlockSpec((tm, tn), lambda i, j: (i, gate_col0 + k * per_gate + j))
    return pl.pallas_call(
        _merge_kernel,
        out_shape=jax.ShapeDtypeStruct((n_rows, d), BF16),
        grid=(n_rows // tm, d // tn),
        in_specs=[branch(), branch(), branch(),
                  pl.BlockSpec((None, 3, bw, tn), lambda i, j: (l, 0, 0, j)),
                  gate(0), gate(1), gate(2)],
        out_specs=pl.BlockSpec((tm, tn), lambda i, j: (i, j)),
        compiler_params=_cparams(("parallel", "arbitrary"), 48),
        name="merge",
    )(br_a, br_b, br_c, w_branch, z, z, z)


def _out_proj_kernel(y_ref, w_ref, x_ref, g_ref, o_ref):
    t = jnp.dot(y_ref[...], w_ref[...], preferred_element_type=F32)
    o_ref[...] = x_ref[...] + g_ref[0] * t


def _out_proj(y, w_out, xall, gate, l, dims):
    n_rows, d = y.shape
    tm = _tile(dims["tile_rows"], 1024, SUBLANE_BF16)
    tn = _tile(d, 512, LANE)
    grp = functools.partial(_group_of_tile, tm=tm, bn=dims["bn"], n=dims["n"])
    return pl.pallas_call(
        _out_proj_kernel,
        out_shape=jax.ShapeDtypeStruct((n_rows, d), F32),
        grid=(n_rows // tm, d // tn),
        in_specs=[
            pl.BlockSpec((tm, d), lambda i, j: (i, 0)),
            pl.BlockSpec((None, d, tn), lambda i, j: (l, 0, j)),
            pl.BlockSpec((tm, tn), lambda i, j: (i, j)),
            pl.BlockSpec((1, 1, tn), lambda i, j: (grp(i), 0, j)),
        ],
        out_specs=pl.BlockSpec((tm, tn), lambda i, j: (i, j)),
        compiler_params=_cparams(("parallel", "arbitrary"), 48),
        name="out_proj",
    )(y, w_out, xall, gate)


def _ffn_kernel(x_ref, xp_ref, xn_ref, gam_ref, sh_ref, sc_ref, g2_ref, wv_ref, wg_ref, cwv_ref, cwg_ref,
                cbv_ref, cbg_ref, wd_ref, nf_ref, *rest, tm, tiles_per_seq, final_norm):
    o_ref, h_ref = rest[-2:]
    i = pl.program_id(0)
    f = pl.program_id(1)
    halo = SUBLANE_F32

    def norm_mod(x):
        y = _rmsnorm_f32(x, gam_ref[...])
        return y * (1.0 + sc_ref[0]) + sh_ref[0]

    @pl.when(f == 0)
    def _():
        h_ref[0:tm, :] = norm_mod(x_ref[...]).astype(BF16)
        hh = norm_mod(jnp.concatenate([xn_ref[...], xp_ref[...]], axis=0))
        pos = i % tiles_per_seq
        keep_next = (pos != tiles_per_seq - 1).astype(F32)
        keep_prev = (pos != 0).astype(F32)
        is_next = lax.broadcasted_iota(jnp.int32, (2 * halo, 1), 0) < halo
        h_ref[tm:tm + 2 * halo, :] = (hh * jnp.where(is_next, keep_next, keep_prev)).astype(BF16)
        o_ref[...] = jnp.zeros_like(o_ref)

    hv = h_ref[...]
    rows = tm + 2 * halo

    def conv(w_ref, cw_ref, cb_ref):
        zz = jnp.dot(hv, w_ref[...], preferred_element_type=F32)
        cw = cw_ref[...]
        zm1 = pltpu.roll(zz, 1, 0)[0:tm]
        zp1 = pltpu.roll(zz, rows - 1, 0)[0:tm]
        return cw[0:1] * zm1 + cw[1:2] * zz[0:tm] + cw[2:3] * zp1 + cb_ref[...]

    val = conv(wv_ref, cwv_ref, cbv_ref)
    gate = conv(wg_ref, cwg_ref, cbg_ref)
    act = (gate * jax.nn.sigmoid(gate) * val).astype(BF16)
    o_ref[...] += jnp.dot(act, wd_ref[...], preferred_element_type=F32)

    @pl.when(f == pl.num_programs(1) - 1)
    def _():
        xo = x_ref[...] + g2_ref[0] * o_ref[...]
        if final_norm:
            xo = _rmsnorm_f32(xo, nf_ref[...])
        o_ref[...] = xo


def _ffn(xall, out_buf, gamma, shift, scale, gate2, w_up, conv_w, conv_b, w_down, norm_final, l, row0, n_rows,
         seq_len, out_rows, final_norm, dims):
    tt, d = xall.shape
    ff = w_down.shape[1]
    tm = _tile(seq_len, 1024, SUBLANE_BF16)
    tf = _tile(ff, 512, LANE)
    nf = ff // tf
    halo = SUBLANE_F32
    per_halo = tm // halo
    n_halo_blocks = tt // halo
    tile0 = row0 // tm
    assert row0 % tm == 0 and tm % halo == 0
    grp = lambda i: _group_of_tile(tile0 + i, tm=tm, bn=dims["bn"], n=dims["n"])
    mod = lambda: pl.BlockSpec((1, 1, d), lambda i, f: (grp(i), 0, 0))
    kern = functools.partial(_ffn_kernel, tm=tm, tiles_per_seq=seq_len // tm, final_norm=final_norm)
    args = [xall, xall, xall, gamma, shift, scale, gate2, w_up, w_up, conv_w, conv_w, conv_b, conv_b, w_down,
            norm_final]
    extra_specs, aliases = [], {}
    if out_buf is not None:
        aliases = {len(args): 0}
        args.append(out_buf)
        extra_specs = [pl.BlockSpec(memory_space=pl.ANY)]
    return pl.pallas_call(
        kern,
        out_shape=jax.ShapeDtypeStruct((out_rows, d), F32),
        grid=(n_rows // tm, nf),
        in_specs=[
            pl.BlockSpec((tm, d), lambda i, f: (tile0 + i, 0), pipeline_mode=pl.Buffered(1)),
            pl.BlockSpec((halo, d), lambda i, f: (jnp.maximum((tile0 + i) * per_halo - 1, 0), 0)),
            pl.BlockSpec((halo, d), lambda i, f: (jnp.minimum((tile0 + i + 1) * per_halo, n_halo_blocks - 1), 0)),
            pl.BlockSpec((None, 1, d), lambda i, f: (l, 0, 0)),
            mod(), mod(), mod(),
            pl.BlockSpec((None, d, tf), lambda i, f: (l, 0, f)),
            pl.BlockSpec((None, d, tf), lambda i, f: (l, 0, nf + f)),
            pl.BlockSpec((None, conv_w.shape[1], tf), lambda i, f: (l, 0, f)),
            pl.BlockSpec((None, conv_w.shape[1], tf), lambda i, f: (l, 0, nf + f)),
            pl.BlockSpec((None, 1, tf), lambda i, f: (l, 0, f)),
            pl.BlockSpec((None, 1, tf), lambda i, f: (l, 0, nf + f)),
            pl.BlockSpec((None, tf, d), lambda i, f: (l, f, 0)),
            pl.BlockSpec((1, d), lambda i, f: (0, 0)),
        ] + extra_specs,
        out_specs=pl.BlockSpec((tm, d), lambda i, f: (tile0 + i, 0)),
        scratch_shapes=[pltpu.VMEM((tm + 2 * halo, d), BF16)],
        input_output_aliases=aliases,
        compiler_params=_cparams(("parallel", "arbitrary"), 58),
        name="conv_ffn",
    )(*args)


def _rope_tables(n, b, c):
    t = jnp.arange(n, dtype=jnp.int32)
    row = (t // GRID_W).astype(F32)
    col = (t % GRID_W).astype(F32)
    n_freq = MLA_ROPE_DIM // 4
    inv_freq = ROPE_THETA ** (-jnp.arange(n_freq, dtype=F32) / n_freq)
    ang = jnp.concatenate([row[:, None] * inv_freq, col[:, None] * inv_freq], axis=-1)
    cos, sin = jnp.cos(ang), jnp.sin(ang)
    zeros = jnp.zeros((n, LANE - MLA_ROPE_DIM), F32)
    ct = jnp.concatenate([cos, cos, zeros], axis=-1)
    st = jnp.concatenate([-sin, sin, zeros], axis=-1)
    ct_ctx = jnp.concatenate([jnp.ones((b * c, MLA_ROPE_DIM), F32), jnp.zeros((b * c, LANE - MLA_ROPE_DIM), F32)], -1)
    ct = jnp.concatenate([jnp.tile(ct, (b, 1)), ct_ctx], axis=0)
    st = jnp.concatenate([jnp.tile(st, (b, 1)), jnp.zeros((b * c, LANE), F32)], axis=0)
    return ct, st


def _swap_halves(w):
    half = w.shape[-1] // 2
    return jnp.concatenate([w[..., half:], w[..., :half]], axis=-1)


def kernel(x, c, ctx, c_ctx, w_mod, b_mod, norm_mix, norm_ffn, w_in, na_rpb, lru_conv_w, lru_conv_b, lru_w_a,
           lru_b_a, lru_w_x, lru_b_x, lru_lam, mla_q_norm, mla_kv_norm, mla_w_q_up, mla_w_kv_up, w_branch, w_out,
           ffn_w_up, ffn_conv_w, ffn_conv_b, ffn_w_down, norm_final):
    b, n, d = x.shape
    ctx_len = ctx.shape[1]
    depth = w_mod.shape[0]
    na_heads = na_rpb.shape[1]
    na_width = na_heads * NA_HEAD_DIM
    lru_width = lru_conv_w.shape[2]
    q_rank = mla_q_norm.shape[1]
    kv_rank = mla_kv_norm.shape[1]
    mla_heads = mla_w_q_up.shape[2] // MLA_QK_DIM
    bn, bc = b * n, b * ctx_len
    assert na_width == lru_width == mla_heads * MLA_V_DIM == w_branch.shape[2]
    assert n % (NA_KROWS * GRID_W) == 0 and n % ctx_len == 0

    off_lru_x = 3 * na_width
    off_lru_g = off_lru_x + lru_width
    off_mla = off_lru_g + lru_width
    mla_used = q_rank + kv_rank + MLA_ROPE_DIM
    mla_group = -(-mla_used // LANE) * LANE
    while off_mla % mla_group:
        mla_group += LANE
    off_gate = off_mla + mla_group
    dims = dict(b=b, n=n, c=ctx_len, d=d, bn=bn, na_heads=na_heads, lru_width=lru_width, q_rank=q_rank,
                kv_rank=kv_rank, mla_heads=mla_heads, off_lru_x=off_lru_x, off_lru_g=off_lru_g, off_mla=off_mla,
                mla_group=mla_group, off_gate=off_gate, tile_rows=int(np.gcd(n, bc)))

    w_in_b = w_in.astype(BF16)
    w_gate_b = w_in_b[:, :, off_mla + mla_used:]
    wq = mla_w_q_up.reshape(depth, q_rank, mla_heads, MLA_QK_DIM)
    wq_rope = wq[..., MLA_NOPE_DIM:]
    wq_p = jnp.concatenate([wq[..., :MLA_NOPE_DIM], wq_rope, _swap_halves(wq_rope)], axis=-1)
    wq_p = wq_p.reshape(depth, q_rank, mla_heads * MLA_HEAD_PAD).astype(BF16)
    wkv = mla_w_kv_up.reshape(depth, kv_rank, mla_heads, MLA_NOPE_DIM + MLA_V_DIM)
    wkv_p = jnp.concatenate([wkv[..., :MLA_NOPE_DIM].reshape(depth, kv_rank, -1),
                             wkv[..., MLA_NOPE_DIM:].reshape(depth, kv_rank, -1)], axis=-1).astype(BF16)
    w_branch_b = w_branch.astype(BF16)
    w_out_b = w_out.astype(BF16)
    w_up_b = ffn_w_up.astype(BF16)
    w_down_b = ffn_w_down.astype(BF16)
    lru_w_a_b = lru_w_a.astype(BF16)
    lru_w_x_b = lru_w_x.astype(BF16)
    ct, st = _rope_tables(n, b, ctx_len)
    na_bias = _na_bias_tables(na_rpb, n // GRID_W)

    n_groups = 1 + b
    pad_rows = -n_groups % SUBLANE_F32
    cond = jnp.concatenate([c_ctx[None, :], c, jnp.zeros((pad_rows, d), F32)], axis=0)
    mod = _modulation(cond, w_mod, b_mod)[:, :n_groups].reshape(depth, n_groups, N_MOD, 1, d)

    r3 = lambda a: a.reshape(a.shape[0], 1, a.shape[1])
    r4 = lambda a: a.reshape(a.shape[0], a.shape[1], 1, a.shape[2])
    norm_mix3, norm_ffn3 = r3(norm_mix), r3(norm_ffn)
    lru_conv_b3 = r3(lru_conv_b)
    lru_b_a4, lru_b_x4, lru_lam4 = r4(lru_b_a), r4(lru_b_x), r4(lru_lam)
    q_norm3, kv_norm3 = r3(mla_q_norm), r3(mla_kv_norm)
    ffn_conv_b3 = r3(ffn_conv_b)
    norm_final2 = norm_final.reshape(1, d)

    xall = jnp.concatenate([x.reshape(bn, d), ctx.reshape(bc, d)], axis=0)
    for l in range(depth):
        last = l == depth - 1
        n_rows = bn if last else bn + bc
        sh1, sc1, g1, sh2, sc2, g2 = (mod[l, :, k] for k in range(N_MOD))
        z = _in_proj(xall, norm_mix3, sh1, sc1, w_in_b, w_gate_b, l, dims)
        out_a = _na_attention(z, na_bias, l, dims)
        lru_args = (lru_conv_w, lru_conv_b3, lru_w_a_b, lru_b_a4, lru_w_x_b, lru_b_x4, lru_lam4)
        lru_fwd = _lru_scan(z, l, 0, *lru_args, None, dims)
        out_b = _lru_scan(z, l, 1, *lru_args, lru_fwd, dims)
        q_m, k_m, v_m = _mla_proj(z, l, q_norm3, kv_norm3, wq_p, wkv_p, ct, st, dims)
        kt_m = k_m.reshape(bn + bc, mla_heads, MLA_HEAD_PAD).transpose(1, 2, 0)
        out_c = _mla_attention(q_m, kt_m, v_m, dims)
        if not last:
            out_a = _ctx_attention(z, z, z, out_a, dims, na_heads, NA_HEAD_DIM, 0, na_heads, 2 * na_heads, 1,
                                   NA_HEAD_DIM ** -0.5, "na_ctx_attention")
            out_c = _ctx_attention(q_m, k_m, v_m, out_c, dims, mla_heads, MLA_HEAD_PAD, 0, 0, 0, 2, 1.0,
                                   "mla_ctx_attention")
        y = _merge(out_a, out_b, out_c, w_branch_b, z, l, n_rows, dims)
        xall = _out_proj(y, w_out_b, xall, g1, l, dims)
        ffn_args = (norm_ffn3, sh2, sc2, g2, w_up_b, ffn_conv_w, ffn_conv_b3, w_down_b, norm_final2, l)
        x_new = _ffn(xall, None, *ffn_args, 0, bn, n, n_rows, last, dims)
        if not last:
            x_new = _ffn(xall, x_new, *ffn_args, bn, bc, ctx_len, n_rows, False, dims)
        xall = x_new
    return xall.reshape(b, n, d)
```

```python
import functools

import numpy as np
import jax
import jax.numpy as jnp
from jax import lax
from jax.experimental import pallas as pl
from jax.experimental.pallas import tpu as pltpu

F32 = jnp.float32
BF16 = jnp.bfloat16

GRID_W = 64
NORM_EPS = 1e-6
NEG_INF = -1e30
N_MOD = 6
NA_HEAD_DIM = 128
NA_WIN_H = 8
NA_WIN_W = 16
LRU_C = 8.0
MLA_NOPE_DIM = 128
MLA_ROPE_DIM = 64
MLA_V_DIM = 128
MLA_QK_DIM = MLA_NOPE_DIM + MLA_ROPE_DIM
ROPE_THETA = 10000.0

LANE = 128
SUBLANE_F32 = 8
SUBLANE_BF16 = 16
MLA_HEAD_PAD = 256
NA_QROWS = 8
NA_KROWS = 16
MIB = 1024 * 1024


def _cparams(semantics, vmem_mib):
    return pltpu.CompilerParams(dimension_semantics=semantics, vmem_limit_bytes=vmem_mib * MIB)


def _tile(n, target, mult):
    best = None
    for t in range(mult, min(n, target) + 1, mult):
        if n % t == 0:
            best = t
    assert best is not None, (n, target, mult)
    return best


def _group_of_tile(i, tm, bn, n):
    return jnp.where(i * tm >= bn, 0, 1 + (i * tm) // n)


def _rmsnorm_f32(x, gamma):
    ms = jnp.mean(x * x, axis=-1, keepdims=True)
    return x * lax.rsqrt(ms + NORM_EPS) * gamma


def _mod_kernel(c_ref, w_ref, b_ref, o_ref):
    cc = c_ref[...]
    s = cc * jax.nn.sigmoid(cc)
    o_ref[...] = jnp.dot(s, w_ref[...], preferred_element_type=F32,
                         precision=lax.Precision.HIGHEST) + b_ref[...]


def _modulation(cond, w_mod, b_mod):
    depth, d, nm = w_mod.shape
    rows = cond.shape[0]
    tn = _tile(nm, 1024, LANE)
    return pl.pallas_call(
        _mod_kernel,
        out_shape=jax.ShapeDtypeStruct((depth, rows, nm), F32),
        grid=(depth, nm // tn),
        in_specs=[
            pl.BlockSpec((rows, d), lambda l, j: (0, 0)),
            pl.BlockSpec((None, d, tn), lambda l, j: (l, 0, j)),
            pl.BlockSpec((None, 1, tn), lambda l, j: (l, 0, j)),
        ],
        out_specs=pl.BlockSpec((None, rows, tn), lambda l, j: (l, 0, j)),
        compiler_params=_cparams(("arbitrary", "arbitrary"), 40),
        name="adaln_mod",
    )(cond, w_mod, b_mod.reshape(depth, 1, nm))


def _in_proj_kernel(x_ref, gam_ref, sh_ref, sc_ref, w_ref, wg_ref, o_ref, h_ref, *, n_head_tiles):
    j = pl.program_id(1)

    @pl.when(j == 0)
    def _():
        y = _rmsnorm_f32(x_ref[...], gam_ref[...])
        h_ref[...] = (y * (1.0 + sc_ref[0]) + sh_ref[0]).astype(BF16)

    @pl.when(j < n_head_tiles)
    def _():
        o_ref[...] = jnp.dot(h_ref[...], w_ref[...], preferred_element_type=F32).astype(o_ref.dtype)

    @pl.when(j >= n_head_tiles)
    def _():
        o_ref[...] = jnp.dot(h_ref[...], wg_ref[...], preferred_element_type=F32).astype(o_ref.dtype)


def _in_proj(xall, gamma, shift, scale, w, w_gate, l, dims):
    tt, d = xall.shape
    n_head, n_gate = dims["off_gate"], w_gate.shape[2]
    tm = _tile(dims["tile_rows"], 1024, SUBLANE_BF16)
    tn = _tile(int(np.gcd(n_head, n_gate)), 1024, LANE)
    n_head_tiles = n_head // tn
    assert n_head <= w.shape[2]
    grp = functools.partial(_group_of_tile, tm=tm, bn=dims["bn"], n=dims["n"])
    return pl.pallas_call(
        functools.partial(_in_proj_kernel, n_head_tiles=n_head_tiles),
        out_shape=jax.ShapeDtypeStruct((tt, n_head + n_gate), BF16),
        grid=(tt // tm, (n_head + n_gate) // tn),
        in_specs=[
            pl.BlockSpec((tm, d), lambda i, j: (i, 0)),
            pl.BlockSpec((None, 1, d), lambda i, j: (l, 0, 0)),
            pl.BlockSpec((1, 1, d), lambda i, j: (grp(i), 0, 0)),
            pl.BlockSpec((1, 1, d), lambda i, j: (grp(i), 0, 0)),
            pl.BlockSpec((None, d, tn), lambda i, j: (l, 0, jnp.minimum(j, n_head_tiles - 1))),
            pl.BlockSpec((None, d, tn), lambda i, j: (l, 0, jnp.maximum(j - n_head_tiles, 0))),
        ],
        out_specs=pl.BlockSpec((tm, tn), lambda i, j: (i, j)),
        scratch_shapes=[pltpu.VMEM((tm, d), BF16)],
        compiler_params=_cparams(("parallel", "arbitrary"), 48),
        name="in_proj",
    )(xall, gamma, shift, scale, w, w_gate)


def _na_group_layout(g, grid_rows):
    kh = min(NA_WIN_H, grid_rows)
    k0 = int(np.clip(NA_QROWS * g - NA_WIN_H // 2, 0, grid_rows - NA_KROWS))
    layout = []
    for qi in range(NA_QROWS):
        r = NA_QROWS * g + qi
        first_key_row = int(np.clip(r - kh // 2, 0, grid_rows - kh))
        layout.append(tuple((k0 + kj) - r + NA_WIN_H - 1 if first_key_row <= k0 + kj < first_key_row + kh
                            else None for kj in range(NA_KROWS)))
    return k0, tuple(layout)


def _na_variants(grid_rows):
    layouts = [_na_group_layout(g, grid_rows)[1] for g in range(grid_rows // NA_QROWS)]
    distinct = sorted(set(layouts), key=layouts.index)
    return distinct, [distinct.index(lay) for lay in layouts]


def _na_kernel(q_ref, k_ref, v_ref, kc_ref, vc_ref, blk_ref, o_ref, bias_ref, *, grid_rows, scale):
    n_groups = grid_rows // NA_QROWS
    layouts, variant_of_group = _na_variants(grid_rows)
    tq = NA_QROWS * GRID_W
    nk = NA_KROWS * GRID_W
    nt = (((1,), (1,)), ((), ()))

    @pl.when(pl.program_id(1) == 0)
    def _():
        left_half = lax.broadcasted_iota(jnp.int32, (GRID_W, 2 * GRID_W), 1) < GRID_W
        masked = blk_ref.shape[0] - 1
        for var, layout in enumerate(layouts):
            for qi, row in enumerate(layout):
                for kp in range(NA_KROWS // 2):
                    r0, r1 = (masked if r is None else r for r in row[2 * kp:2 * kp + 2])
                    pair = blk_ref[r0] if r0 == r1 else jnp.where(left_half, blk_ref[r0], blk_ref[r1])
                    bias_ref[var, qi * GRID_W:(qi + 1) * GRID_W, 2 * kp * GRID_W:2 * (kp + 1) * GRID_W] = pair

    def key_start(g):
        return _na_group_layout(g, grid_rows)[0] * GRID_W

    def scores(g):
        q = q_ref[g * tq:(g + 1) * tq, :]
        k0 = key_start(g)
        var = variant_of_group[g]
        s = lax.dot_general(q, k_ref[k0:k0 + nk, :], nt, preferred_element_type=F32) * scale + bias_ref[var]
        sc = lax.dot_general(q, kc_ref[...], nt, preferred_element_type=F32) * scale
        return s, sc

    nxt = scores(0)
    for g in range(n_groups):
        s, sc = nxt
        if g + 1 < n_groups:
            nxt = scores(g + 1)
        k0 = key_start(g)
        m = jnp.maximum(jnp.max(s, axis=-1, keepdims=True), jnp.max(sc, axis=-1, keepdims=True))
        p = jnp.exp(s - m)
        pc = jnp.exp(sc - m)
        denom = jnp.sum(p, axis=-1, keepdims=True) + jnp.sum(pc, axis=-1, keepdims=True)
        o = jnp.dot(p.astype(BF16), v_ref[k0:k0 + nk, :], preferred_element_type=F32)
        o = o + jnp.dot(pc.astype(BF16), vc_ref[...], preferred_element_type=F32)
        o_ref[g * tq:(g + 1) * tq, :] = (o * (1.0 / denom)).astype(o_ref.dtype)


def _na_bias_blocks(rpb):
    lead = rpb.shape[:2]
    cidx = np.arange(GRID_W)
    c_start = np.clip(cidx - NA_WIN_W // 2, 0, GRID_W - NA_WIN_W)
    in_win = (cidx[None, :] >= c_start[:, None]) & (cidx[None, :] < c_start[:, None] + NA_WIN_W)
    col_idx = np.clip(cidx[None, :] - cidx[:, None], -(NA_WIN_W - 1), NA_WIN_W - 1) + (NA_WIN_W - 1)
    n_rel = 2 * NA_WIN_W - 1
    expand = (np.arange(n_rel)[:, None, None] == col_idx[None]) & in_win[None]
    expand = jnp.asarray(expand.reshape(n_rel, GRID_W * GRID_W), F32)
    blocks = jnp.einsum("lhrd,dx->lhrx", rpb.astype(F32), expand, precision=lax.Precision.HIGHEST)
    blocks = blocks + jnp.asarray(np.where(in_win, 0.0, NEG_INF).reshape(-1), F32)
    blocks = blocks.reshape(*lead, 2 * NA_WIN_H - 1, GRID_W, GRID_W)
    blocks = jnp.concatenate([blocks, jnp.full((*lead, 1, GRID_W, GRID_W), NEG_INF, F32)], axis=2)
    return jnp.concatenate([blocks, blocks], axis=-1)


def _na_attention(z, bias_blocks, l, dims):
    tt = z.shape[0]
    b, n, c = dims["b"], dims["n"], dims["c"]
    heads = dims["na_heads"]
    grid_rows = n // GRID_W
    n_var = len(_na_variants(grid_rows)[0])
    ctx_blk0 = dims["bn"] // c
    kern = functools.partial(_na_kernel, grid_rows=grid_rows, scale=NA_HEAD_DIM ** -0.5)
    return pl.pallas_call(
        kern,
        out_shape=jax.ShapeDtypeStruct((tt, heads * NA_HEAD_DIM), BF16),
        grid=(heads, b),
        in_specs=[
            pl.BlockSpec((n, NA_HEAD_DIM), lambda h, bi: (bi, h)),
            pl.BlockSpec((n, NA_HEAD_DIM), lambda h, bi: (bi, heads + h)),
            pl.BlockSpec((n, NA_HEAD_DIM), lambda h, bi: (bi, 2 * heads + h)),
            pl.BlockSpec((c, NA_HEAD_DIM), lambda h, bi: (ctx_blk0 + bi, heads + h)),
            pl.BlockSpec((c, NA_HEAD_DIM), lambda h, bi: (ctx_blk0 + bi, 2 * heads + h)),
            pl.BlockSpec((None, None) + bias_blocks.shape[2:], lambda h, bi: (l, h, 0, 0, 0)),
        ],
        out_specs=pl.BlockSpec((n, NA_HEAD_DIM), lambda h, bi: (bi, h)),
        scratch_shapes=[pltpu.VMEM((n_var, NA_QROWS * GRID_W, NA_KROWS * GRID_W), F32)],
        compiler_params=_cparams(("arbitrary", "arbitrary"), 48),
        name="na_attention",
    )(z, z, z, z, z, bias_blocks)


def _ctx_attn_kernel(q_ref, k_ref, v_ref, prev_ref, o_ref, *, scale):
    del prev_ref
    nt = (((1,), (1,)), ((), ()))
    s = lax.dot_general(q_ref[...], k_ref[...], nt, preferred_element_type=F32) * scale
    m = jnp.max(s, axis=-1, keepdims=True)
    p = jnp.exp(s - m)
    denom = jnp.sum(p, axis=-1, keepdims=True)
    o = jnp.dot(p.astype(BF16), v_ref[...], preferred_element_type=F32)
    o_ref[...] = (o * (1.0 / denom)).astype(o_ref.dtype)


def _ctx_attention(q_arr, k_arr, v_arr, out_buf, dims, heads, dq, q_col0, k_col0, v_col0, v_step, scale, name):
    b, c = dims["b"], dims["c"]
    blk0 = dims["bn"] // c
    dv = out_buf.shape[1] // heads
    return pl.pallas_call(
        functools.partial(_ctx_attn_kernel, scale=scale),
        out_shape=jax.ShapeDtypeStruct(out_buf.shape, out_buf.dtype),
        grid=(b, heads),
        in_specs=[
            pl.BlockSpec((c, dq), lambda bi, h: (blk0 + bi, q_col0 + h)),
            pl.BlockSpec((c, dq), lambda bi, h: (blk0 + bi, k_col0 + h)),
            pl.BlockSpec((c, dv), lambda bi, h: (blk0 + bi, v_col0 + v_step * h)),
            pl.BlockSpec(memory_space=pl.ANY),
        ],
        out_specs=pl.BlockSpec((c, dv), lambda bi, h: (blk0 + bi, h)),
        input_output_aliases={3: 0},
        compiler_params=_cparams(("parallel", "parallel"), 32),
        name=name,
    )(q_arr, k_arr, v_arr, out_buf)


def _lru_kernel(*refs, reverse, t_chunk, n_ctx_chunks, n_lat_chunks, n_blocks, block_dim):
    if reverse:
        u_ref, wa_ref, ba_ref, wx_ref, bx_ref, lam_ref, hf_ref, g_ref, o_ref, carry_ref = refs
    else:
        (x_ref, xp_ref, xn_ref, cw_ref, cb_ref, wa_ref, ba_ref, wx_ref, bx_ref, lam_ref,
         o_ref, u_ref, carry_ref) = refs
    j = pl.program_id(1)

    @pl.when(j == 0)
    def _():
        carry_ref[...] = jnp.zeros_like(carry_ref)

    if reverse:
        u = u_ref[...]
    else:
        is_ctx = j < n_ctx_chunks
        chunk = jnp.where(is_ctx, j, j - n_ctx_chunks)
        first = chunk == 0
        last = chunk == jnp.where(is_ctx, n_ctx_chunks, n_lat_chunks) - 1
        xm = x_ref[...].astype(F32)
        xp = xp_ref[...].astype(F32)
        xn = xn_ref[...].astype(F32)
        hp = xp.shape[0]
        pm1 = jnp.where(first, 0.0, xp[hp - 1:hp, :])
        pm2 = jnp.where(first, 0.0, xp[hp - 2:hp - 1, :])
        nn0 = jnp.where(last, 0.0, xn[0:1, :])
        t = lax.broadcasted_iota(jnp.int32, (t_chunk, 1), 0)
        xm1 = jnp.where(t == 0, pm1, pltpu.roll(xm, 1, 0))
        xm2 = jnp.where(t == 0, pm2, jnp.where(t == 1, pm1, pltpu.roll(xm, 2, 0)))
        xp1 = jnp.where(t == t_chunk - 1, nn0, pltpu.roll(xm, t_chunk - 1, 0))
        cw = cw_ref[...]
        u = cw[0:1] * xm2 + cw[1:2] * xm1 + cw[2:3] * xm + cw[3:4] * xp1 + cb_ref[...]
        u_ref[...] = u

    ub = u.astype(BF16)
    ga, gx = [], []
    for kb in range(n_blocks):
        blk = ub[:, kb * block_dim:(kb + 1) * block_dim]
        ga.append(jnp.dot(blk, wa_ref[kb], preferred_element_type=F32))
        gx.append(jnp.dot(blk, wx_ref[kb], preferred_element_type=F32))
    gate_a = jnp.concatenate(ga, axis=1) + ba_ref[...]
    gate_x = jnp.concatenate(gx, axis=1) + bx_ref[...]
    r = jax.nn.sigmoid(gate_a)
    i_gate = jax.nn.sigmoid(gate_x)
    nlam = -lam_ref[...]
    softplus = jnp.maximum(nlam, 0.0) + jnp.log1p(jnp.exp(-jnp.abs(nlam)))
    log_a = -LRU_C * r * softplus
    a = jnp.exp(log_a)
    bv = jnp.sqrt(-jnp.tanh(log_a) * (a * a + 1.0)) * (i_gate * u)

    grp = SUBLANE_F32
    sub = lax.broadcasted_iota(jnp.int32, (grp, 1), 0)
    order = range(t_chunk // grp - 1, -1, -1) if reverse else range(t_chunk // grp)
    carry = carry_ref[0:1, :]
    h_groups = {}
    for gi in order:
        a_g = a[gi * grp:(gi + 1) * grp, :]
        b_g = bv[gi * grp:(gi + 1) * grp, :]
        k = 1
        while k < grp:
            keep = (sub < grp - k) if reverse else (sub >= k)
            shift = grp - k if reverse else k
            b_g = jnp.where(keep, a_g * pltpu.roll(b_g, shift, 0) + b_g, b_g)
            a_g = jnp.where(keep, a_g * pltpu.roll(a_g, shift, 0), a_g)
            k *= 2
        h_g = a_g * carry + b_g
        carry = h_g[0:1, :] if reverse else h_g[grp - 1:grp, :]
        h_groups[gi] = h_g
    h = jnp.concatenate([h_groups[gi] for gi in range(t_chunk // grp)], axis=0)
    carry_ref[0:1, :] = carry
    if reverse:
        g = g_ref[...].astype(F32)
        o_ref[...] = (jax.nn.gelu(g) * (hf_ref[...] + h)).astype(o_ref.dtype)
    else:
        o_ref[...] = h


def _lru_scan(z, l, direction, conv_w, conv_b, w_a, b_a, w_x, b_x, lam, fwd_out, dims):
    tt = z.shape[0]
    b, n, c, lw = dims["b"], dims["n"], dims["c"], dims["lru_width"]
    n_blocks, block_dim = w_a.shape[2], w_a.shape[3]
    t_chunk = _tile(c, 256, SUBLANE_BF16)
    halo = SUBLANE_BF16
    n_ctx_chunks, n_lat_chunks = c // t_chunk, n // t_chunk
    ctx_chunk0 = dims["bn"] // t_chunk
    col_x = dims["off_lru_x"] // lw
    col_g = dims["off_lru_g"] // lw
    reverse = direction == 1
    per_halo = t_chunk // halo
    n_halo_blocks = tt // halo

    def chunk_block(bi, j):
        is_ctx = j < n_ctx_chunks
        pos = jnp.where(is_ctx, j, j - n_ctx_chunks)
        if reverse:
            pos = jnp.where(is_ctx, n_ctx_chunks - 1 - pos, n_lat_chunks - 1 - pos)
        return jnp.where(is_ctx, ctx_chunk0 + bi * n_ctx_chunks + pos, bi * n_lat_chunks + pos)

    def prev_block(bi, j):
        return jnp.maximum(chunk_block(bi, j) * per_halo - 1, 0)

    def next_block(bi, j):
        return jnp.minimum((chunk_block(bi, j) + 1) * per_halo, n_halo_blocks - 1)

    vec = lambda: pl.BlockSpec((None, None, 1, lw), lambda bi, j: (l, direction, 0, 0))
    gate_w = lambda: pl.BlockSpec((None, None, n_blocks, block_dim, block_dim),
                                  lambda bi, j: (l, direction, 0, 0, 0))
    chunk_spec = lambda col: pl.BlockSpec((t_chunk, lw), lambda bi, j: (chunk_block(bi, j), col))
    gate_specs = [gate_w(), vec(), gate_w(), vec(), vec()]
    gate_args = [w_a, b_a, w_x, b_x, lam]
    if reverse:
        h_fwd, u = fwd_out
        in_specs = [chunk_spec(0)] + gate_specs + [chunk_spec(0), chunk_spec(col_g)]
        args = [u] + gate_args + [h_fwd, z]
        out_shape = jax.ShapeDtypeStruct((tt, lw), BF16)
        out_specs = chunk_spec(0)
    else:
        in_specs = [
            chunk_spec(col_x),
            pl.BlockSpec((halo, lw), lambda bi, j: (prev_block(bi, j), col_x)),
            pl.BlockSpec((halo, lw), lambda bi, j: (next_block(bi, j), col_x)),
            pl.BlockSpec((None, conv_w.shape[1], lw), lambda bi, j: (l, 0, 0)),
            pl.BlockSpec((None, 1, lw), lambda bi, j: (l, 0, 0)),
        ] + gate_specs
        args = [z, z, z, conv_w, conv_b] + gate_args
        out_shape = (jax.ShapeDtypeStruct((tt, lw), F32), jax.ShapeDtypeStruct((tt, lw), F32))
        out_specs = (chunk_spec(0), chunk_spec(0))
    kern = functools.partial(_lru_kernel, reverse=reverse, t_chunk=t_chunk, n_ctx_chunks=n_ctx_chunks,
                             n_lat_chunks=n_lat_chunks, n_blocks=n_blocks, block_dim=block_dim)
    return pl.pallas_call(
        kern,
        out_shape=out_shape,
        grid=(b, n_ctx_chunks + n_lat_chunks),
        in_specs=in_specs,
        out_specs=out_specs,
        scratch_shapes=[pltpu.VMEM((SUBLANE_F32, lw), F32)],
        compiler_params=_cparams(("parallel", "arbitrary"), 40),
        name="lru_bwd" if reverse else "lru_fwd",
    )(*args)


def _mla_proj_kernel(z_ref, qn_ref, kvn_ref, wq_ref, wkv_ref, ct_ref, st_ref, q_ref, k_ref, v_ref,
                     *, q_rank, kv_rank, heads, q_scale):
    z = z_ref[...]
    cq = z[:, :q_rank].astype(F32)
    ckv = z[:, q_rank:q_rank + kv_rank].astype(F32)
    kr_tile = z[:, q_rank + kv_rank:q_rank + kv_rank + LANE].astype(F32)
    ct = ct_ref[...]
    st = st_ref[...]

    def rope(tile):
        return tile * ct + pltpu.roll(tile, LANE // 2, 1) * st

    quarter = MLA_ROPE_DIM // 2
    lane = lax.broadcasted_iota(jnp.int32, kr_tile.shape, 1)
    swapped = jnp.where(lane < MLA_ROPE_DIM + quarter, pltpu.roll(kr_tile, quarter, 1),
                        pltpu.roll(kr_tile, LANE - quarter, 1))
    krp = jnp.where(lane < MLA_ROPE_DIM, kr_tile, swapped)

    qn = _rmsnorm_f32(cq, qn_ref[...]).astype(BF16)
    q = jnp.dot(qn, wq_ref[...], preferred_element_type=F32)
    kvn = _rmsnorm_f32(ckv, kvn_ref[...]).astype(BF16)
    kv = jnp.dot(kvn, wkv_ref[...], preferred_element_type=F32)
    kr = rope(krp).astype(BF16)
    for h in range(heads):
        c0 = h * MLA_HEAD_PAD
        q_ref[:, c0:c0 + LANE] = (q[:, c0:c0 + LANE] * q_scale).astype(BF16)
        q_ref[:, c0 + LANE:c0 + 2 * LANE] = (rope(q[:, c0 + LANE:c0 + 2 * LANE]) * q_scale).astype(BF16)
        k_ref[:, c0:c0 + LANE] = kv[:, h * LANE:(h + 1) * LANE].astype(BF16)
        k_ref[:, c0 + LANE:c0 + 2 * LANE] = kr
        v_ref[:, c0:c0 + LANE] = kv[:, (heads + h) * LANE:(heads + h + 1) * LANE].astype(BF16)
        v_ref[:, c0 + LANE:c0 + 2 * LANE] = jnp.ones((z.shape[0], LANE), BF16)


def _mla_proj(z, l, q_norm, kv_norm, wq, wkv, ct, st, dims):
    tt = z.shape[0]
    heads, q_rank, kv_rank = dims["mla_heads"], dims["q_rank"], dims["kv_rank"]
    mg = dims["mla_group"]
    tm = _tile(dims["tile_rows"], 512, SUBLANE_BF16)
    col = dims["off_mla"] // mg
    kern = functools.partial(_mla_proj_kernel, q_rank=q_rank, kv_rank=kv_rank, heads=heads,
                             q_scale=MLA_QK_DIM ** -0.5)
    return pl.pallas_call(
        kern,
        out_shape=(jax.ShapeDtypeStruct((tt, heads * MLA_HEAD_PAD), BF16),
                   jax.ShapeDtypeStruct((tt, heads * MLA_HEAD_PAD), BF16),
                   jax.ShapeDtypeStruct((tt, heads * MLA_HEAD_PAD), BF16)),
        grid=(tt // tm,),
        in_specs=[
            pl.BlockSpec((tm, mg), lambda i: (i, col)),
            pl.BlockSpec((None, 1, q_rank), lambda i: (l, 0, 0)),
            pl.BlockSpec((None, 1, kv_rank), lambda i: (l, 0, 0)),
            pl.BlockSpec((None, q_rank, heads * MLA_HEAD_PAD), lambda i: (l, 0, 0)),
            pl.BlockSpec((None, kv_rank, heads * 2 * LANE), lambda i: (l, 0, 0)),
            pl.BlockSpec((tm, LANE), lambda i: (i, 0)),
            pl.BlockSpec((tm, LANE), lambda i: (i, 0)),
        ],
        out_specs=(pl.BlockSpec((tm, heads * MLA_HEAD_PAD), lambda i: (i, 0)),
                   pl.BlockSpec((tm, heads * MLA_HEAD_PAD), lambda i: (i, 0)),
                   pl.BlockSpec((tm, heads * MLA_HEAD_PAD), lambda i: (i, 0))),
        compiler_params=_cparams(("parallel",), 48),
        name="mla_proj",
    )(z, q_norm, kv_norm, wq, wkv, ct, st)


def _mla_flash_kernel(q_ref, kt_ref, v_ref, ktc_ref, vc_ref, o_ref, *, sub, tk):
    n_chunks = kt_ref.shape[1] // tk
    chunks = [None] + list(range(n_chunks))

    def scores(q, ch):
        kt = ktc_ref[...] if ch is None else kt_ref[:, ch * tk:(ch + 1) * tk]
        return jnp.dot(q, kt, preferred_element_type=F32)

    def values(ch):
        return vc_ref[...] if ch is None else v_ref[ch * tk:(ch + 1) * tk, :]

    for a in range(q_ref.shape[0] // sub):
        rows = slice(a * sub, (a + 1) * sub)
        q = q_ref[rows, :]
        s_next = scores(q, chunks[0])
        m = acc = None
        for idx, ch in enumerate(chunks):
            s = s_next
            if idx + 1 < len(chunks):
                s_next = scores(q, chunks[idx + 1])
            m_new = jnp.max(s, axis=-1, keepdims=True)
            if m is not None:
                m_new = jnp.maximum(m, m_new)
            p = jnp.exp(s - m_new).astype(BF16)
            pv = jnp.dot(p, values(ch), preferred_element_type=F32)
            acc = pv if m is None else jnp.exp(m - m_new) * acc + pv
            m = m_new
        o_ref[rows, :] = (acc[:, :MLA_V_DIM] * (1.0 / acc[:, MLA_V_DIM:])).astype(o_ref.dtype)


def _mla_attention(q, kt, v, dims):
    tt = q.shape[0]
    b, n, c, heads = dims["b"], dims["n"], dims["c"], dims["mla_heads"]
    tq = _tile(n, 1024, SUBLANE_BF16)
    sub = _tile(tq, 512, SUBLANE_BF16)
    ctx_blk0 = dims["bn"] // c
    nq = n // tq
    vw = 2 * MLA_V_DIM
    return pl.pallas_call(
        functools.partial(_mla_flash_kernel, sub=sub, tk=_tile(n, 2048, LANE)),
        out_shape=jax.ShapeDtypeStruct((tt, heads * MLA_V_DIM), BF16),
        grid=(b, heads, nq),
        in_specs=[
            pl.BlockSpec((tq, MLA_HEAD_PAD), lambda bi, h, i: (bi * nq + i, h)),
            pl.BlockSpec((None, MLA_HEAD_PAD, n), lambda bi, h, i: (h, 0, bi)),
            pl.BlockSpec((n, vw), lambda bi, h, i: (bi, h)),
            pl.BlockSpec((None, MLA_HEAD_PAD, c), lambda bi, h, i: (h, 0, ctx_blk0 + bi)),
            pl.BlockSpec((c, vw), lambda bi, h, i: (ctx_blk0 + bi, h)),
        ],
        out_specs=pl.BlockSpec((tq, MLA_V_DIM), lambda bi, h, i: (bi * nq + i, h)),
        compiler_params=_cparams(("parallel", "parallel", "arbitrary"), 48),
        name="mla_attention",
    )(q, kt, v, kt, v)


def _merge_kernel(ba_ref, bb_ref, bc_ref, wb_ref, ga_ref, gb_ref, gc_ref, o_ref):
    y = None
    for br, gate, i in ((ba_ref, ga_ref, 0), (bb_ref, gb_ref, 1), (bc_ref, gc_ref, 2)):
        t = jnp.dot(br[...], wb_ref[i], preferred_element_type=F32)
        t = jax.nn.sigmoid(gate[...].astype(F32)) * t
        y = t if y is None else y + t
    o_ref[...] = y.astype(o_ref.dtype)


def _merge(br_a, br_b, br_c, w_branch, z, l, n_rows, dims):
    d = dims["d"]
    bw = br_a.shape[1]
    tm = _tile(dims["tile_rows"], 1024, SUBLANE_BF16)
    tn = _tile(d, 512, LANE)
    gate_col0 = dims["off_gate"] // tn
    per_gate = d // tn
    branch = lambda: pl.BlockSpec((tm, bw), lambda i, j: (i, 0))
    gate = lambda k: pl.BlockSpec((tm, tn), lambda i, j: (i, gate_col0 + k * per_gate + j))
    return pl.pallas_call(
        _merge_kernel,
        out_shape=jax.ShapeDtypeStruct((n_rows, d), BF16),
        grid=(n_rows // tm, d // tn),
        in_specs=[branch(), branch(), branch(),
                  pl.BlockSpec((None, 3, bw, tn), lambda i, j: (l, 0, 0, j)),
                  gate(0), gate(1), gate(2)],
        out_specs=pl.BlockSpec((tm, tn), lambda i, j: (i, j)),
        compiler_params=_cparams(("parallel", "arbitrary"), 48),
        name="merge",
    )(br_a, br_b, br_c, w_branch, z, z, z)


def _out_proj_kernel(y_ref, w_ref, x_ref, g_ref, o_ref):
    t = jnp.dot(y_ref[...], w_ref[...], preferred_element_type=F32)
    o_ref[...] = x_ref[...] + g_ref[0] * t


def _out_proj(y, w_out, xall, gate, l, dims):
    n_rows, d = y.shape
    tm = _tile(dims["tile_rows"], 1024, SUBLANE_BF16)
    tn = _tile(d, 512, LANE)
    grp = functools.partial(_group_of_tile, tm=tm, bn=dims["bn"], n=dims["n"])
    return pl.pallas_call(
        _out_proj_kernel,
        out_shape=jax.ShapeDtypeStruct((n_rows, d), F32),
        grid=(n_rows // tm, d // tn),
        in_specs=[
            pl.BlockSpec((tm, d), lambda i, j: (i, 0)),
            pl.BlockSpec((None, d, tn), lambda i, j: (l, 0, j)),
            pl.BlockSpec((tm, tn), lambda i, j: (i, j)),
            pl.BlockSpec((1, 1, tn), lambda i, j: (grp(i), 0, j)),
        ],
        out_specs=pl.BlockSpec((tm, tn), lambda i, j: (i, j)),
        compiler_params=_cparams(("parallel", "arbitrary"), 48),
        name="out_proj",
    )(y, w_out, xall, gate)


def _ffn_kernel(x_ref, xp_ref, xn_ref, gam_ref, sh_ref, sc_ref, g2_ref, wv_ref, wg_ref, cwv_ref, cwg_ref,
                cbv_ref, cbg_ref, wd_ref, nf_ref, *rest, tm, tiles_per_seq, final_norm):
    o_ref, h_ref = rest[-2:]
    i = pl.program_id(0)
    f = pl.program_id(1)
    halo = SUBLANE_F32

    def norm_mod(x):
        y = _rmsnorm_f32(x, gam_ref[...])
        return y * (1.0 + sc_ref[0]) + sh_ref[0]

    @pl.when(f == 0)
    def _():
        h_ref[0:tm, :] = norm_mod(x_ref[...]).astype(BF16)
        hh = norm_mod(jnp.concatenate([xn_ref[...], xp_ref[...]], axis=0))
        pos = i % tiles_per_seq
        keep_next = (pos != tiles_per_seq - 1).astype(F32)
        keep_prev = (pos != 0).astype(F32)
        is_next = lax.broadcasted_iota(jnp.int32, (2 * halo, 1), 0) < halo
        h_ref[tm:tm + 2 * halo, :] = (hh * jnp.where(is_next, keep_next, keep_prev)).astype(BF16)
        o_ref[...] = jnp.zeros_like(o_ref)

    hv = h_ref[...]
    rows = tm + 2 * halo

    def conv(w_ref, cw_ref, cb_ref):
        zz = jnp.dot(hv, w_ref[...], preferred_element_type=F32)
        cw = cw_ref[...]
        zm1 = pltpu.roll(zz, 1, 0)[0:tm]
        zp1 = pltpu.roll(zz, rows - 1, 0)[0:tm]
        return cw[0:1] * zm1 + cw[1:2] * zz[0:tm] + cw[2:3] * zp1 + cb_ref[...]

    val = conv(wv_ref, cwv_ref, cbv_ref)
    gate = conv(wg_ref, cwg_ref, cbg_ref)
    act = (gate * jax.nn.sigmoid(gate) * val).astype(BF16)
    o_ref[...] += jnp.dot(act, wd_ref[...], preferred_element_type=F32)

    @pl.when(f == pl.num_programs(1) - 1)
    def _():
        xo = x_ref[...] + g2_ref[0] * o_ref[...]
        if final_norm:
            xo = _rmsnorm_f32(xo, nf_ref[...])
        o_ref[...] = xo


def _ffn(xall, out_buf, gamma, shift, scale, gate2, w_up, conv_w, conv_b, w_down, norm_final, l, row0, n_rows,
         seq_len, out_rows, final_norm, dims):
    tt, d = xall.shape
    ff = w_down.shape[1]
    tm = _tile(seq_len, 1024, SUBLANE_BF16)
    tf = _tile(ff, 512, LANE)
    nf = ff // tf
    halo = SUBLANE_F32
    per_halo = tm // halo
    n_halo_blocks = tt // halo
    tile0 = row0 // tm
    assert row0 % tm == 0 and tm % halo == 0
    grp = lambda i: _group_of_tile(tile0 + i, tm=tm, bn=dims["bn"], n=dims["n"])
    mod = lambda: pl.BlockSpec((1, 1, d), lambda i, f: (grp(i), 0, 0))
    kern = functools.partial(_ffn_kernel, tm=tm, tiles_per_seq=seq_len // tm, final_norm=final_norm)
    args = [xall, xall, xall, gamma, shift, scale, gate2, w_up, w_up, conv_w, conv_w, conv_b, conv_b, w_down,
            norm_final]
    extra_specs, aliases = [], {}
    if out_buf is not None:
        aliases = {len(args): 0}
        args.append(out_buf)
        extra_specs = [pl.BlockSpec(memory_space=pl.ANY)]
    return pl.pallas_call(
        kern,
        out_shape=jax.ShapeDtypeStruct((out_rows, d), F32),
        grid=(n_rows // tm, nf),
        in_specs=[
            pl.BlockSpec((tm, d), lambda i, f: (tile0 + i, 0), pipeline_mode=pl.Buffered(1)),
            pl.BlockSpec((halo, d), lambda i, f: (jnp.maximum((tile0 + i) * per_halo - 1, 0), 0)),
            pl.BlockSpec((halo, d), lambda i, f: (jnp.minimum((tile0 + i + 1) * per_halo, n_halo_blocks - 1), 0)),
            pl.BlockSpec((None, 1, d), lambda i, f: (l, 0, 0)),
            mod(), mod(), mod(),
            pl.BlockSpec((None, d, tf), lambda i, f: (l, 0, f)),
            pl.BlockSpec((None, d, tf), lambda i, f: (l, 0, nf + f)),
            pl.BlockSpec((None, conv_w.shape[1], tf), lambda i, f: (l, 0, f)),
            pl.BlockSpec((None, conv_w.shape[1], tf), lambda i, f: (l, 0, nf + f)),
            pl.BlockSpec((None, 1, tf), lambda i, f: (l, 0, f)),
            pl.BlockSpec((None, 1, tf), lambda i, f: (l, 0, nf + f)),
            pl.BlockSpec((None, tf, d), lambda i, f: (l, f, 0)),
            pl.BlockSpec((1, d), lambda i, f: (0, 0)),
        ] + extra_specs,
        out_specs=pl.BlockSpec((tm, d), lambda i, f: (tile0 + i, 0)),
        scratch_shapes=[pltpu.VMEM((tm + 2 * halo, d), BF16)],
        input_output_aliases=aliases,
        compiler_params=_cparams(("parallel", "arbitrary"), 58),
        name="conv_ffn",
    )(*args)


def _rope_tables(n, b, c):
    t = jnp.arange(n, dtype=jnp.int32)
    row = (t // GRID_W).astype(F32)
    col = (t % GRID_W).astype(F32)
    n_freq = MLA_ROPE_DIM // 4
    inv_freq = ROPE_THETA ** (-jnp.arange(n_freq, dtype=F32) / n_freq)
    ang = jnp.concatenate([row[:, None] * inv_freq, col[:, None] * inv_freq], axis=-1)
    cos, sin = jnp.cos(ang), jnp.sin(ang)
    zeros = jnp.zeros((n, LANE - MLA_ROPE_DIM), F32)
    ct = jnp.concatenate([cos, cos, zeros], axis=-1)
    st = jnp.concatenate([-sin, sin, zeros], axis=-1)
    ct_ctx = jnp.concatenate([jnp.ones((b * c, MLA_ROPE_DIM), F32), jnp.zeros((b * c, LANE - MLA_ROPE_DIM), F32)], -1)
    ct = jnp.concatenate([jnp.tile(ct, (b, 1)), ct_ctx], axis=0)
    st = jnp.concatenate([jnp.tile(st, (b, 1)), jnp.zeros((b * c, LANE), F32)], axis=0)
    return ct, st


def _swap_halves(w):
    half = w.shape[-1] // 2
    return jnp.concatenate([w[..., half:], w[..., :half]], axis=-1)


def kernel(x, c, ctx, c_ctx, w_mod, b_mod, norm_mix, norm_ffn, w_in, na_rpb, lru_conv_w, lru_conv_b, lru_w_a,
           lru_b_a, lru_w_x, lru_b_x, lru_lam, mla_q_norm, mla_kv_norm, mla_w_q_up, mla_w_kv_up, w_branch, w_out,
           ffn_w_up, ffn_conv_w, ffn_conv_b, ffn_w_down, norm_final):
    b, n, d = x.shape
    ctx_len = ctx.shape[1]
    depth = w_mod.shape[0]
    na_heads = na_rpb.shape[1]
    na_width = na_heads * NA_HEAD_DIM
    lru_width = lru_conv_w.shape[2]
    q_rank = mla_q_norm.shape[1]
    kv_rank = mla_kv_norm.shape[1]
    mla_heads = mla_w_q_up.shape[2] // MLA_QK_DIM
    bn, bc = b * n, b * ctx_len
    assert na_width == lru_width == mla_heads * MLA_V_DIM == w_branch.shape[2]
    assert n % (NA_KROWS * GRID_W) == 0 and n % ctx_len == 0

    off_lru_x = 3 * na_width
    off_lru_g = off_lru_x + lru_width
    off_mla = off_lru_g + lru_width
    mla_used = q_rank + kv_rank + MLA_ROPE_DIM
    mla_group = -(-mla_used // LANE) * LANE
    while off_mla % mla_group:
        mla_group += LANE
    off_gate = off_mla + mla_group
    dims = dict(b=b, n=n, c=ctx_len, d=d, bn=bn, na_heads=na_heads, lru_width=lru_width, q_rank=q_rank,
                kv_rank=kv_rank, mla_heads=mla_heads, off_lru_x=off_lru_x, off_lru_g=off_lru_g, off_mla=off_mla,
                mla_group=mla_group, off_gate=off_gate, tile_rows=int(np.gcd(n, bc)))

    w_in_b = w_in.astype(BF16)
    w_gate_b = w_in_b[:, :, off_mla + mla_used:]
    wq = mla_w_q_up.reshape(depth, q_rank, mla_heads, MLA_QK_DIM)
    wq_rope = wq[..., MLA_NOPE_DIM:]
    wq_p = jnp.concatenate([wq[..., :MLA_NOPE_DIM], wq_rope, _swap_halves(wq_rope)], axis=-1)
    wq_p = wq_p.reshape(depth, q_rank, mla_heads * MLA_HEAD_PAD).astype(BF16)
    wkv = mla_w_kv_up.reshape(depth, kv_rank, mla_heads, MLA_NOPE_DIM + MLA_V_DIM)
    wkv_p = jnp.concatenate([wkv[..., :MLA_NOPE_DIM].reshape(depth, kv_rank, -1),
                             wkv[..., MLA_NOPE_DIM:].reshape(depth, kv_rank, -1)], axis=-1).astype(BF16)
    w_branch_b = w_branch.astype(BF16)
    w_out_b = w_out.astype(BF16)
    w_up_b = ffn_w_up.astype(BF16)
    w_down_b = ffn_w_down.astype(BF16)
    lru_w_a_b = lru_w_a.astype(BF16)
    lru_w_x_b = lru_w_x.astype(BF16)
    ct, st = _rope_tables(n, b, ctx_len)
    na_bias = _na_bias_blocks(na_rpb)

    n_groups = 1 + b
    pad_rows = -n_groups % SUBLANE_F32
    cond = jnp.concatenate([c_ctx[None, :], c, jnp.zeros((pad_rows, d), F32)], axis=0)
    mod = _modulation(cond, w_mod, b_mod)[:, :n_groups].reshape(depth, n_groups, N_MOD, 1, d)

    r3 = lambda a: a.reshape(a.shape[0], 1, a.shape[1])
    r4 = lambda a: a.reshape(a.shape[0], a.shape[1], 1, a.shape[2])
    norm_mix3, norm_ffn3 = r3(norm_mix), r3(norm_ffn)
    lru_conv_b3 = r3(lru_conv_b)
    lru_b_a4, lru_b_x4, lru_lam4 = r4(lru_b_a), r4(lru_b_x), r4(lru_lam)
    q_norm3, kv_norm3 = r3(mla_q_norm), r3(mla_kv_norm)
    ffn_conv_b3 = r3(ffn_conv_b)
    norm_final2 = norm_final.reshape(1, d)

    xall = jnp.concatenate([x.reshape(bn, d), ctx.reshape(bc, d)], axis=0)
    for l in range(depth):
        last = l == depth - 1
        n_rows = bn if last else bn + bc
        sh1, sc1, g1, sh2, sc2, g2 = (mod[l, :, k] for k in range(N_MOD))
        z = _in_proj(xall, norm_mix3, sh1, sc1, w_in_b, w_gate_b, l, dims)
        out_a = _na_attention(z, na_bias, l, dims)
        lru_args = (lru_conv_w, lru_conv_b3, lru_w_a_b, lru_b_a4, lru_w_x_b, lru_b_x4, lru_lam4)
        lru_fwd = _lru_scan(z, l, 0, *lru_args, None, dims)
        out_b = _lru_scan(z, l, 1, *lru_args, lru_fwd, dims)
        q_m, k_m, v_m = _mla_proj(z, l, q_norm3, kv_norm3, wq_p, wkv_p, ct, st, dims)
        kt_m = k_m.reshape(bn + bc, mla_heads, MLA_HEAD_PAD).transpose(1, 2, 0)
        out_c = _mla_attention(q_m, kt_m, v_m, dims)
        if not last:
            out_a = _ctx_attention(z, z, z, out_a, dims, na_heads, NA_HEAD_DIM, 0, na_heads, 2 * na_heads, 1,
                                   NA_HEAD_DIM ** -0.5, "na_ctx_attention")
            out_c = _ctx_attention(q_m, k_m, v_m, out_c, dims, mla_heads, MLA_HEAD_PAD, 0, 0, 0, 2, 1.0,
                                   "mla_ctx_attention")
        y = _merge(out_a, out_b, out_c, w_branch_b, z, l, n_rows, dims)
        xall = _out_proj(y, w_out_b, xall, g1, l, dims)
        ffn_args = (norm_ffn3, sh2, sc2, g2, w_up_b, ffn_conv_w, ffn_conv_b3, w_down_b, norm_final2, l)
        x_new = _ffn(xall, None, *ffn_args, 0, bn, n, n_rows, last, dims)
        if not last:
            x_new = _ffn(xall, x_new, *ffn_args, bn, bc, ctx_len, n_rows, False, dims)
        xall = x_new
    return xall.reshape(b, n, d)
```

```python
import functools

import numpy as np
import jax
import jax.numpy as jnp
from jax import lax
from jax.experimental import pallas as pl
from jax.experimental.pallas import tpu as pltpu

F32 = jnp.float32
BF16 = jnp.bfloat16

GRID_W = 64
NORM_EPS = 1e-6
NEG_INF = -1e30
N_MOD = 6
NA_HEAD_DIM = 128
NA_WIN_H = 8
NA_WIN_W = 16
LRU_C = 8.0
MLA_NOPE_DIM = 128
MLA_ROPE_DIM = 64
MLA_V_DIM = 128
MLA_QK_DIM = MLA_NOPE_DIM + MLA_ROPE_DIM
ROPE_THETA = 10000.0

LANE = 128
SUBLANE_F32 = 8
SUBLANE_BF16 = 16
MLA_HEAD_PAD = 256
NA_QROWS = 8
NA_KROWS = 16
MIB = 1024 * 1024


def _cparams(semantics, vmem_mib):
    return pltpu.CompilerParams(dimension_semantics=semantics, vmem_limit_bytes=vmem_mib * MIB)


def _tile(n, target, mult):
    best = None
    for t in range(mult, min(n, target) + 1, mult):
        if n % t == 0:
            best = t
    assert best is not None, (n, target, mult)
    return best


def _group_of_tile(i, tm, bn, n):
    return jnp.where(i * tm >= bn, 0, 1 + (i * tm) // n)


def _rmsnorm_f32(x, gamma):
    ms = jnp.mean(x * x, axis=-1, keepdims=True)
    return x * lax.rsqrt(ms + NORM_EPS) * gamma


def _mod_kernel(c_ref, w_ref, b_ref, o_ref):
    cc = c_ref[...]
    s = cc * jax.nn.sigmoid(cc)
    o_ref[...] = jnp.dot(s, w_ref[...], preferred_element_type=F32,
                         precision=lax.Precision.HIGHEST) + b_ref[...]


def _modulation(cond, w_mod, b_mod):
    depth, d, nm = w_mod.shape
    rows = cond.shape[0]
    tn = _tile(nm, 1024, LANE)
    return pl.pallas_call(
        _mod_kernel,
        out_shape=jax.ShapeDtypeStruct((depth, rows, nm), F32),
        grid=(depth, nm // tn),
        in_specs=[
            pl.BlockSpec((rows, d), lambda l, j: (0, 0)),
            pl.BlockSpec((None, d, tn), lambda l, j: (l, 0, j)),
            pl.BlockSpec((None, 1, tn), lambda l, j: (l, 0, j)),
        ],
        out_specs=pl.BlockSpec((None, rows, tn), lambda l, j: (l, 0, j)),
        compiler_params=_cparams(("arbitrary", "arbitrary"), 40),
        name="adaln_mod",
    )(cond, w_mod, b_mod.reshape(depth, 1, nm))


def _in_proj_kernel(x_ref, gam_ref, sh_ref, sc_ref, w_ref, wg_ref, o_ref, h_ref, *, n_head_tiles):
    j = pl.program_id(1)

    @pl.when(j == 0)
    def _():
        y = _rmsnorm_f32(x_ref[...], gam_ref[...])
        h_ref[...] = (y * (1.0 + sc_ref[0]) + sh_ref[0]).astype(BF16)

    @pl.when(j < n_head_tiles)
    def _():
        o_ref[...] = jnp.dot(h_ref[...], w_ref[...], preferred_element_type=F32).astype(o_ref.dtype)

    @pl.when(j >= n_head_tiles)
    def _():
        o_ref[...] = jnp.dot(h_ref[...], wg_ref[...], preferred_element_type=F32).astype(o_ref.dtype)


def _in_proj(xall, gamma, shift, scale, w, w_gate, l, dims):
    tt, d = xall.shape
    n_head, n_gate = dims["off_gate"], w_gate.shape[2]
    tm = _tile(dims["tile_rows"], 1024, SUBLANE_BF16)
    tn = _tile(int(np.gcd(n_head, n_gate)), 1024, LANE)
    n_head_tiles = n_head // tn
    assert n_head <= w.shape[2]
    grp = functools.partial(_group_of_tile, tm=tm, bn=dims["bn"], n=dims["n"])
    return pl.pallas_call(
        functools.partial(_in_proj_kernel, n_head_tiles=n_head_tiles),
        out_shape=jax.ShapeDtypeStruct((tt, n_head + n_gate), BF16),
        grid=(tt // tm, (n_head + n_gate) // tn),
        in_specs=[
            pl.BlockSpec((tm, d), lambda i, j: (i, 0)),
            pl.BlockSpec((None, 1, d), lambda i, j: (l, 0, 0)),
            pl.BlockSpec((1, 1, d), lambda i, j: (grp(i), 0, 0)),
            pl.BlockSpec((1, 1, d), lambda i, j: (grp(i), 0, 0)),
            pl.BlockSpec((None, d, tn), lambda i, j: (l, 0, jnp.minimum(j, n_head_tiles - 1))),
            pl.BlockSpec((None, d, tn), lambda i, j: (l, 0, jnp.maximum(j - n_head_tiles, 0))),
        ],
        out_specs=pl.BlockSpec((tm, tn), lambda i, j: (i, j)),
        scratch_shapes=[pltpu.VMEM((tm, d), BF16)],
        compiler_params=_cparams(("parallel", "arbitrary"), 48),
        name="in_proj",
    )(xall, gamma, shift, scale, w, w_gate)


def _na_group_layout(g, grid_rows):
    kh = min(NA_WIN_H, grid_rows)
    k0 = int(np.clip(NA_QROWS * g - NA_WIN_H // 2, 0, grid_rows - NA_KROWS))
    layout = []
    for qi in range(NA_QROWS):
        r = NA_QROWS * g + qi
        first_key_row = int(np.clip(r - kh // 2, 0, grid_rows - kh))
        layout.append(tuple((k0 + kj) - r + NA_WIN_H - 1 if first_key_row <= k0 + kj < first_key_row + kh
                            else None for kj in range(NA_KROWS)))
    return k0, tuple(layout)


def _na_variants(grid_rows):
    layouts = [_na_group_layout(g, grid_rows)[1] for g in range(grid_rows // NA_QROWS)]
    distinct = sorted(set(layouts), key=layouts.index)
    return distinct, [distinct.index(lay) for lay in layouts]


def _na_kernel(q_ref, k_ref, v_ref, kc_ref, vc_ref, blk_ref, o_ref, bias_ref, *, grid_rows, scale):
    n_groups = grid_rows // NA_QROWS
    layouts, variant_of_group = _na_variants(grid_rows)
    tq = NA_QROWS * GRID_W
    nk = NA_KROWS * GRID_W
    nt = (((1,), (1,)), ((), ()))

    @pl.when(pl.program_id(1) == 0)
    def _():
        left_half = lax.broadcasted_iota(jnp.int32, (GRID_W, 2 * GRID_W), 1) < GRID_W
        masked = blk_ref.shape[0] - 1
        for var, layout in enumerate(layouts):
            for qi, row in enumerate(layout):
                for kp in range(NA_KROWS // 2):
                    r0, r1 = (masked if r is None else r for r in row[2 * kp:2 * kp + 2])
                    pair = blk_ref[r0] if r0 == r1 else jnp.where(left_half, blk_ref[r0], blk_ref[r1])
                    bias_ref[var, qi * GRID_W:(qi + 1) * GRID_W, 2 * kp * GRID_W:2 * (kp + 1) * GRID_W] = pair

    def key_start(g):
        return _na_group_layout(g, grid_rows)[0] * GRID_W

    def scores(g):
        q = q_ref[g * tq:(g + 1) * tq, :]
        k0 = key_start(g)
        var = variant_of_group[g]
        s = lax.dot_general(q, k_ref[k0:k0 + nk, :], nt, preferred_element_type=F32) * scale + bias_ref[var]
        sc = lax.dot_general(q, kc_ref[...], nt, preferred_element_type=F32) * scale
        return s, sc

    nxt = scores(0)
    for g in range(n_groups):
        s, sc = nxt
        if g + 1 < n_groups:
            nxt = scores(g + 1)
        k0 = key_start(g)
        m = jnp.maximum(jnp.max(s, axis=-1, keepdims=True), jnp.max(sc, axis=-1, keepdims=True))
        p = jnp.exp(s - m)
        pc = jnp.exp(sc - m)
        denom = jnp.sum(p, axis=-1, keepdims=True) + jnp.sum(pc, axis=-1, keepdims=True)
        o = jnp.dot(p.astype(BF16), v_ref[k0:k0 + nk, :], preferred_element_type=F32)
        o = o + jnp.dot(pc.astype(BF16), vc_ref[...], preferred_element_type=F32)
        o_ref[g * tq:(g + 1) * tq, :] = (o * (1.0 / denom)).astype(o_ref.dtype)


def _na_bias_blocks(rpb):
    lead = rpb.shape[:2]
    cidx = np.arange(GRID_W)
    c_start = np.clip(cidx - NA_WIN_W // 2, 0, GRID_W - NA_WIN_W)
    in_win = (cidx[None, :] >= c_start[:, None]) & (cidx[None, :] < c_start[:, None] + NA_WIN_W)
    col_idx = np.clip(cidx[None, :] - cidx[:, None], -(NA_WIN_W - 1), NA_WIN_W - 1) + (NA_WIN_W - 1)
    n_rel = 2 * NA_WIN_W - 1
    expand = (np.arange(n_rel)[:, None, None] == col_idx[None]) & in_win[None]
    expand = jnp.asarray(expand.reshape(n_rel, GRID_W * GRID_W), F32)
    blocks = jnp.einsum("lhrd,dx->lhrx", rpb.astype(F32), expand, precision=lax.Precision.HIGHEST)
    blocks = blocks + jnp.asarray(np.where(in_win, 0.0, NEG_INF).reshape(-1), F32)
    blocks = blocks.reshape(*lead, 2 * NA_WIN_H - 1, GRID_W, GRID_W)
    blocks = jnp.concatenate([blocks, jnp.full((*lead, 1, GRID_W, GRID_W), NEG_INF, F32)], axis=2)
    return jnp.concatenate([blocks, blocks], axis=-1)


def _na_attention(z, bias_blocks, l, dims):
    tt = z.shape[0]
    b, n, c = dims["b"], dims["n"], dims["c"]
    heads = dims["na_heads"]
    grid_rows = n // GRID_W
    n_var = len(_na_variants(grid_rows)[0])
    ctx_blk0 = dims["bn"] // c
    kern = functools.partial(_na_kernel, grid_rows=grid_rows, scale=NA_HEAD_DIM ** -0.5)
    return pl.pallas_call(
        kern,
        out_shape=jax.ShapeDtypeStruct((tt, heads * NA_HEAD_DIM), BF16),
        grid=(heads, b),
        in_specs=[
            pl.BlockSpec((n, NA_HEAD_DIM), lambda h, bi: (bi, h)),
            pl.BlockSpec((n, NA_HEAD_DIM), lambda h, bi: (bi, heads + h)),
            pl.BlockSpec((n, NA_HEAD_DIM), lambda h, bi: (bi, 2 * heads + h)),
            pl.BlockSpec((c, NA_HEAD_DIM), lambda h, bi: (ctx_blk0 + bi, heads + h)),
            pl.BlockSpec((c, NA_HEAD_DIM), lambda h, bi: (ctx_blk0 + bi, 2 * heads + h)),
            pl.BlockSpec((None, None) + bias_blocks.shape[2:], lambda h, bi: (l, h, 0, 0, 0)),
        ],
        out_specs=pl.BlockSpec((n, NA_HEAD_DIM), lambda h, bi: (bi, h)),
        scratch_shapes=[pltpu.VMEM((n_var, NA_QROWS * GRID_W, NA_KROWS * GRID_W), F32)],
        compiler_params=_cparams(("arbitrary", "arbitrary"), 48),
        name="na_attention",
    )(z, z, z, z, z, bias_blocks)


def _ctx_attn_kernel(q_ref, k_ref, v_ref, prev_ref, o_ref, *, scale):
    del prev_ref
    nt = (((1,), (1,)), ((), ()))
    s = lax.dot_general(q_ref[...], k_ref[...], nt, preferred_element_type=F32) * scale
    m = jnp.max(s, axis=-1, keepdims=True)
    p = jnp.exp(s - m)
    denom = jnp.sum(p, axis=-1, keepdims=True)
    o = jnp.dot(p.astype(BF16), v_ref[...], preferred_element_type=F32)
    o_ref[...] = (o * (1.0 / denom)).astype(o_ref.dtype)


def _ctx_attention(q_arr, k_arr, v_arr, out_buf, dims, heads, dq, q_col0, k_col0, v_col0, v_step, scale, name):
    b, c = dims["b"], dims["c"]
    blk0 = dims["bn"] // c
    dv = out_buf.shape[1] // heads
    return pl.pallas_call(
        functools.partial(_ctx_attn_kernel, scale=scale),
        out_shape=jax.ShapeDtypeStruct(out_buf.shape, out_buf.dtype),
        grid=(b, heads),
        in_specs=[
            pl.BlockSpec((c, dq), lambda bi, h: (blk0 + bi, q_col0 + h)),
            pl.BlockSpec((c, dq), lambda bi, h: (blk0 + bi, k_col0 + h)),
            pl.BlockSpec((c, dv), lambda bi, h: (blk0 + bi, v_col0 + v_step * h)),
            pl.BlockSpec(memory_space=pl.ANY),
        ],
        out_specs=pl.BlockSpec((c, dv), lambda bi, h: (blk0 + bi, h)),
        input_output_aliases={3: 0},
        compiler_params=_cparams(("parallel", "parallel"), 32),
        name=name,
    )(q_arr, k_arr, v_arr, out_buf)


def _lru_kernel(*refs, reverse, t_chunk, n_ctx_chunks, n_lat_chunks, n_blocks, block_dim):
    if reverse:
        u_ref, wa_ref, ba_ref, wx_ref, bx_ref, lam_ref, hf_ref, g_ref, o_ref, carry_ref = refs
    else:
        (x_ref, xp_ref, xn_ref, cw_ref, cb_ref, wa_ref, ba_ref, wx_ref, bx_ref, lam_ref,
         o_ref, u_ref, carry_ref) = refs
    j = pl.program_id(1)

    @pl.when(j == 0)
    def _():
        carry_ref[...] = jnp.zeros_like(carry_ref)

    if reverse:
        u = u_ref[...]
    else:
        is_ctx = j < n_ctx_chunks
        chunk = jnp.where(is_ctx, j, j - n_ctx_chunks)
        first = chunk == 0
        last = chunk == jnp.where(is_ctx, n_ctx_chunks, n_lat_chunks) - 1
        xm = x_ref[...].astype(F32)
        xp = xp_ref[...].astype(F32)
        xn = xn_ref[...].astype(F32)
        hp = xp.shape[0]
        pm1 = jnp.where(first, 0.0, xp[hp - 1:hp, :])
        pm2 = jnp.where(first, 0.0, xp[hp - 2:hp - 1, :])
        nn0 = jnp.where(last, 0.0, xn[0:1, :])
        t = lax.broadcasted_iota(jnp.int32, (t_chunk, 1), 0)
        xm1 = jnp.where(t == 0, pm1, pltpu.roll(xm, 1, 0))
        xm2 = jnp.where(t == 0, pm2, jnp.where(t == 1, pm1, pltpu.roll(xm, 2, 0)))
        xp1 = jnp.where(t == t_chunk - 1, nn0, pltpu.roll(xm, t_chunk - 1, 0))
        cw = cw_ref[...]
        u = cw[0:1] * xm2 + cw[1:2] * xm1 + cw[2:3] * xm + cw[3:4] * xp1 + cb_ref[...]
        u_ref[...] = u

    ub = u.astype(BF16)
    ga, gx = [], []
    for kb in range(n_blocks):
        blk = ub[:, kb * block_dim:(kb + 1) * block_dim]
        ga.append(jnp.dot(blk, wa_ref[kb], preferred_element_type=F32))
        gx.append(jnp.dot(blk, wx_ref[kb], preferred_element_type=F32))
    gate_a = jnp.concatenate(ga, axis=1) + ba_ref[...]
    gate_x = jnp.concatenate(gx, axis=1) + bx_ref[...]
    r = jax.nn.sigmoid(gate_a)
    i_gate = jax.nn.sigmoid(gate_x)
    nlam = -lam_ref[...]
    softplus = jnp.maximum(nlam, 0.0) + jnp.log1p(jnp.exp(-jnp.abs(nlam)))
    log_a = -LRU_C * r * softplus
    a = jnp.exp(log_a)
    bv = jnp.sqrt(-jnp.tanh(log_a) * (a * a + 1.0)) * (i_gate * u)

    grp = SUBLANE_F32
    sub = lax.broadcasted_iota(jnp.int32, (grp, 1), 0)
    order = range(t_chunk // grp - 1, -1, -1) if reverse else range(t_chunk // grp)
    carry = carry_ref[0:1, :]
    h_groups = {}
    for gi in order:
        a_g = a[gi * grp:(gi + 1) * grp, :]
        b_g = bv[gi * grp:(gi + 1) * grp, :]
        k = 1
        while k < grp:
            keep = (sub < grp - k) if reverse else (sub >= k)
            shift = grp - k if reverse else k
            b_g = jnp.where(keep, a_g * pltpu.roll(b_g, shift, 0) + b_g, b_g)
            a_g = jnp.where(keep, a_g * pltpu.roll(a_g, shift, 0), a_g)
            k *= 2
        h_g = a_g * carry + b_g
        carry = h_g[0:1, :] if reverse else h_g[grp - 1:grp, :]
        h_groups[gi] = h_g
    h = jnp.concatenate([h_groups[gi] for gi in range(t_chunk // grp)], axis=0)
    carry_ref[0:1, :] = carry
    if reverse:
        g = g_ref[...].astype(F32)
        o_ref[...] = (jax.nn.gelu(g) * (hf_ref[...] + h)).astype(o_ref.dtype)
    else:
        o_ref[...] = h


def _lru_scan(z, l, direction, conv_w, conv_b, w_a, b_a, w_x, b_x, lam, fwd_out, dims):
    tt = z.shape[0]
    b, n, c, lw = dims["b"], dims["n"], dims["c"], dims["lru_width"]
    n_blocks, block_dim = w_a.shape[2], w_a.shape[3]
    t_chunk = _tile(c, 256, SUBLANE_BF16)
    halo = SUBLANE_BF16
    n_ctx_chunks, n_lat_chunks = c // t_chunk, n // t_chunk
    ctx_chunk0 = dims["bn"] // t_chunk
    col_x = dims["off_lru_x"] // lw
    col_g = dims["off_lru_g"] // lw
    reverse = direction == 1
    per_halo = t_chunk // halo
    n_halo_blocks = tt // halo

    def chunk_block(bi, j):
        is_ctx = j < n_ctx_chunks
        pos = jnp.where(is_ctx, j, j - n_ctx_chunks)
        if reverse:
            pos = jnp.where(is_ctx, n_ctx_chunks - 1 - pos, n_lat_chunks - 1 - pos)
        return jnp.where(is_ctx, ctx_chunk0 + bi * n_ctx_chunks + pos, bi * n_lat_chunks + pos)

    def prev_block(bi, j):
        return jnp.maximum(chunk_block(bi, j) * per_halo - 1, 0)

    def next_block(bi, j):
        return jnp.minimum((chunk_block(bi, j) + 1) * per_halo, n_halo_blocks - 1)

    vec = lambda: pl.BlockSpec((None, None, 1, lw), lambda bi, j: (l, direction, 0, 0))
    gate_w = lambda: pl.BlockSpec((None, None, n_blocks, block_dim, block_dim),
                                  lambda bi, j: (l, direction, 0, 0, 0))
    chunk_spec = lambda col: pl.BlockSpec((t_chunk, lw), lambda bi, j: (chunk_block(bi, j), col))
    gate_specs = [gate_w(), vec(), gate_w(), vec(), vec()]
    gate_args = [w_a, b_a, w_x, b_x, lam]
    if reverse:
        h_fwd, u = fwd_out
        in_specs = [chunk_spec(0)] + gate_specs + [chunk_spec(0), chunk_spec(col_g)]
        args = [u] + gate_args + [h_fwd, z]
        out_shape = jax.ShapeDtypeStruct((tt, lw), BF16)
        out_specs = chunk_spec(0)
    else:
        in_specs = [
            chunk_spec(col_x),
            pl.BlockSpec((halo, lw), lambda bi, j: (prev_block(bi, j), col_x)),
            pl.BlockSpec((halo, lw), lambda bi, j: (next_block(bi, j), col_x)),
            pl.BlockSpec((None, conv_w.shape[1], lw), lambda bi, j: (l, 0, 0)),
            pl.BlockSpec((None, 1, lw), lambda bi, j: (l, 0, 0)),
        ] + gate_specs
        args = [z, z, z, conv_w, conv_b] + gate_args
        out_shape = (jax.ShapeDtypeStruct((tt, lw), F32), jax.ShapeDtypeStruct((tt, lw), F32))
        out_specs = (chunk_spec(0), chunk_spec(0))
    kern = functools.partial(_lru_kernel, reverse=reverse, t_chunk=t_chunk, n_ctx_chunks=n_ctx_chunks,
                             n_lat_chunks=n_lat_chunks, n_blocks=n_blocks, block_dim=block_dim)
    return pl.pallas_call(
        kern,
        out_shape=out_shape,
        grid=(b, n_ctx_chunks + n_lat_chunks),
        in_specs=in_specs,
        out_specs=out_specs,
        scratch_shapes=[pltpu.VMEM((SUBLANE_F32, lw), F32)],
        compiler_params=_cparams(("parallel", "arbitrary"), 40),
        name="lru_bwd" if reverse else "lru_fwd",
    )(*args)


def _mla_proj_kernel(z_ref, qn_ref, kvn_ref, wq_ref, wkv_ref, ct_ref, st_ref, q_ref, k_ref, v_ref,
                     *, q_rank, kv_rank, heads, q_scale):
    z = z_ref[...]
    cq = z[:, :q_rank].astype(F32)
    ckv = z[:, q_rank:q_rank + kv_rank].astype(F32)
    kr_tile = z[:, q_rank + kv_rank:q_rank + kv_rank + LANE].astype(F32)
    ct = ct_ref[...]
    st = st_ref[...]

    def rope(tile):
        return tile * ct + pltpu.roll(tile, LANE // 2, 1) * st

    quarter = MLA_ROPE_DIM // 2
    lane = lax.broadcasted_iota(jnp.int32, kr_tile.shape, 1)
    swapped = jnp.where(lane < MLA_ROPE_DIM + quarter, pltpu.roll(kr_tile, quarter, 1),
                        pltpu.roll(kr_tile, LANE - quarter, 1))
    krp = jnp.where(lane < MLA_ROPE_DIM, kr_tile, swapped)

    qn = _rmsnorm_f32(cq, qn_ref[...]).astype(BF16)
    q = jnp.dot(qn, wq_ref[...], preferred_element_type=F32)
    kvn = _rmsnorm_f32(ckv, kvn_ref[...]).astype(BF16)
    kv = jnp.dot(kvn, wkv_ref[...], preferred_element_type=F32)
    kr = rope(krp).astype(BF16)
    for h in range(heads):
        c0 = h * MLA_HEAD_PAD
        q_ref[:, c0:c0 + LANE] = (q[:, c0:c0 + LANE] * q_scale).astype(BF16)
        q_ref[:, c0 + LANE:c0 + 2 * LANE] = (rope(q[:, c0 + LANE:c0 + 2 * LANE]) * q_scale).astype(BF16)
        k_ref[:, c0:c0 + LANE] = kv[:, h * LANE:(h + 1) * LANE].astype(BF16)
        k_ref[:, c0 + LANE:c0 + 2 * LANE] = kr
        v_ref[:, c0:c0 + LANE] = kv[:, (heads + h) * LANE:(heads + h + 1) * LANE].astype(BF16)
        v_ref[:, c0 + LANE:c0 + 2 * LANE] = jnp.ones((z.shape[0], LANE), BF16)


def _mla_proj(z, l, q_norm, kv_norm, wq, wkv, ct, st, dims):
    tt = z.shape[0]
    heads, q_rank, kv_rank = dims["mla_heads"], dims["q_rank"], dims["kv_rank"]
    mg = dims["mla_group"]
    tm = _tile(dims["tile_rows"], 512, SUBLANE_BF16)
    col = dims["off_mla"] // mg
    kern = functools.partial(_mla_proj_kernel, q_rank=q_rank, kv_rank=kv_rank, heads=heads,
                             q_scale=MLA_QK_DIM ** -0.5)
    return pl.pallas_call(
        kern,
        out_shape=(jax.ShapeDtypeStruct((tt, heads * MLA_HEAD_PAD), BF16),
                   jax.ShapeDtypeStruct((tt, heads * MLA_HEAD_PAD), BF16),
                   jax.ShapeDtypeStruct((tt, heads * MLA_HEAD_PAD), BF16)),
        grid=(tt // tm,),
        in_specs=[
            pl.BlockSpec((tm, mg), lambda i: (i, col)),
            pl.BlockSpec((None, 1, q_rank), lambda i: (l, 0, 0)),
            pl.BlockSpec((None, 1, kv_rank), lambda i: (l, 0, 0)),
            pl.BlockSpec((None, q_rank, heads * MLA_HEAD_PAD), lambda i: (l, 0, 0)),
            pl.BlockSpec((None, kv_rank, heads * 2 * LANE), lambda i: (l, 0, 0)),
            pl.BlockSpec((tm, LANE), lambda i: (i, 0)),
            pl.BlockSpec((tm, LANE), lambda i: (i, 0)),
        ],
        out_specs=(pl.BlockSpec((tm, heads * MLA_HEAD_PAD), lambda i: (i, 0)),
                   pl.BlockSpec((tm, heads * MLA_HEAD_PAD), lambda i: (i, 0)),
                   pl.BlockSpec((tm, heads * MLA_HEAD_PAD), lambda i: (i, 0))),
        compiler_params=_cparams(("parallel",), 48),
        name="mla_proj",
    )(z, q_norm, kv_norm, wq, wkv, ct, st)


def _mla_flash_kernel(q_ref, kt_ref, v_ref, ktc_ref, vc_ref, o_ref, *, sub, tk):
    n_chunks = kt_ref.shape[1] // tk
    chunks = [None] + list(range(n_chunks))

    def scores(q, ch):
        kt = ktc_ref[...] if ch is None else kt_ref[:, ch * tk:(ch + 1) * tk]
        return jnp.dot(q, kt, preferred_element_type=F32)

    def values(ch):
        return vc_ref[...] if ch is None else v_ref[ch * tk:(ch + 1) * tk, :]

    for a in range(q_ref.shape[0] // sub):
        rows = slice(a * sub, (a + 1) * sub)
        q = q_ref[rows, :]
        s_next = scores(q, chunks[0])
        m = acc = None
        for idx, ch in enumerate(chunks):
            s = s_next
            if idx + 1 < len(chunks):
                s_next = scores(q, chunks[idx + 1])
            m_new = jnp.max(s, axis=-1, keepdims=True)
            if m is not None:
                m_new = jnp.maximum(m, m_new)
            p = jnp.exp(s - m_new).astype(BF16)
            pv = jnp.dot(p, values(ch), preferred_element_type=F32)
            acc = pv if m is None else jnp.exp(m - m_new) * acc + pv
            m = m_new
        o_ref[rows, :] = (acc[:, :MLA_V_DIM] * (1.0 / acc[:, MLA_V_DIM:])).astype(o_ref.dtype)


def _mla_attention(q, kt, v, dims):
    tt = q.shape[0]
    b, n, c, heads = dims["b"], dims["n"], dims["c"], dims["mla_heads"]
    tq = _tile(n, 1024, SUBLANE_BF16)
    sub = _tile(tq, 512, SUBLANE_BF16)
    ctx_blk0 = dims["bn"] // c
    nq = n // tq
    vw = 2 * MLA_V_DIM
    return pl.pallas_call(
        functools.partial(_mla_flash_kernel, sub=sub, tk=_tile(n, 2048, LANE)),
        out_shape=jax.ShapeDtypeStruct((tt, heads * MLA_V_DIM), BF16),
        grid=(b, heads, nq),
        in_specs=[
            pl.BlockSpec((tq, MLA_HEAD_PAD), lambda bi, h, i: (bi * nq + i, h)),
            pl.BlockSpec((None, MLA_HEAD_PAD, n), lambda bi, h, i: (h, 0, bi)),
            pl.BlockSpec((n, vw), lambda bi, h, i: (bi, h)),
            pl.BlockSpec((None, MLA_HEAD_PAD, c), lambda bi, h, i: (h, 0, ctx_blk0 + bi)),
            pl.BlockSpec((c, vw), lambda bi, h, i: (ctx_blk0 + bi, h)),
        ],
        out_specs=pl.BlockSpec((tq, MLA_V_DIM), lambda bi, h, i: (bi * nq + i, h)),
        compiler_params=_cparams(("parallel", "parallel", "arbitrary"), 48),
        name="mla_attention",
    )(q, kt, v, kt, v)


def _merge_kernel(ba_ref, bb_ref, bc_ref, wb_ref, ga_ref, gb_ref, gc_ref, o_ref):
    y = None
    for br, gate, i in ((ba_ref, ga_ref, 0), (bb_ref, gb_ref, 1), (bc_ref, gc_ref, 2)):
        t = jnp.dot(br[...], wb_ref[i], preferred_element_type=F32)
        t = jax.nn.sigmoid(gate[...].astype(F32)) * t
        y = t if y is None else y + t
    o_ref[...] = y.astype(o_ref.dtype)


def _merge(br_a, br_b, br_c, w_branch, z, l, n_rows, dims):
    d = dims["d"]
    bw = br_a.shape[1]
    tm = _tile(dims["tile_rows"], 1024, SUBLANE_BF16)
    tn = _tile(d, 1024, LANE)
    gate_col0 = dims["off_gate"] // tn
    per_gate = d // tn
    branch = lambda: pl.BlockSpec((tm, bw), lambda i, j: (i, 0))
    gate = lambda k: pl.BlockSpec((tm, tn), lambda i, j: (i, gate_col0 + k * per_gate + j))
    return pl.pallas_call(
        _merge_kernel,
        out_shape=jax.ShapeDtypeStruct((n_rows, d), BF16),
        grid=(n_rows // tm, d // tn),
        in_specs=[branch(), branch(), branch(),
                  pl.BlockSpec((None, 3, bw, tn), lambda i, j: (l, 0, 0, j)),
                  gate(0), gate(1), gate(2)],
        out_specs=pl.BlockSpec((tm, tn), lambda i, j: (i, j)),
        compiler_params=_cparams(("parallel", "arbitrary"), 56),
        name="merge",
    )(br_a, br_b, br_c, w_branch, z, z, z)


def _out_proj_kernel(y_ref, w_ref, x_ref, g_ref, o_ref):
    t = jnp.dot(y_ref[...], w_ref[...], preferred_element_type=F32)
    o_ref[...] = x_ref[...] + g_ref[0] * t


def _out_proj(y, w_out, xall, gate, l, dims):
    n_rows, d = y.shape
    tm = _tile(dims["tile_rows"], 1024, SUBLANE_BF16)
    tn = _tile(d, 1024, LANE)
    grp =functools.partial(_group_of_tile, tm=tm, bn=dims["bn"], n=dims["n"])
    return pl.pallas_call(
        _out_proj_kernel,
        out_shape=jax.ShapeDtypeStruct((n_rows, d), F32),
        grid=(n_rows // tm, d // tn),
        in_specs=[
            pl.BlockSpec((tm, d), lambda i, j: (i, 0)),
            pl.BlockSpec((None, d, tn), lambda i, j: (l, 0, j)),
            pl.BlockSpec((tm, tn), lambda i, j: (i, j)),
            pl.BlockSpec((1, 1, tn), lambda i, j: (grp(i), 0, j)),
        ],
        out_specs=pl.BlockSpec((tm, tn), lambda i, j: (i, j)),
        compiler_params=_cparams(("parallel", "arbitrary"), 48),
        name="out_proj",
    )(y, w_out, xall, gate)


def _ffn_kernel(x_ref, xp_ref, xn_ref, gam_ref, sh_ref, sc_ref, g2_ref, wv_ref, wg_ref, cwv_ref, cwg_ref,
                cbv_ref, cbg_ref, wd_ref, nf_ref, *rest, tm, tiles_per_seq, final_norm):
    o_ref, h_ref = rest[-2:]
    i = pl.program_id(0)
    f = pl.program_id(1)
    halo = SUBLANE_F32

    def norm_mod(x):
        y = _rmsnorm_f32(x, gam_ref[...])
        return y * (1.0 + sc_ref[0]) + sh_ref[0]

    @pl.when(f == 0)
    def _():
        h_ref[0:tm, :] = norm_mod(x_ref[...]).astype(BF16)
        hh = norm_mod(jnp.concatenate([xn_ref[...], xp_ref[...]], axis=0))
        pos = i % tiles_per_seq
        keep_next = (pos != tiles_per_seq - 1).astype(F32)
        keep_prev = (pos != 0).astype(F32)
        is_next = lax.broadcasted_iota(jnp.int32, (2 * halo, 1), 0) < halo
        h_ref[tm:tm + 2 * halo, :] = (hh * jnp.where(is_next, keep_next, keep_prev)).astype(BF16)
        o_ref[...] = jnp.zeros_like(o_ref)

    hv = h_ref[...]
    rows = tm + 2 * halo

    def conv(w_ref, cw_ref, cb_ref):
        zz = jnp.dot(hv, w_ref[...], preferred_element_type=F32)
        cw = cw_ref[...]
        zm1 = pltpu.roll(zz, 1, 0)[0:tm]
        zp1 = pltpu.roll(zz, rows - 1, 0)[0:tm]
        return cw[0:1] * zm1 + cw[1:2] * zz[0:tm] + cw[2:3] * zp1 + cb_ref[...]

    val = conv(wv_ref, cwv_ref, cbv_ref)
    gate = conv(wg_ref, cwg_ref, cbg_ref)
    act = (gate * jax.nn.sigmoid(gate) * val).astype(BF16)
    o_ref[...] += jnp.dot(act, wd_ref[...], preferred_element_type=F32)

    @pl.when(f == pl.num_programs(1) - 1)
    def _():
        xo = x_ref[...] + g2_ref[0] * o_ref[...]
        if final_norm:
            xo = _rmsnorm_f32(xo, nf_ref[...])
        o_ref[...] = xo


def _ffn(xall, out_buf, gamma, shift, scale, gate2, w_up, conv_w, conv_b, w_down, norm_final, l, row0, n_rows,
         seq_len, out_rows, final_norm, dims):
    tt, d = xall.shape
    ff = w_down.shape[1]
    tm = _tile(seq_len, 1024, SUBLANE_BF16)
    tf = _tile(ff, 512, LANE)
    nf = ff // tf
    halo = SUBLANE_F32
    per_halo = tm // halo
    n_halo_blocks = tt // halo
    tile0 = row0 // tm
    assert row0 % tm == 0 and tm % halo == 0
    grp = lambda i: _group_of_tile(tile0 + i, tm=tm, bn=dims["bn"], n=dims["n"])
    mod = lambda: pl.BlockSpec((1, 1, d), lambda i, f: (grp(i), 0, 0))
    kern = functools.partial(_ffn_kernel, tm=tm, tiles_per_seq=seq_len // tm, final_norm=final_norm)
    args = [xall, xall, xall, gamma, shift, scale, gate2, w_up, w_up, conv_w, conv_w, conv_b, conv_b, w_down,
            norm_final]
    extra_specs, aliases = [], {}
    if out_buf is not None:
        aliases = {len(args): 0}
        args.append(out_buf)
        extra_specs = [pl.BlockSpec(memory_space=pl.ANY)]
    return pl.pallas_call(
        kern,
        out_shape=jax.ShapeDtypeStruct((out_rows, d), F32),
        grid=(n_rows // tm, nf),
        in_specs=[
            pl.BlockSpec((tm, d), lambda i, f: (tile0 + i, 0), pipeline_mode=pl.Buffered(1)),
            pl.BlockSpec((halo, d), lambda i, f: (jnp.maximum((tile0 + i) * per_halo - 1, 0), 0)),
            pl.BlockSpec((halo, d), lambda i, f: (jnp.minimum((tile0 + i + 1) * per_halo, n_halo_blocks - 1), 0)),
            pl.BlockSpec((None, 1, d), lambda i, f: (l, 0, 0)),
            mod(), mod(), mod(),
            pl.BlockSpec((None, d, tf), lambda i, f: (l, 0, f)),
            pl.BlockSpec((None, d, tf), lambda i, f: (l, 0, nf + f)),
            pl.BlockSpec((None, conv_w.shape[1], tf), lambda i, f: (l, 0, f)),
            pl.BlockSpec((None, conv_w.shape[1], tf), lambda i, f: (l, 0, nf + f)),
            pl.BlockSpec((None, 1, tf), lambda i, f: (l, 0, f)),
            pl.BlockSpec((None, 1, tf), lambda i, f: (l, 0, nf + f)),
            pl.BlockSpec((None, tf, d), lambda i, f: (l, f, 0)),
            pl.BlockSpec((1, d), lambda i, f: (0, 0)),
        ] + extra_specs,
        out_specs=pl.BlockSpec((tm, d), lambda i, f: (tile0 + i, 0)),
        scratch_shapes=[pltpu.VMEM((tm + 2 * halo, d), BF16)],
        input_output_aliases=aliases,
        compiler_params=_cparams(("parallel", "arbitrary"), 58),
        name="conv_ffn",
    )(*args)


def _rope_tables(n, b, c):
    t = jnp.arange(n, dtype=jnp.int32)
    row = (t // GRID_W).astype(F32)
    col = (t % GRID_W).astype(F32)
    n_freq = MLA_ROPE_DIM // 4
    inv_freq = ROPE_THETA ** (-jnp.arange(n_freq, dtype=F32) / n_freq)
    ang = jnp.concatenate([row[:, None] * inv_freq, col[:, None] * inv_freq], axis=-1)
    cos, sin = jnp.cos(ang), jnp.sin(ang)
    zeros = jnp.zeros((n, LANE - MLA_ROPE_DIM), F32)
    ct = jnp.concatenate([cos, cos, zeros], axis=-1)
    st = jnp.concatenate([-sin, sin, zeros], axis=-1)
    ct_ctx = jnp.concatenate([jnp.ones((b * c, MLA_ROPE_DIM), F32), jnp.zeros((b * c, LANE - MLA_ROPE_DIM), F32)], -1)
    ct = jnp.concatenate([jnp.tile(ct, (b, 1)), ct_ctx], axis=0)
    st = jnp.concatenate([jnp.tile(st, (b, 1)), jnp.zeros((b * c, LANE), F32)], axis=0)
    return ct, st


def _swap_halves(w):
    half = w.shape[-1] // 2
    return jnp.concatenate([w[..., half:], w[..., :half]], axis=-1)


def kernel(x, c, ctx, c_ctx, w_mod, b_mod, norm_mix, norm_ffn, w_in, na_rpb, lru_conv_w, lru_conv_b, lru_w_a,
           lru_b_a, lru_w_x, lru_b_x, lru_lam, mla_q_norm, mla_kv_norm, mla_w_q_up, mla_w_kv_up, w_branch, w_out,
           ffn_w_up, ffn_conv_w, ffn_conv_b, ffn_w_down, norm_final):
    b, n, d = x.shape
    ctx_len = ctx.shape[1]
    depth = w_mod.shape[0]
    na_heads = na_rpb.shape[1]
    na_width = na_heads * NA_HEAD_DIM
    lru_width = lru_conv_w.shape[2]
    q_rank = mla_q_norm.shape[1]
    kv_rank = mla_kv_norm.shape[1]
    mla_heads = mla_w_q_up.shape[2] // MLA_QK_DIM
    bn, bc = b * n, b * ctx_len
    assert na_width == lru_width == mla_heads * MLA_V_DIM == w_branch.shape[2]
    assert n % (NA_KROWS * GRID_W) == 0 and n % ctx_len == 0

    off_lru_x = 3 * na_width
    off_lru_g = off_lru_x + lru_width
    off_mla = off_lru_g + lru_width
    mla_used = q_rank + kv_rank + MLA_ROPE_DIM
    mla_group = -(-mla_used // LANE) * LANE
    while off_mla % mla_group:
        mla_group += LANE
    off_gate = off_mla + mla_group
    dims = dict(b=b, n=n, c=ctx_len, d=d, bn=bn, na_heads=na_heads, lru_width=lru_width, q_rank=q_rank,
                kv_rank=kv_rank, mla_heads=mla_heads, off_lru_x=off_lru_x, off_lru_g=off_lru_g, off_mla=off_mla,
                mla_group=mla_group, off_gate=off_gate, tile_rows=int(np.gcd(n, bc)))

    w_in_b = w_in.astype(BF16)
    w_gate_b = w_in_b[:, :, off_mla + mla_used:]
    wq = mla_w_q_up.reshape(depth, q_rank, mla_heads, MLA_QK_DIM)
    wq_rope = wq[..., MLA_NOPE_DIM:]
    wq_p = jnp.concatenate([wq[..., :MLA_NOPE_DIM], wq_rope, _swap_halves(wq_rope)], axis=-1)
    wq_p = wq_p.reshape(depth, q_rank, mla_heads * MLA_HEAD_PAD).astype(BF16)
    wkv = mla_w_kv_up.reshape(depth, kv_rank, mla_heads, MLA_NOPE_DIM + MLA_V_DIM)
    wkv_p = jnp.concatenate([wkv[..., :MLA_NOPE_DIM].reshape(depth, kv_rank, -1),
                             wkv[..., MLA_NOPE_DIM:].reshape(depth, kv_rank, -1)], axis=-1).astype(BF16)
    w_branch_b = w_branch.astype(BF16)
    w_out_b = w_out.astype(BF16)
    w_up_b = ffn_w_up.astype(BF16)
    w_down_b = ffn_w_down.astype(BF16)
    lru_w_a_b = lru_w_a.astype(BF16)
    lru_w_x_b = lru_w_x.astype(BF16)
    ct, st = _rope_tables(n, b, ctx_len)
    na_bias = _na_bias_blocks(na_rpb)

    n_groups = 1 + b
    pad_rows = -n_groups % SUBLANE_F32
    cond = jnp.concatenate([c_ctx[None, :], c, jnp.zeros((pad_rows, d), F32)], axis=0)
    mod = _modulation(cond, w_mod, b_mod)[:, :n_groups].reshape(depth, n_groups, N_MOD, 1, d)

    r3 = lambda a: a.reshape(a.shape[0], 1, a.shape[1])
    r4 = lambda a: a.reshape(a.shape[0], a.shape[1], 1, a.shape[2])
    norm_mix3, norm_ffn3 = r3(norm_mix), r3(norm_ffn)
    lru_conv_b3 = r3(lru_conv_b)
    lru_b_a4, lru_b_x4, lru_lam4 = r4(lru_b_a), r4(lru_b_x), r4(lru_lam)
    q_norm3, kv_norm3 = r3(mla_q_norm), r3(mla_kv_norm)
    ffn_conv_b3 = r3(ffn_conv_b)
    norm_final2 = norm_final.reshape(1, d)

    xall = jnp.concatenate([x.reshape(bn, d), ctx.reshape(bc, d)], axis=0)
    for l in range(depth):
        last = l == depth - 1
        n_rows = bn if last else bn + bc
        sh1, sc1, g1, sh2, sc2, g2 = (mod[l, :, k] for k in range(N_MOD))
        z = _in_proj(xall, norm_mix3, sh1, sc1, w_in_b, w_gate_b, l, dims)
        out_a = _na_attention(z, na_bias, l, dims)
        lru_args = (lru_conv_w, lru_conv_b3, lru_w_a_b, lru_b_a4, lru_w_x_b, lru_b_x4, lru_lam4)
        lru_fwd = _lru_scan(z, l, 0, *lru_args, None, dims)
        out_b = _lru_scan(z, l, 1, *lru_args, lru_fwd, dims)
        q_m, k_m, v_m = _mla_proj(z, l, q_norm3, kv_norm3, wq_p, wkv_p, ct, st, dims)
        kt_m = k_m.reshape(bn + bc, mla_heads, MLA_HEAD_PAD).transpose(1, 2, 0)
        out_c = _mla_attention(q_m, kt_m, v_m, dims)
        if not last:
            out_a = _ctx_attention(z, z, z, out_a, dims, na_heads, NA_HEAD_DIM, 0, na_heads, 2 * na_heads, 1,
                                   NA_HEAD_DIM ** -0.5, "na_ctx_attention")
            out_c = _ctx_attention(q_m, k_m, v_m, out_c, dims, mla_heads, MLA_HEAD_PAD, 0, 0, 0, 2, 1.0,
                                   "mla_ctx_attention")
        y = _merge(out_a, out_b, out_c, w_branch_b, z, l, n_rows, dims)
        xall = _out_proj(y, w_out_b, xall, g1, l, dims)
        ffn_args = (norm_ffn3, sh2, sc2, g2, w_up_b, ffn_conv_w, ffn_conv_b3, w_down_b, norm_final2, l)
        x_new = _ffn(xall, None, *ffn_args, 0, bn, n, n_rows, last, dims)
        if not last:
            x_new = _ffn(xall, x_new, *ffn_args, bn, bc, ctx_len, n_rows, False, dims)
        xall = x_new
    return xall.reshape(b, n, d)
```

```python
import functools

import numpy as np
import jax
import jax.numpy as jnp
from jax import lax
from jax.experimental import pallas as pl
from jax.experimental.pallas import tpu as pltpu

F32 = jnp.float32
BF16 = jnp.bfloat16

GRID_W = 64
NORM_EPS = 1e-6
NEG_INF = -1e30
N_MOD = 6
NA_HEAD_DIM = 128
NA_WIN_H = 8
NA_WIN_W = 16
LRU_C = 8.0
MLA_NOPE_DIM = 128
MLA_ROPE_DIM = 64
MLA_V_DIM = 128
MLA_QK_DIM = MLA_NOPE_DIM + MLA_ROPE_DIM
ROPE_THETA = 10000.0

LANE = 128
SUBLANE_F32 = 8
SUBLANE_BF16 = 16
MLA_HEAD_PAD = 256
NA_QROWS = 8
NA_KROWS = 16
MIB = 1024 * 1024


def _cparams(semantics, vmem_mib):
    return pltpu.CompilerParams(dimension_semantics=semantics, vmem_limit_bytes=vmem_mib * MIB)


def _tile(n, target, mult):
    best = None
    for t in range(mult, min(n, target) + 1, mult):
        if n % t == 0:
            best = t
    assert best is not None, (n, target, mult)
    return best


def _group_of_tile(i, tm, bn, n):
    return jnp.where(i * tm >= bn, 0, 1 + (i * tm) // n)


def _rmsnorm_f32(x, gamma):
    ms = jnp.mean(x * x, axis=-1, keepdims=True)
    return x * lax.rsqrt(ms + NORM_EPS) * gamma


def _mod_kernel(c_ref, w_ref, b_ref, o_ref):
    cc = c_ref[...]
    s = cc * jax.nn.sigmoid(cc)
    o_ref[...] = jnp.dot(s, w_ref[...], preferred_element_type=F32,
                         precision=lax.Precision.HIGHEST) + b_ref[...]


def _modulation(cond, w_mod, b_mod):
    depth, d, nm = w_mod.shape
    rows = cond.shape[0]
    tn = _tile(nm, 1024, LANE)
    return pl.pallas_call(
        _mod_kernel,
        out_shape=jax.ShapeDtypeStruct((depth, rows, nm), F32),
        grid=(depth, nm // tn),
        in_specs=[
            pl.BlockSpec((rows, d), lambda l, j: (0, 0)),
            pl.BlockSpec((None, d, tn), lambda l, j: (l, 0, j)),
            pl.BlockSpec((None, 1, tn), lambda l, j: (l, 0, j)),
        ],
        out_specs=pl.BlockSpec((None, rows, tn), lambda l, j: (l, 0, j)),
        compiler_params=_cparams(("arbitrary", "arbitrary"), 40),
        name="adaln_mod",
    )(cond, w_mod, b_mod.reshape(depth, 1, nm))


def _in_proj_kernel(x_ref, gam_ref, sh_ref, sc_ref, w_ref, wg_ref, o_ref, h_ref, *, n_head_tiles):
    j = pl.program_id(1)

    @pl.when(j == 0)
    def _():
        y = _rmsnorm_f32(x_ref[...], gam_ref[...])
        h_ref[...] = (y * (1.0 + sc_ref[0]) + sh_ref[0]).astype(BF16)

    @pl.when(j < n_head_tiles)
    def _():
        o_ref[...] = jnp.dot(h_ref[...], w_ref[...], preferred_element_type=F32).astype(o_ref.dtype)

    @pl.when(j >= n_head_tiles)
    def _():
        o_ref[...] = jnp.dot(h_ref[...], wg_ref[...], preferred_element_type=F32).astype(o_ref.dtype)


def _in_proj(xall, gamma, shift, scale, w, w_gate, l, dims):
    tt, d = xall.shape
    n_head, n_gate = dims["off_gate"], w_gate.shape[2]
    tm = _tile(dims["tile_rows"], 1024, SUBLANE_BF16)
    tn = _tile(int(np.gcd(n_head, n_gate)), 1024, LANE)
    n_head_tiles = n_head // tn
    assert n_head <= w.shape[2]
    grp = functools.partial(_group_of_tile, tm=tm, bn=dims["bn"], n=dims["n"])
    return pl.pallas_call(
        functools.partial(_in_proj_kernel, n_head_tiles=n_head_tiles),
        out_shape=jax.ShapeDtypeStruct((tt, n_head + n_gate), BF16),
        grid=(tt // tm, (n_head + n_gate) // tn),
        in_specs=[
            pl.BlockSpec((tm, d), lambda i, j: (i, 0)),
            pl.BlockSpec((None, 1, d), lambda i, j: (l, 0, 0)),
            pl.BlockSpec((1, 1, d), lambda i, j: (grp(i), 0, 0)),
            pl.BlockSpec((1, 1, d), lambda i, j: (grp(i), 0, 0)),
            pl.BlockSpec((None, d, tn), lambda i, j: (l, 0, jnp.minimum(j, n_head_tiles - 1))),
            pl.BlockSpec((None, d, tn), lambda i, j: (l, 0, jnp.maximum(j - n_head_tiles, 0))),
        ],
        out_specs=pl.BlockSpec((tm, tn), lambda i, j: (i, j)),
        scratch_shapes=[pltpu.VMEM((tm, d), BF16)],
        compiler_params=_cparams(("parallel", "arbitrary"), 48),
        name="in_proj",
    )(xall, gamma, shift, scale, w, w_gate)


def _na_group_layout(g, grid_rows):
    kh = min(NA_WIN_H, grid_rows)
    k0 = int(np.clip(NA_QROWS * g - NA_WIN_H // 2, 0, grid_rows - NA_KROWS))
    layout = []
    for qi in range(NA_QROWS):
        r = NA_QROWS * g + qi
        first_key_row = int(np.clip(r - kh // 2, 0, grid_rows - kh))
        layout.append(tuple((k0 + kj) - r + NA_WIN_H - 1 if first_key_row <= k0 + kj < first_key_row + kh
                            else None for kj in range(NA_KROWS)))
    return k0, tuple(layout)


def _na_variants(grid_rows):
    layouts = [_na_group_layout(g, grid_rows)[1] for g in range(grid_rows // NA_QROWS)]
    distinct = sorted(set(layouts), key=layouts.index)
    return distinct, [distinct.index(lay) for lay in layouts]


def _na_kernel(q_ref, k_ref, v_ref, kc_ref, vc_ref, blk_ref, o_ref, bias_ref, *, grid_rows, scale):
    n_groups = grid_rows // NA_QROWS
    layouts, variant_of_group = _na_variants(grid_rows)
    tq = NA_QROWS * GRID_W
    nk = NA_KROWS * GRID_W
    nt = (((1,), (1,)), ((), ()))

    @pl.when(pl.program_id(1) == 0)
    def _():
        left_half = lax.broadcasted_iota(jnp.int32, (GRID_W, 2 * GRID_W), 1) < GRID_W
        masked = blk_ref.shape[0] - 1
        for var, layout in enumerate(layouts):
            for qi, row in enumerate(layout):
                for kp in range(NA_KROWS // 2):
                    r0, r1 = (masked if r is None else r for r in row[2 * kp:2 * kp + 2])
                    pair = blk_ref[r0] if r0 == r1 else jnp.where(left_half, blk_ref[r0], blk_ref[r1])
                    bias_ref[var, qi * GRID_W:(qi + 1) * GRID_W, 2 * kp * GRID_W:2 * (kp + 1) * GRID_W] = pair

    def key_start(g):
        return _na_group_layout(g, grid_rows)[0] * GRID_W

    def scores(g):
        q = q_ref[g * tq:(g + 1) * tq, :]
        k0 = key_start(g)
        var = variant_of_group[g]
        s = lax.dot_general(q, k_ref[k0:k0 + nk, :], nt, preferred_element_type=F32) * scale + bias_ref[var]
        sc = lax.dot_general(q, kc_ref[...], nt, preferred_element_type=F32) * scale
        return s, sc

    nxt = scores(0)
    for g in range(n_groups):
        s, sc = nxt
        if g + 1 < n_groups:
            nxt = scores(g + 1)
        k0 = key_start(g)
        m = jnp.maximum(jnp.max(s, axis=-1, keepdims=True), jnp.max(sc, axis=-1, keepdims=True))
        p = jnp.exp(s - m)
        pc = jnp.exp(sc - m)
        denom = jnp.sum(p, axis=-1, keepdims=True) + jnp.sum(pc, axis=-1, keepdims=True)
        o = jnp.dot(p.astype(BF16), v_ref[k0:k0 + nk, :], preferred_element_type=F32)
        o = o + jnp.dot(pc.astype(BF16), vc_ref[...], preferred_element_type=F32)
        o_ref[g * tq:(g + 1) * tq, :] = (o * (1.0 / denom)).astype(o_ref.dtype)


def _na_bias_blocks(rpb):
    lead = rpb.shape[:2]
    cidx = np.arange(GRID_W)
    c_start = np.clip(cidx - NA_WIN_W // 2, 0, GRID_W - NA_WIN_W)
    in_win = (cidx[None, :] >= c_start[:, None]) & (cidx[None, :] < c_start[:, None] + NA_WIN_W)
    col_idx = np.clip(cidx[None, :] - cidx[:, None], -(NA_WIN_W - 1), NA_WIN_W - 1) + (NA_WIN_W - 1)
    n_rel = 2 * NA_WIN_W - 1
    expand = (np.arange(n_rel)[:, None, None] == col_idx[None]) & in_win[None]
    expand = jnp.asarray(expand.reshape(n_rel, GRID_W * GRID_W), F32)
    blocks = jnp.einsum("lhrd,dx->lhrx", rpb.astype(F32), expand, precision=lax.Precision.HIGHEST)
    blocks = blocks + jnp.asarray(np.where(in_win, 0.0, NEG_INF).reshape(-1), F32)
    blocks = blocks.reshape(*lead, 2 * NA_WIN_H - 1, GRID_W, GRID_W)
    blocks = jnp.concatenate([blocks, jnp.full((*lead, 1, GRID_W, GRID_W), NEG_INF, F32)], axis=2)
    return jnp.concatenate([blocks, blocks], axis=-1)


def _na_attention(z, bias_blocks, l, dims):
    tt = z.shape[0]
    b, n, c = dims["b"], dims["n"], dims["c"]
    heads = dims["na_heads"]
    grid_rows = n // GRID_W
    n_var = len(_na_variants(grid_rows)[0])
    ctx_blk0 = dims["bn"] // c
    kern = functools.partial(_na_kernel, grid_rows=grid_rows, scale=NA_HEAD_DIM ** -0.5)
    return pl.pallas_call(
        kern,
        out_shape=jax.ShapeDtypeStruct((tt, heads * NA_HEAD_DIM), BF16),
        grid=(heads, b),
        in_specs=[
            pl.BlockSpec((n, NA_HEAD_DIM), lambda h, bi: (bi, h)),
            pl.BlockSpec((n, NA_HEAD_DIM), lambda h, bi: (bi, heads + h)),
            pl.BlockSpec((n, NA_HEAD_DIM), lambda h, bi: (bi, 2 * heads + h)),
            pl.BlockSpec((c, NA_HEAD_DIM), lambda h, bi: (ctx_blk0 + bi, heads + h)),
            pl.BlockSpec((c, NA_HEAD_DIM), lambda h, bi: (ctx_blk0 + bi, 2 * heads + h)),
            pl.BlockSpec((None, None) + bias_blocks.shape[2:], lambda h, bi: (l, h, 0, 0, 0)),
        ],
        out_specs=pl.BlockSpec((n, NA_HEAD_DIM), lambda h, bi: (bi, h)),
        scratch_shapes=[pltpu.VMEM((n_var, NA_QROWS * GRID_W, NA_KROWS * GRID_W), F32)],
        compiler_params=_cparams(("arbitrary", "arbitrary"), 48),
        name="na_attention",
    )(z, z, z, z, z, bias_blocks)


def _ctx_attn_kernel(q_ref, k_ref, v_ref, prev_ref, o_ref, *, scale):
    del prev_ref
    nt = (((1,), (1,)), ((), ()))
    s = lax.dot_general(q_ref[...], k_ref[...], nt, preferred_element_type=F32) * scale
    m = jnp.max(s, axis=-1, keepdims=True)
    p = jnp.exp(s - m)
    denom = jnp.sum(p, axis=-1, keepdims=True)
    o = jnp.dot(p.astype(BF16), v_ref[...], preferred_element_type=F32)
    o_ref[...] = (o * (1.0 / denom)).astype(o_ref.dtype)


def _ctx_attention(q_arr, k_arr, v_arr, out_buf, dims, heads, dq, q_col0, k_col0, v_col0, v_step, scale, name):
    b, c = dims["b"], dims["c"]
    blk0 = dims["bn"] // c
    dv = out_buf.shape[1] // heads
    return pl.pallas_call(
        functools.partial(_ctx_attn_kernel, scale=scale),
        out_shape=jax.ShapeDtypeStruct(out_buf.shape, out_buf.dtype),
        grid=(b, heads),
        in_specs=[
            pl.BlockSpec((c, dq), lambda bi, h: (blk0 + bi, q_col0 + h)),
            pl.BlockSpec((c, dq), lambda bi, h: (blk0 + bi, k_col0 + h)),
            pl.BlockSpec((c, dv), lambda bi, h: (blk0 + bi, v_col0 + v_step * h)),
            pl.BlockSpec(memory_space=pl.ANY),
        ],
        out_specs=pl.BlockSpec((c, dv), lambda bi, h: (blk0 + bi, h)),
        input_output_aliases={3: 0},
        compiler_params=_cparams(("parallel", "parallel"), 32),
        name=name,
    )(q_arr, k_arr, v_arr, out_buf)


def _lru_kernel(*refs, reverse, t_chunk, n_ctx_chunks, n_lat_chunks, n_blocks, block_dim):
    if reverse:
        u_ref, wa_ref, ba_ref, wx_ref, bx_ref, lam_ref, hf_ref, g_ref, o_ref, carry_ref = refs
    else:
        (x_ref, xp_ref, xn_ref, cw_ref, cb_ref, wa_ref, ba_ref, wx_ref, bx_ref, lam_ref,
         o_ref, u_ref, carry_ref) = refs
    j = pl.program_id(1)

    @pl.when(j == 0)
    def _():
        carry_ref[...] = jnp.zeros_like(carry_ref)

    if reverse:
        u = u_ref[...]
    else:
        is_ctx = j < n_ctx_chunks
        chunk = jnp.where(is_ctx, j, j - n_ctx_chunks)
        first = chunk == 0
        last = chunk == jnp.where(is_ctx, n_ctx_chunks, n_lat_chunks) - 1
        xm = x_ref[...].astype(F32)
        xp = xp_ref[...].astype(F32)
        xn = xn_ref[...].astype(F32)
        hp = xp.shape[0]
        pm1 = jnp.where(first, 0.0, xp[hp - 1:hp, :])
        pm2 = jnp.where(first, 0.0, xp[hp - 2:hp - 1, :])
        nn0 = jnp.where(last, 0.0, xn[0:1, :])
        t = lax.broadcasted_iota(jnp.int32, (t_chunk, 1), 0)
        xm1 = jnp.where(t == 0, pm1, pltpu.roll(xm, 1, 0))
        xm2 = jnp.where(t == 0, pm2, jnp.where(t == 1, pm1, pltpu.roll(xm, 2, 0)))
        xp1 = jnp.where(t == t_chunk - 1, nn0, pltpu.roll(xm, t_chunk - 1, 0))
        cw = cw_ref[...]
        u = cw[0:1] * xm2 + cw[1:2] * xm1 + cw[2:3] * xm + cw[3:4] * xp1 + cb_ref[...]
        u_ref[...] = u

    ub = u.astype(BF16)
    ga, gx = [], []
    for kb in range(n_blocks):
        blk = ub[:, kb * block_dim:(kb + 1) * block_dim]
        ga.append(jnp.dot(blk, wa_ref[kb], preferred_element_type=F32))
        gx.append(jnp.dot(blk, wx_ref[kb], preferred_element_type=F32))
    gate_a = jnp.concatenate(ga, axis=1) + ba_ref[...]
    gate_x = jnp.concatenate(gx, axis=1) + bx_ref[...]
    r = jax.nn.sigmoid(gate_a)
    i_gate = jax.nn.sigmoid(gate_x)
    nlam = -lam_ref[...]
    softplus = jnp.maximum(nlam, 0.0) + jnp.log1p(jnp.exp(-jnp.abs(nlam)))
    log_a = -LRU_C * r * softplus
    a = jnp.exp(log_a)
    bv = jnp.sqrt(-jnp.tanh(log_a) * (a * a + 1.0)) * (i_gate * u)

    grp = SUBLANE_F32
    sub = lax.broadcasted_iota(jnp.int32, (grp, 1), 0)
    order = range(t_chunk // grp - 1, -1, -1) if reverse else range(t_chunk // grp)
    carry = carry_ref[0:1, :]
    h_groups = {}
    for gi in order:
        a_g = a[gi * grp:(gi + 1) * grp, :]
        b_g = bv[gi * grp:(gi + 1) * grp, :]
        k = 1
        while k < grp:
            keep = (sub < grp - k) if reverse else (sub >= k)
            shift = grp - k if reverse else k
            b_g = jnp.where(keep, a_g * pltpu.roll(b_g, shift, 0) + b_g, b_g)
            a_g = jnp.where(keep, a_g * pltpu.roll(a_g, shift, 0), a_g)
            k *= 2
        h_g = a_g * carry + b_g
        carry = h_g[0:1, :] if reverse else h_g[grp - 1:grp, :]
        h_groups[gi] = h_g
    h = jnp.concatenate([h_groups[gi] for gi in range(t_chunk // grp)], axis=0)
    carry_ref[0:1, :] = carry
    if reverse:
        g = g_ref[...].astype(F32)
        o_ref[...] = (jax.nn.gelu(g) * (hf_ref[...] + h)).astype(o_ref.dtype)
    else:
        o_ref[...] = h


def _lru_scan(z, l, direction, conv_w, conv_b, w_a, b_a, w_x, b_x, lam, fwd_out, dims):
    tt = z.shape[0]
    b, n, c, lw = dims["b"], dims["n"], dims["c"], dims["lru_width"]
    n_blocks, block_dim = w_a.shape[2], w_a.shape[3]
    t_chunk = _tile(c, 256, SUBLANE_BF16)
    halo = SUBLANE_BF16
    n_ctx_chunks, n_lat_chunks = c // t_chunk, n // t_chunk
    ctx_chunk0 = dims["bn"] // t_chunk
    col_x = dims["off_lru_x"] // lw
    col_g = dims["off_lru_g"] // lw
    reverse = direction == 1
    per_halo = t_chunk // halo
    n_halo_blocks = tt // halo

    def chunk_block(bi, j):
        is_ctx = j < n_ctx_chunks
        pos = jnp.where(is_ctx, j, j - n_ctx_chunks)
        if reverse:
            pos = jnp.where(is_ctx, n_ctx_chunks - 1 - pos, n_lat_chunks - 1 - pos)
        return jnp.where(is_ctx, ctx_chunk0 + bi * n_ctx_chunks + pos, bi * n_lat_chunks + pos)

    def prev_block(bi, j):
        return jnp.maximum(chunk_block(bi, j) * per_halo - 1, 0)

    def next_block(bi, j):
        return jnp.minimum((chunk_block(bi, j) + 1) * per_halo, n_halo_blocks - 1)

    vec = lambda: pl.BlockSpec((None, None, 1, lw), lambda bi, j: (l, direction, 0, 0))
    gate_w = lambda: pl.BlockSpec((None, None, n_blocks, block_dim, block_dim),
                                  lambda bi, j: (l, direction, 0, 0, 0))
    chunk_spec = lambda col: pl.BlockSpec((t_chunk, lw), lambda bi, j: (chunk_block(bi, j), col))
    gate_specs = [gate_w(), vec(), gate_w(), vec(), vec()]
    gate_args = [w_a, b_a, w_x, b_x, lam]
    if reverse:
        h_fwd, u = fwd_out
        in_specs = [chunk_spec(0)] + gate_specs + [chunk_spec(0), chunk_spec(col_g)]
        args = [u] + gate_args + [h_fwd, z]
        out_shape = jax.ShapeDtypeStruct((tt, lw), BF16)
        out_specs = chunk_spec(0)
    else:
        in_specs = [
            chunk_spec(col_x),
            pl.BlockSpec((halo, lw), lambda bi, j: (prev_block(bi, j), col_x)),
            pl.BlockSpec((halo, lw), lambda bi, j: (next_block(bi, j), col_x)),
            pl.BlockSpec((None, conv_w.shape[1], lw), lambda bi, j: (l, 0, 0)),
            pl.BlockSpec((None, 1, lw), lambda bi, j: (l, 0, 0)),
        ] + gate_specs
        args = [z, z, z, conv_w, conv_b] + gate_args
        out_shape = (jax.ShapeDtypeStruct((tt, lw), F32), jax.ShapeDtypeStruct((tt, lw), F32))
        out_specs = (chunk_spec(0), chunk_spec(0))
    kern = functools.partial(_lru_kernel, reverse=reverse, t_chunk=t_chunk, n_ctx_chunks=n_ctx_chunks,
                             n_lat_chunks=n_lat_chunks, n_blocks=n_blocks, block_dim=block_dim)
    return pl.pallas_call(
        kern,
        out_shape=out_shape,
        grid=(b, n_ctx_chunks + n_lat_chunks),
        in_specs=in_specs,
        out_specs=out_specs,
        scratch_shapes=[pltpu.VMEM((SUBLANE_F32, lw), F32)],
        compiler_params=_cparams(("parallel", "arbitrary"), 40),
        name="lru_bwd" if reverse else "lru_fwd",
    )(*args)


def _mla_proj_kernel(z_ref, qn_ref, kvn_ref, wq_ref, wkv_ref, ct_ref, st_ref, q_ref, k_ref, v_ref,
                     *, q_rank, kv_rank, heads, q_scale):
    z = z_ref[...]
    cq = z[:, :q_rank].astype(F32)
    ckv = z[:, q_rank:q_rank + kv_rank].astype(F32)
    kr_tile = z[:, q_rank + kv_rank:q_rank + kv_rank + LANE].astype(F32)
    ct = ct_ref[...]
    st = st_ref[...]

    def rope(tile):
        return tile * ct + pltpu.roll(tile, LANE // 2, 1) * st

    quarter = MLA_ROPE_DIM // 2
    lane = lax.broadcasted_iota(jnp.int32, kr_tile.shape, 1)
    swapped = jnp.where(lane < MLA_ROPE_DIM + quarter, pltpu.roll(kr_tile, quarter, 1),
                        pltpu.roll(kr_tile, LANE - quarter, 1))
    krp = jnp.where(lane < MLA_ROPE_DIM, kr_tile, swapped)

    qn = _rmsnorm_f32(cq, qn_ref[...]).astype(BF16)
    q = jnp.dot(qn, wq_ref[...], preferred_element_type=F32)
    kvn = _rmsnorm_f32(ckv, kvn_ref[...]).astype(BF16)
    kv = jnp.dot(kvn, wkv_ref[...], preferred_element_type=F32)
    kr = rope(krp).astype(BF16)
    for h in range(heads):
        c0 = h * MLA_HEAD_PAD
        q_ref[:, c0:c0 + LANE] = (q[:, c0:c0 + LANE] * q_scale).astype(BF16)
        q_ref[:, c0 + LANE:c0 + 2 * LANE] = (rope(q[:, c0 + LANE:c0 + 2 * LANE]) * q_scale).astype(BF16)
        k_ref[:, c0:c0 + LANE] = kv[:, h * LANE:(h + 1) * LANE].astype(BF16)
        k_ref[:, c0 + LANE:c0 + 2 * LANE] = kr
        v_ref[:, c0:c0 + LANE] = kv[:, (heads + h) * LANE:(heads + h + 1) * LANE].astype(BF16)
        v_ref[:, c0 + LANE:c0 + 2 * LANE] = jnp.ones((z.shape[0], LANE), BF16)


def _mla_proj(z, l, q_norm, kv_norm, wq, wkv, ct, st, dims):
    tt = z.shape[0]
    heads, q_rank, kv_rank = dims["mla_heads"], dims["q_rank"], dims["kv_rank"]
    mg = dims["mla_group"]
    tm = _tile(dims["tile_rows"], 512, SUBLANE_BF16)
    col = dims["off_mla"] // mg
    kern = functools.partial(_mla_proj_kernel, q_rank=q_rank, kv_rank=kv_rank, heads=heads,
                             q_scale=MLA_QK_DIM ** -0.5)
    return pl.pallas_call(
        kern,
        out_shape=(jax.ShapeDtypeStruct((tt, heads * MLA_HEAD_PAD), BF16),
                   jax.ShapeDtypeStruct((tt, heads * MLA_HEAD_PAD), BF16),
                   jax.ShapeDtypeStruct((tt, heads * MLA_HEAD_PAD), BF16)),
        grid=(tt // tm,),
        in_specs=[
            pl.BlockSpec((tm, mg), lambda i: (i, col)),
            pl.BlockSpec((None, 1, q_rank), lambda i: (l, 0, 0)),
            pl.BlockSpec((None, 1, kv_rank), lambda i: (l, 0, 0)),
            pl.BlockSpec((None, q_rank, heads * MLA_HEAD_PAD), lambda i: (l, 0, 0)),
            pl.BlockSpec((None, kv_rank, heads * 2 * LANE), lambda i: (l, 0, 0)),
            pl.BlockSpec((tm, LANE), lambda i: (i, 0)),
            pl.BlockSpec((tm, LANE), lambda i: (i, 0)),
        ],
        out_specs=(pl.BlockSpec((tm, heads * MLA_HEAD_PAD), lambda i: (i, 0)),
                   pl.BlockSpec((tm, heads * MLA_HEAD_PAD), lambda i: (i, 0)),
                   pl.BlockSpec((tm, heads * MLA_HEAD_PAD), lambda i: (i, 0))),
        compiler_params=_cparams(("parallel",), 48),
        name="mla_proj",
    )(z, q_norm, kv_norm, wq, wkv, ct, st)


def _mla_flash_kernel(q_ref, kt_ref, v_ref, ktc_ref, vc_ref, o_ref, *, sub, tk):
    n_chunks = kt_ref.shape[1] // tk
    chunks = [None] + list(range(n_chunks))

    def scores(q, ch):
        kt = ktc_ref[...] if ch is None else kt_ref[:, ch * tk:(ch + 1) * tk]
        return jnp.dot(q, kt, preferred_element_type=F32)

    def values(ch):
        return vc_ref[...] if ch is None else v_ref[ch * tk:(ch + 1) * tk, :]

    for a in range(q_ref.shape[0] // sub):
        rows = slice(a * sub, (a + 1) * sub)
        q = q_ref[rows, :]
        s_next = scores(q, chunks[0])
        m = acc = None
        for idx, ch in enumerate(chunks):
            s = s_next
            if idx + 1 < len(chunks):
                s_next = scores(q, chunks[idx + 1])
            m_new = jnp.max(s, axis=-1, keepdims=True)
            if m is not None:
                m_new = jnp.maximum(m, m_new)
            p = jnp.exp(s - m_new).astype(BF16)
            pv = jnp.dot(p, values(ch), preferred_element_type=F32)
            acc = pv if m is None else jnp.exp(m - m_new) * acc + pv
            m = m_new
        o_ref[rows, :] = (acc[:, :MLA_V_DIM] * (1.0 / acc[:, MLA_V_DIM:])).astype(o_ref.dtype)


def _mla_attention(q, kt, v, dims):
    tt = q.shape[0]
    b, n, c, heads = dims["b"], dims["n"], dims["c"], dims["mla_heads"]
    tq = _tile(n, 4096, SUBLANE_BF16)
    sub = _tile(tq, 512, SUBLANE_BF16)
    ctx_blk0 = dims["bn"] // c
    nq = n // tq
    vw = 2 * MLA_V_DIM
    return pl.pallas_call(
        functools.partial(_mla_flash_kernel, sub=sub, tk=_tile(n, 2048, LANE)),
        out_shape=jax.ShapeDtypeStruct((tt, heads * MLA_V_DIM), BF16),
        grid=(b, heads, nq),
        in_specs=[
            pl.BlockSpec((tq, MLA_HEAD_PAD), lambda bi, h, i: (bi * nq + i, h)),
            pl.BlockSpec((None, MLA_HEAD_PAD, n), lambda bi, h, i: (h, 0, bi)),
            pl.BlockSpec((n, vw), lambda bi, h, i: (bi, h)),
            pl.BlockSpec((None, MLA_HEAD_PAD, c), lambda bi, h, i: (h, 0, ctx_blk0 + bi)),
            pl.BlockSpec((c, vw), lambda bi, h, i: (ctx_blk0 + bi, h)),
        ],
        out_specs=pl.BlockSpec((tq, MLA_V_DIM), lambda bi, h, i: (bi * nq + i, h)),
        compiler_params=_cparams(("parallel", "parallel", "arbitrary"), 48),
        name="mla_attention",
    )(q, kt, v, kt, v)


def _merge_kernel(ba_ref, bb_ref, bc_ref, wb_ref, ga_ref, gb_ref, gc_ref, o_ref):
    y = None
    for br, gate, i in ((ba_ref, ga_ref, 0), (bb_ref, gb_ref, 1), (bc_ref, gc_ref, 2)):
        t = jnp.dot(br[...], wb_ref[i], preferred_element_type=F32)
        t = jax.nn.sigmoid(gate[...].astype(F32)) * t
        y = t if y is None else y + t
    o_ref[...] = y.astype(o_ref.dtype)


def _merge(br_a, br_b, br_c, w_branch, z, l, n_rows, dims):
    d = dims["d"]
    bw = br_a.shape[1]
    tm = _tile(dims["tile_rows"], 1024, SUBLANE_BF16)
    tn = _tile(d, 1024, LANE)
    gate_col0 = dims["off_gate"] // tn
    per_gate = d // tn
    branch = lambda: pl.BlockSpec((tm, bw), lambda i, j: (i, 0))
    gate = lambda k: pl.BlockSpec((tm, tn), lambda i, j: (i, gate_col0 + k * per_gate + j))
    return pl.pallas_call(
        _merge_kernel,
        out_shape=jax.ShapeDtypeStruct((n_rows, d), BF16),
        grid=(n_rows // tm, d // tn),
        in_specs=[branch(), branch(), branch(),
                  pl.BlockSpec((None, 3, bw, tn), lambda i, j: (l, 0, 0, j)),
                  gate(0), gate(1), gate(2)],
        out_specs=pl.BlockSpec((tm, tn), lambda i, j: (i, j)),
        compiler_params=_cparams(("parallel", "arbitrary"), 56),
        name="merge",
    )(br_a, br_b, br_c, w_branch, z, z, z)


def _out_proj_kernel(y_ref, w_ref, x_ref, g_ref, o_ref):
    t = jnp.dot(y_ref[...], w_ref[...], preferred_element_type=F32)
    o_ref[...] = x_ref[...] + g_ref[0] * t


def _out_proj(y, w_out, xall, gate, l, dims):
    n_rows, d = y.shape
    tm = _tile(dims["tile_rows"], 1024, SUBLANE_BF16)
    tn = _tile(d, 1024, LANE)
    grp =functools.partial(_group_of_tile, tm=tm, bn=dims["bn"], n=dims["n"])
    return pl.pallas_call(
        _out_proj_kernel,
        out_shape=jax.ShapeDtypeStruct((n_rows, d), F32),
        grid=(n_rows // tm, d // tn),
        in_specs=[
            pl.BlockSpec((tm, d), lambda i, j: (i, 0)),
            pl.BlockSpec((None, d, tn), lambda i, j: (l, 0, j)),
            pl.BlockSpec((tm, tn), lambda i, j: (i, j)),
            pl.BlockSpec((1, 1, tn), lambda i, j: (grp(i), 0, j)),
        ],
        out_specs=pl.BlockSpec((tm, tn), lambda i, j: (i, j)),
        compiler_params=_cparams(("parallel", "arbitrary"), 48),
        name="out_proj",
    )(y, w_out, xall, gate)


def _ffn_kernel(x_ref, xp_ref, xn_ref, gam_ref, sh_ref, sc_ref, g2_ref, wv_ref, wg_ref, cwv_ref, cwg_ref,
                cbv_ref, cbg_ref, wd_ref, nf_ref, *rest, tm, tiles_per_seq, final_norm):
    o_ref, h_ref = rest[-2:]
    i = pl.program_id(0)
    f = pl.program_id(1)
    halo = SUBLANE_F32

    def norm_mod(x):
        y = _rmsnorm_f32(x, gam_ref[...])
        return y * (1.0 + sc_ref[0]) + sh_ref[0]

    @pl.when(f == 0)
    def _():
        h_ref[0:tm, :] = norm_mod(x_ref[...]).astype(BF16)
        hh = norm_mod(jnp.concatenate([xn_ref[...], xp_ref[...]], axis=0))
        pos = i % tiles_per_seq
        keep_next = (pos != tiles_per_seq - 1).astype(F32)
        keep_prev = (pos != 0).astype(F32)
        is_next = lax.broadcasted_iota(jnp.int32, (2 * halo, 1), 0) < halo
        h_ref[tm:tm + 2 * halo, :] = (hh * jnp.where(is_next, keep_next, keep_prev)).astype(BF16)
        o_ref[...] = jnp.zeros_like(o_ref)

    hv = h_ref[...]
    rows = tm + 2 * halo

    def conv(w_ref, cw_ref, cb_ref):
        zz = jnp.dot(hv, w_ref[...], preferred_element_type=F32)
        cw = cw_ref[...]
        zm1 = pltpu.roll(zz, 1, 0)[0:tm]
        zp1 = pltpu.roll(zz, rows - 1, 0)[0:tm]
        return cw[0:1] * zm1 + cw[1:2] * zz[0:tm] + cw[2:3] * zp1 + cb_ref[...]

    val = conv(wv_ref, cwv_ref, cbv_ref)
    gate = conv(wg_ref, cwg_ref, cbg_ref)
    act = (gate * jax.nn.sigmoid(gate) * val).astype(BF16)
    o_ref[...] += jnp.dot(act, wd_ref[...], preferred_element_type=F32)

    @pl.when(f == pl.num_programs(1) - 1)
    def _():
        xo = x_ref[...] + g2_ref[0] * o_ref[...]
        if final_norm:
            xo = _rmsnorm_f32(xo, nf_ref[...])
        o_ref[...] = xo


def _ffn(xall, out_buf, gamma, shift, scale, gate2, w_up, conv_w, conv_b, w_down, norm_final, l, row0, n_rows,
         seq_len, out_rows, final_norm, dims):
    tt, d = xall.shape
    ff = w_down.shape[1]
    tm = _tile(seq_len, 1024, SUBLANE_BF16)
    tf = _tile(ff, 512, LANE)
    nf = ff // tf
    halo = SUBLANE_F32
    per_halo = tm // halo
    n_halo_blocks = tt // halo
    tile0 = row0 // tm
    assert row0 % tm == 0 and tm % halo == 0
    grp = lambda i: _group_of_tile(tile0 + i, tm=tm, bn=dims["bn"], n=dims["n"])
    mod = lambda: pl.BlockSpec((1, 1, d), lambda i, f: (grp(i), 0, 0))
    kern = functools.partial(_ffn_kernel, tm=tm, tiles_per_seq=seq_len // tm, final_norm=final_norm)
    args = [xall, xall, xall, gamma, shift, scale, gate2, w_up, w_up, conv_w, conv_w, conv_b, conv_b, w_down,
            norm_final]
    extra_specs, aliases = [], {}
    if out_buf is not None:
        aliases = {len(args): 0}
        args.append(out_buf)
        extra_specs = [pl.BlockSpec(memory_space=pl.ANY)]
    return pl.pallas_call(
        kern,
        out_shape=jax.ShapeDtypeStruct((out_rows, d), F32),
        grid=(n_rows // tm, nf),
        in_specs=[
            pl.BlockSpec((tm, d), lambda i, f: (tile0 + i, 0), pipeline_mode=pl.Buffered(1)),
            pl.BlockSpec((halo, d), lambda i, f: (jnp.maximum((tile0 + i) * per_halo - 1, 0), 0)),
            pl.BlockSpec((halo, d), lambda i, f: (jnp.minimum((tile0 + i + 1) * per_halo, n_halo_blocks - 1), 0)),
            pl.BlockSpec((None, 1, d), lambda i, f: (l, 0, 0)),
            mod(), mod(), mod(),
            pl.BlockSpec((None, d, tf), lambda i, f: (l, 0, f)),
            pl.BlockSpec((None, d, tf), lambda i, f: (l, 0, nf + f)),
            pl.BlockSpec((None, conv_w.shape[1], tf), lambda i, f: (l, 0, f)),
            pl.BlockSpec((None, conv_w.shape[1], tf), lambda i, f: (l, 0, nf + f)),
            pl.BlockSpec((None, 1, tf), lambda i, f: (l, 0, f)),
            pl.BlockSpec((None, 1, tf), lambda i, f: (l, 0, nf + f)),
            pl.BlockSpec((None, tf, d), lambda i, f: (l, f, 0)),
            pl.BlockSpec((1, d), lambda i, f: (0, 0)),
        ] + extra_specs,
        out_specs=pl.BlockSpec((tm, d), lambda i, f: (tile0 + i, 0)),
        scratch_shapes=[pltpu.VMEM((tm + 2 * halo, d), BF16)],
        input_output_aliases=aliases,
        compiler_params=_cparams(("parallel", "arbitrary"), 58),
        name="conv_ffn",
    )(*args)


def _rope_tables(n, b, c):
    t = jnp.arange(n, dtype=jnp.int32)
    row = (t // GRID_W).astype(F32)
    col = (t % GRID_W).astype(F32)
    n_freq = MLA_ROPE_DIM // 4
    inv_freq = ROPE_THETA ** (-jnp.arange(n_freq, dtype=F32) / n_freq)
    ang = jnp.concatenate([row[:, None] * inv_freq, col[:, None] * inv_freq], axis=-1)
    cos, sin = jnp.cos(ang), jnp.sin(ang)
    zeros = jnp.zeros((n, LANE - MLA_ROPE_DIM), F32)
    ct = jnp.concatenate([cos, cos, zeros], axis=-1)
    st = jnp.concatenate([-sin, sin, zeros], axis=-1)
    ct_ctx = jnp.concatenate([jnp.ones((b * c, MLA_ROPE_DIM), F32), jnp.zeros((b * c, LANE - MLA_ROPE_DIM), F32)], -1)
    ct = jnp.concatenate([jnp.tile(ct, (b, 1)), ct_ctx], axis=0)
    st = jnp.concatenate([jnp.tile(st, (b, 1)), jnp.zeros((b * c, LANE), F32)], axis=0)
    return ct, st


def _swap_halves(w):
    half = w.shape[-1] // 2
    return jnp.concatenate([w[..., half:], w[..., :half]], axis=-1)


def kernel(x, c, ctx, c_ctx, w_mod, b_mod, norm_mix, norm_ffn, w_in, na_rpb, lru_conv_w, lru_conv_b, lru_w_a,
           lru_b_a, lru_w_x, lru_b_x, lru_lam, mla_q_norm, mla_kv_norm, mla_w_q_up, mla_w_kv_up, w_branch, w_out,
           ffn_w_up, ffn_conv_w, ffn_conv_b, ffn_w_down, norm_final):
    b, n, d = x.shape
    ctx_len = ctx.shape[1]
    depth = w_mod.shape[0]
    na_heads = na_rpb.shape[1]
    na_width = na_heads * NA_HEAD_DIM
    lru_width = lru_conv_w.shape[2]
    q_rank = mla_q_norm.shape[1]
    kv_rank = mla_kv_norm.shape[1]
    mla_heads = mla_w_q_up.shape[2] // MLA_QK_DIM
    bn, bc = b * n, b * ctx_len
    assert na_width == lru_width == mla_heads * MLA_V_DIM == w_branch.shape[2]
    assert n % (NA_KROWS * GRID_W) == 0 and n % ctx_len == 0

    off_lru_x = 3 * na_width
    off_lru_g = off_lru_x + lru_width
    off_mla = off_lru_g + lru_width
    mla_used = q_rank + kv_rank + MLA_ROPE_DIM
    mla_group = -(-mla_used // LANE) * LANE
    while off_mla % mla_group:
        mla_group += LANE
    off_gate = off_mla + mla_group
    dims = dict(b=b, n=n, c=ctx_len, d=d, bn=bn, na_heads=na_heads, lru_width=lru_width, q_rank=q_rank,
                kv_rank=kv_rank, mla_heads=mla_heads, off_lru_x=off_lru_x, off_lru_g=off_lru_g, off_mla=off_mla,
                mla_group=mla_group, off_gate=off_gate, tile_rows=int(np.gcd(n, bc)))

    w_in_b = w_in.astype(BF16)
    w_gate_b = w_in_b[:, :, off_mla + mla_used:]
    wq = mla_w_q_up.reshape(depth, q_rank, mla_heads, MLA_QK_DIM)
    wq_rope = wq[..., MLA_NOPE_DIM:]
    wq_p = jnp.concatenate([wq[..., :MLA_NOPE_DIM], wq_rope, _swap_halves(wq_rope)], axis=-1)
    wq_p = wq_p.reshape(depth, q_rank, mla_heads * MLA_HEAD_PAD).astype(BF16)
    wkv = mla_w_kv_up.reshape(depth, kv_rank, mla_heads, MLA_NOPE_DIM + MLA_V_DIM)
    wkv_p = jnp.concatenate([wkv[..., :MLA_NOPE_DIM].reshape(depth, kv_rank, -1),
                             wkv[..., MLA_NOPE_DIM:].reshape(depth, kv_rank, -1)], axis=-1).astype(BF16)
    w_branch_b = w_branch.astype(BF16)
    w_out_b = w_out.astype(BF16)
    w_up_b = ffn_w_up.astype(BF16)
    w_down_b = ffn_w_down.astype(BF16)
    lru_w_a_b = lru_w_a.astype(BF16)
    lru_w_x_b = lru_w_x.astype(BF16)
    ct, st = _rope_tables(n, b, ctx_len)
    na_bias = _na_bias_blocks(na_rpb)

    n_groups = 1 + b
    pad_rows = -n_groups % SUBLANE_F32
    cond = jnp.concatenate([c_ctx[None, :], c, jnp.zeros((pad_rows, d), F32)], axis=0)
    mod = _modulation(cond, w_mod, b_mod)[:, :n_groups].reshape(depth, n_groups, N_MOD, 1, d)

    r3 = lambda a: a.reshape(a.shape[0], 1, a.shape[1])
    r4 = lambda a: a.reshape(a.shape[0], a.shape[1], 1, a.shape[2])
    norm_mix3, norm_ffn3 = r3(norm_mix), r3(norm_ffn)
    lru_conv_b3 = r3(lru_conv_b)
    lru_b_a4, lru_b_x4, lru_lam4 = r4(lru_b_a), r4(lru_b_x), r4(lru_lam)
    q_norm3, kv_norm3 = r3(mla_q_norm), r3(mla_kv_norm)
    ffn_conv_b3 = r3(ffn_conv_b)
    norm_final2 = norm_final.reshape(1, d)

    xall = jnp.concatenate([x.reshape(bn, d), ctx.reshape(bc, d)], axis=0)
    for l in range(depth):
        last = l == depth - 1
        n_rows = bn if last else bn + bc
        sh1, sc1, g1, sh2, sc2, g2 = (mod[l, :, k] for k in range(N_MOD))
        z = _in_proj(xall, norm_mix3, sh1, sc1, w_in_b, w_gate_b, l, dims)
        out_a = _na_attention(z, na_bias, l, dims)
        lru_args = (lru_conv_w, lru_conv_b3, lru_w_a_b, lru_b_a4, lru_w_x_b, lru_b_x4, lru_lam4)
        lru_fwd = _lru_scan(z, l, 0, *lru_args, None, dims)
        out_b = _lru_scan(z, l, 1, *lru_args, lru_fwd, dims)
        q_m, k_m, v_m = _mla_proj(z, l, q_norm3, kv_norm3, wq_p, wkv_p, ct, st, dims)
        kt_m = k_m.reshape(bn + bc, mla_heads, MLA_HEAD_PAD).transpose(1, 2, 0)
        out_c = _mla_attention(q_m, kt_m, v_m, dims)
        if not last:
            out_a = _ctx_attention(z, z, z, out_a, dims, na_heads, NA_HEAD_DIM, 0, na_heads, 2 * na_heads, 1,
                                   NA_HEAD_DIM ** -0.5, "na_ctx_attention")
            out_c = _ctx_attention(q_m, k_m, v_m, out_c, dims, mla_heads, MLA_HEAD_PAD, 0, 0, 0, 2, 1.0,
                                   "mla_ctx_attention")
        y = _merge(out_a, out_b, out_c, w_branch_b, z, l, n_rows, dims)
        xall = _out_proj(y, w_out_b, xall, g1, l, dims)
        ffn_args = (norm_ffn3, sh2, sc2, g2, w_up_b, ffn_conv_w, ffn_conv_b3, w_down_b, norm_final2, l)
        x_new = _ffn(xall, None, *ffn_args, 0, bn, n, n_rows, last, dims)
        if not last:
            x_new = _ffn(xall, x_new, *ffn_args, bn, bc, ctx_len, n_rows, False, dims)
        xall = x_new
    return xall.reshape(b, n, d)
```

```python
import functools

import numpy as np
import jax
import jax.numpy as jnp
from jax import lax
from jax.experimental import pallas as pl
from jax.experimental.pallas import tpu as pltpu

F32 = jnp.float32
BF16 = jnp.bfloat16

GRID_W = 64
NORM_EPS = 1e-6
NEG_INF = -1e30
N_MOD = 6
NA_HEAD_DIM = 128
NA_WIN_H = 8
NA_WIN_W = 16
LRU_C = 8.0
MLA_NOPE_DIM = 128
MLA_ROPE_DIM = 64
MLA_V_DIM = 128
MLA_QK_DIM = MLA_NOPE_DIM + MLA_ROPE_DIM
ROPE_THETA = 10000.0

LANE = 128
SUBLANE_F32 = 8
SUBLANE_BF16 = 16
MLA_HEAD_PAD = 256
NA_QROWS = 4
NA_KROWS = 12
MIB = 1024 * 1024


def _cparams(semantics, vmem_mib):
    return pltpu.CompilerParams(dimension_semantics=semantics, vmem_limit_bytes=vmem_mib * MIB)


def _tile(n, target, mult):
    best = None
    for t in range(mult, min(n, target) + 1, mult):
        if n % t == 0:
            best = t
    assert best is not None, (n, target, mult)
    return best


def _group_of_tile(i, tm, bn, n):
    return jnp.where(i * tm >= bn, 0, 1 + (i * tm) // n)


def _rmsnorm_f32(x, gamma):
    ms = jnp.mean(x * x, axis=-1, keepdims=True)
    return x * lax.rsqrt(ms + NORM_EPS) * gamma


def _mod_kernel(c_ref, w_ref, b_ref, o_ref):
    cc = c_ref[...]
    s = cc * jax.nn.sigmoid(cc)
    o_ref[...] = jnp.dot(s, w_ref[...], preferred_element_type=F32,
                         precision=lax.Precision.HIGHEST) + b_ref[...]


def _modulation(cond, w_mod, b_mod):
    depth, d, nm = w_mod.shape
    rows = cond.shape[0]
    tn = _tile(nm, 1024, LANE)
    return pl.pallas_call(
        _mod_kernel,
        out_shape=jax.ShapeDtypeStruct((depth, rows, nm), F32),
        grid=(depth, nm // tn),
        in_specs=[
            pl.BlockSpec((rows, d), lambda l, j: (0, 0)),
            pl.BlockSpec((None, d, tn), lambda l, j: (l, 0, j)),
            pl.BlockSpec((None, 1, tn), lambda l, j: (l, 0, j)),
        ],
        out_specs=pl.BlockSpec((None, rows, tn), lambda l, j: (l, 0, j)),
        compiler_params=_cparams(("arbitrary", "arbitrary"), 40),
        name="adaln_mod",
    )(cond, w_mod, b_mod.reshape(depth, 1, nm))


def _in_proj_kernel(x_ref, gam_ref, sh_ref, sc_ref, w_ref, wg_ref, o_ref, h_ref, *, n_head_tiles):
    j = pl.program_id(1)

    @pl.when(j == 0)
    def _():
        y = _rmsnorm_f32(x_ref[...], gam_ref[...])
        h_ref[...] = (y * (1.0 + sc_ref[0]) + sh_ref[0]).astype(BF16)

    @pl.when(j < n_head_tiles)
    def _():
        o_ref[...] = jnp.dot(h_ref[...], w_ref[...], preferred_element_type=F32).astype(o_ref.dtype)

    @pl.when(j >= n_head_tiles)
    def _():
        o_ref[...] = jnp.dot(h_ref[...], wg_ref[...], preferred_element_type=F32).astype(o_ref.dtype)


def _in_proj(xall, gamma, shift, scale, w, w_gate, l, dims):
    tt, d = xall.shape
    n_head, n_gate = dims["off_gate"], w_gate.shape[2]
    tm = _tile(dims["tile_rows"], 1024, SUBLANE_BF16)
    tn = _tile(int(np.gcd(n_head, n_gate)), 1024, LANE)
    n_head_tiles = n_head // tn
    assert n_head <= w.shape[2]
    grp = functools.partial(_group_of_tile, tm=tm, bn=dims["bn"], n=dims["n"])
    return pl.pallas_call(
        functools.partial(_in_proj_kernel, n_head_tiles=n_head_tiles),
        out_shape=jax.ShapeDtypeStruct((tt, n_head + n_gate), BF16),
        grid=(tt // tm, (n_head + n_gate) // tn),
        in_specs=[
            pl.BlockSpec((tm, d), lambda i, j: (i, 0)),
            pl.BlockSpec((None, 1, d), lambda i, j: (l, 0, 0)),
            pl.BlockSpec((1, 1, d), lambda i, j: (grp(i), 0, 0)),
            pl.BlockSpec((1, 1, d), lambda i, j: (grp(i), 0, 0)),
            pl.BlockSpec((None, d, tn), lambda i, j: (l, 0, jnp.minimum(j, n_head_tiles - 1))),
            pl.BlockSpec((None, d, tn), lambda i, j: (l, 0, jnp.maximum(j - n_head_tiles, 0))),
        ],
        out_specs=pl.BlockSpec((tm, tn), lambda i, j: (i, j)),
        scratch_shapes=[pltpu.VMEM((tm, d), BF16)],
        compiler_params=_cparams(("parallel", "arbitrary"), 48),
        name="in_proj",
    )(xall, gamma, shift, scale, w, w_gate)


def _na_group_layout(g, grid_rows):
    kh = min(NA_WIN_H, grid_rows)
    k0 = int(np.clip(NA_QROWS * g - NA_WIN_H // 2, 0, grid_rows - NA_KROWS))
    layout = []
    for qi in range(NA_QROWS):
        r = NA_QROWS * g + qi
        first_key_row = int(np.clip(r - kh // 2, 0, grid_rows - kh))
        layout.append(tuple((k0 + kj) - r + NA_WIN_H - 1 if first_key_row <= k0 + kj < first_key_row + kh
                            else None for kj in range(NA_KROWS)))
    return k0, tuple(layout)


def _na_variants(grid_rows):
    layouts = [_na_group_layout(g, grid_rows)[1] for g in range(grid_rows // NA_QROWS)]
    distinct = sorted(set(layouts), key=layouts.index)
    return distinct, [distinct.index(lay) for lay in layouts]


def _na_kernel(q_ref, k_ref, v_ref, kc_ref, vc_ref, blk_ref, o_ref, bias_ref, *, grid_rows, scale):
    n_groups = grid_rows // NA_QROWS
    layouts, variant_of_group = _na_variants(grid_rows)
    tq = NA_QROWS * GRID_W
    nk = NA_KROWS * GRID_W
    nt = (((1,), (1,)), ((), ()))

    @pl.when(pl.program_id(1) == 0)
    def _():
        left_half = lax.broadcasted_iota(jnp.int32, (GRID_W, 2 * GRID_W), 1) < GRID_W
        masked = blk_ref.shape[0] - 1
        for var, layout in enumerate(layouts):
            for qi, row in enumerate(layout):
                for kp in range(NA_KROWS // 2):
                    r0, r1 = (masked if r is None else r for r in row[2 * kp:2 * kp + 2])
                    pair = blk_ref[r0] if r0 == r1 else jnp.where(left_half, blk_ref[r0], blk_ref[r1])
                    bias_ref[var, qi * GRID_W:(qi + 1) * GRID_W, 2 * kp * GRID_W:2 * (kp + 1) * GRID_W] = pair

    def key_start(g):
        return _na_group_layout(g, grid_rows)[0] * GRID_W

    def scores(g):
        q = q_ref[g * tq:(g + 1) * tq, :]
        k0 = key_start(g)
        var = variant_of_group[g]
        s = lax.dot_general(q, k_ref[k0:k0 + nk, :], nt, preferred_element_type=F32) * scale + bias_ref[var]
        sc = lax.dot_general(q, kc_ref[...], nt, preferred_element_type=F32) * scale
        return s, sc

    nxt = scores(0)
    for g in range(n_groups):
        s, sc = nxt
        if g + 1 < n_groups:
            nxt = scores(g + 1)
        k0 = key_start(g)
        m = jnp.maximum(jnp.max(s, axis=-1, keepdims=True), jnp.max(sc, axis=-1, keepdims=True))
        p = jnp.exp(s - m)
        pc = jnp.exp(sc - m)
        denom = jnp.sum(p, axis=-1, keepdims=True) + jnp.sum(pc, axis=-1, keepdims=True)
        o = jnp.dot(p.astype(BF16), v_ref[k0:k0 + nk, :], preferred_element_type=F32)
        o = o + jnp.dot(pc.astype(BF16), vc_ref[...], preferred_element_type=F32)
        o_ref[g * tq:(g + 1) * tq, :] = (o * (1.0 / denom)).astype(o_ref.dtype)


def _na_bias_blocks(rpb):
    lead = rpb.shape[:2]
    cidx = np.arange(GRID_W)
    c_start = np.clip(cidx - NA_WIN_W // 2, 0, GRID_W - NA_WIN_W)
    in_win = (cidx[None, :] >= c_start[:, None]) & (cidx[None, :] < c_start[:, None] + NA_WIN_W)
    col_idx = np.clip(cidx[None, :] - cidx[:, None], -(NA_WIN_W - 1), NA_WIN_W - 1) + (NA_WIN_W - 1)
    n_rel = 2 * NA_WIN_W - 1
    expand = (np.arange(n_rel)[:, None, None] == col_idx[None]) & in_win[None]
    expand = jnp.asarray(expand.reshape(n_rel, GRID_W * GRID_W), F32)
    blocks = jnp.einsum("lhrd,dx->lhrx", rpb.astype(F32), expand, precision=lax.Precision.HIGHEST)
    blocks = blocks + jnp.asarray(np.where(in_win, 0.0, NEG_INF).reshape(-1), F32)
    blocks = blocks.reshape(*lead, 2 * NA_WIN_H - 1, GRID_W, GRID_W)
    blocks = jnp.concatenate([blocks, jnp.full((*lead, 1, GRID_W, GRID_W), NEG_INF, F32)], axis=2)
    return jnp.concatenate([blocks, blocks], axis=-1)


def _na_attention(z, bias_blocks, l, dims):
    tt = z.shape[0]
    b, n, c = dims["b"], dims["n"], dims["c"]
    heads = dims["na_heads"]
    grid_rows = n // GRID_W
    n_var = len(_na_variants(grid_rows)[0])
    ctx_blk0 = dims["bn"] // c
    kern = functools.partial(_na_kernel, grid_rows=grid_rows, scale=NA_HEAD_DIM ** -0.5)
    return pl.pallas_call(
        kern,
        out_shape=jax.ShapeDtypeStruct((tt, heads * NA_HEAD_DIM), BF16),
        grid=(heads, b),
        in_specs=[
            pl.BlockSpec((n, NA_HEAD_DIM), lambda h, bi: (bi, h)),
            pl.BlockSpec((n, NA_HEAD_DIM), lambda h, bi: (bi, heads + h)),
            pl.BlockSpec((n, NA_HEAD_DIM), lambda h, bi: (bi, 2 * heads + h)),
            pl.BlockSpec((c, NA_HEAD_DIM), lambda h, bi: (ctx_blk0 + bi, heads + h)),
            pl.BlockSpec((c, NA_HEAD_DIM), lambda h, bi: (ctx_blk0 + bi, 2 * heads + h)),
            pl.BlockSpec((None, None) + bias_blocks.shape[2:], lambda h, bi: (l, h, 0, 0, 0)),
        ],
        out_specs=pl.BlockSpec((n, NA_HEAD_DIM), lambda h, bi: (bi, h)),
        scratch_shapes=[pltpu.VMEM((n_var, NA_QROWS * GRID_W, NA_KROWS * GRID_W), F32)],
        compiler_params=_cparams(("arbitrary", "arbitrary"), 48),
        name="na_attention",
    )(z, z, z, z, z, bias_blocks)


def _ctx_attn_kernel(q_ref, k_ref, v_ref, prev_ref, o_ref, *, scale):
    del prev_ref
    nt = (((1,), (1,)), ((), ()))
    s = lax.dot_general(q_ref[...], k_ref[...], nt, preferred_element_type=F32) * scale
    m = jnp.max(s, axis=-1, keepdims=True)
    p = jnp.exp(s - m)
    denom = jnp.sum(p, axis=-1, keepdims=True)
    o = jnp.dot(p.astype(BF16), v_ref[...], preferred_element_type=F32)
    o_ref[...] = (o * (1.0 / denom)).astype(o_ref.dtype)


def _ctx_attention(q_arr, k_arr, v_arr, out_buf, dims, heads, dq, q_col0, k_col0, v_col0, v_step, scale, name):
    b, c = dims["b"], dims["c"]
    blk0 = dims["bn"] // c
    dv = out_buf.shape[1] // heads
    return pl.pallas_call(
        functools.partial(_ctx_attn_kernel, scale=scale),
        out_shape=jax.ShapeDtypeStruct(out_buf.shape, out_buf.dtype),
        grid=(b, heads),
        in_specs=[
            pl.BlockSpec((c, dq), lambda bi, h: (blk0 + bi, q_col0 + h)),
            pl.BlockSpec((c, dq), lambda bi, h: (blk0 + bi, k_col0 + h)),
            pl.BlockSpec((c, dv), lambda bi, h: (blk0 + bi, v_col0 + v_step * h)),
            pl.BlockSpec(memory_space=pl.ANY),
        ],
        out_specs=pl.BlockSpec((c, dv), lambda bi, h: (blk0 + bi, h)),
        input_output_aliases={3: 0},
        compiler_params=_cparams(("parallel", "parallel"), 32),
        name=name,
    )(q_arr, k_arr, v_arr, out_buf)


def _lru_kernel(*refs, reverse, t_chunk, n_ctx_chunks, n_lat_chunks, n_blocks, block_dim):
    if reverse:
        u_ref, wa_ref, ba_ref, wx_ref, bx_ref, lam_ref, hf_ref, g_ref, o_ref, carry_ref = refs
    else:
        (x_ref, xp_ref, xn_ref, cw_ref, cb_ref, wa_ref, ba_ref, wx_ref, bx_ref, lam_ref,
         o_ref, u_ref, carry_ref) = refs
    j = pl.program_id(1)

    @pl.when(j == 0)
    def _():
        carry_ref[...] = jnp.zeros_like(carry_ref)

    if reverse:
        u = u_ref[...]
    else:
        is_ctx = j < n_ctx_chunks
        chunk = jnp.where(is_ctx, j, j - n_ctx_chunks)
        first = chunk == 0
        last = chunk == jnp.where(is_ctx, n_ctx_chunks, n_lat_chunks) - 1
        xm = x_ref[...].astype(F32)
        xp = xp_ref[...].astype(F32)
        xn = xn_ref[...].astype(F32)
        hp = xp.shape[0]
        pm1 = jnp.where(first, 0.0, xp[hp - 1:hp, :])
        pm2 = jnp.where(first, 0.0, xp[hp - 2:hp - 1, :])
        nn0 = jnp.where(last, 0.0, xn[0:1, :])
        t = lax.broadcasted_iota(jnp.int32, (t_chunk, 1), 0)
        xm1 = jnp.where(t == 0, pm1, pltpu.roll(xm, 1, 0))
        xm2 = jnp.where(t == 0, pm2, jnp.where(t == 1, pm1, pltpu.roll(xm, 2, 0)))
        xp1 = jnp.where(t == t_chunk - 1, nn0, pltpu.roll(xm, t_chunk - 1, 0))
        cw = cw_ref[...]
        u = cw[0:1] * xm2 + cw[1:2] * xm1 + cw[2:3] * xm + cw[3:4] * xp1 + cb_ref[...]
        u_ref[...] = u

    ub = u.astype(BF16)
    ga, gx = [], []
    for kb in range(n_blocks):
        blk = ub[:, kb * block_dim:(kb + 1) * block_dim]
        ga.append(jnp.dot(blk, wa_ref[kb], preferred_element_type=F32))
        gx.append(jnp.dot(blk, wx_ref[kb], preferred_element_type=F32))
    gate_a = jnp.concatenate(ga, axis=1) + ba_ref[...]
    gate_x = jnp.concatenate(gx, axis=1) + bx_ref[...]
    r = jax.nn.sigmoid(gate_a)
    i_gate = jax.nn.sigmoid(gate_x)
    nlam = -lam_ref[...]
    softplus = jnp.maximum(nlam, 0.0) + jnp.log1p(jnp.exp(-jnp.abs(nlam)))
    log_a = -LRU_C * r * softplus
    a = jnp.exp(log_a)
    bv = jnp.sqrt(-jnp.tanh(log_a) * (a * a + 1.0)) * (i_gate * u)

    grp = SUBLANE_F32
    sub = lax.broadcasted_iota(jnp.int32, (grp, 1), 0)
    order = range(t_chunk // grp - 1, -1, -1) if reverse else range(t_chunk // grp)
    carry = carry_ref[0:1, :]
    h_groups = {}
    for gi in order:
        a_g = a[gi * grp:(gi + 1) * grp, :]
        b_g = bv[gi * grp:(gi + 1) * grp, :]
        k = 1
        while k < grp:
            keep = (sub < grp - k) if reverse else (sub >= k)
            shift = grp - k if reverse else k
            b_g = jnp.where(keep, a_g * pltpu.roll(b_g, shift, 0) + b_g, b_g)
            a_g = jnp.where(keep, a_g * pltpu.roll(a_g, shift, 0), a_g)
            k *= 2
        h_g = a_g * carry + b_g
        carry = h_g[0:1, :] if reverse else h_g[grp - 1:grp, :]
        h_groups[gi] = h_g
    h = jnp.concatenate([h_groups[gi] for gi in range(t_chunk // grp)], axis=0)
    carry_ref[0:1, :] = carry
    if reverse:
        g = g_ref[...].astype(F32)
        o_ref[...] = (jax.nn.gelu(g) * (hf_ref[...] + h)).astype(o_ref.dtype)
    else:
        o_ref[...] = h


def _lru_scan(z, l, direction, conv_w, conv_b, w_a, b_a, w_x, b_x, lam, fwd_out, dims):
    tt = z.shape[0]
    b, n, c, lw = dims["b"], dims["n"], dims["c"], dims["lru_width"]
    n_blocks, block_dim = w_a.shape[2], w_a.shape[3]
    t_chunk = _tile(c, 256, SUBLANE_BF16)
    halo = SUBLANE_BF16
    n_ctx_chunks, n_lat_chunks = c // t_chunk, n // t_chunk
    ctx_chunk0 = dims["bn"] // t_chunk
    col_x = dims["off_lru_x"] // lw
    col_g = dims["off_lru_g"] // lw
    reverse = direction == 1
    per_halo = t_chunk // halo
    n_halo_blocks = tt // halo

    def chunk_block(bi, j):
        is_ctx = j < n_ctx_chunks
        pos = jnp.where(is_ctx, j, j - n_ctx_chunks)
        if reverse:
            pos = jnp.where(is_ctx, n_ctx_chunks - 1 - pos, n_lat_chunks - 1 - pos)
        return jnp.where(is_ctx, ctx_chunk0 + bi * n_ctx_chunks + pos, bi * n_lat_chunks + pos)

    def prev_block(bi, j):
        return jnp.maximum(chunk_block(bi, j) * per_halo - 1, 0)

    def next_block(bi, j):
        return jnp.minimum((chunk_block(bi, j) + 1) * per_halo, n_halo_blocks - 1)

    vec = lambda: pl.BlockSpec((None, None, 1, lw), lambda bi, j: (l, direction, 0, 0))
    gate_w = lambda: pl.BlockSpec((None, None, n_blocks, block_dim, block_dim),
                                  lambda bi, j: (l, direction, 0, 0, 0))
    chunk_spec = lambda col: pl.BlockSpec((t_chunk, lw), lambda bi, j: (chunk_block(bi, j), col))
    gate_specs = [gate_w(), vec(), gate_w(), vec(), vec()]
    gate_args = [w_a, b_a, w_x, b_x, lam]
    if reverse:
        h_fwd, u = fwd_out
        in_specs = [chunk_spec(0)] + gate_specs + [chunk_spec(0), chunk_spec(col_g)]
        args = [u] + gate_args + [h_fwd, z]
        out_shape = jax.ShapeDtypeStruct((tt, lw), BF16)
        out_specs = chunk_spec(0)
    else:
        in_specs = [
            chunk_spec(col_x),
            pl.BlockSpec((halo, lw), lambda bi, j: (prev_block(bi, j), col_x)),
            pl.BlockSpec((halo, lw), lambda bi, j: (next_block(bi, j), col_x)),
            pl.BlockSpec((None, conv_w.shape[1], lw), lambda bi, j: (l, 0, 0)),
            pl.BlockSpec((None, 1, lw), lambda bi, j: (l, 0, 0)),
        ] + gate_specs
        args = [z, z, z, conv_w, conv_b] + gate_args
        out_shape = (jax.ShapeDtypeStruct((tt, lw), F32), jax.ShapeDtypeStruct((tt, lw), F32))
        out_specs = (chunk_spec(0), chunk_spec(0))
    kern = functools.partial(_lru_kernel, reverse=reverse, t_chunk=t_chunk, n_ctx_chunks=n_ctx_chunks,
                             n_lat_chunks=n_lat_chunks, n_blocks=n_blocks, block_dim=block_dim)
    return pl.pallas_call(
        kern,
        out_shape=out_shape,
        grid=(b, n_ctx_chunks + n_lat_chunks),
        in_specs=in_specs,
        out_specs=out_specs,
        scratch_shapes=[pltpu.VMEM((SUBLANE_F32, lw), F32)],
        compiler_params=_cparams(("parallel", "arbitrary"), 40),
        name="lru_bwd" if reverse else "lru_fwd",
    )(*args)


def _mla_proj_kernel(z_ref, qn_ref, kvn_ref, wq_ref, wkv_ref, ct_ref, st_ref, q_ref, k_ref, v_ref,
                     *, q_rank, kv_rank, heads, q_scale):
    z = z_ref[...]
    cq = z[:, :q_rank].astype(F32)
    ckv = z[:, q_rank:q_rank + kv_rank].astype(F32)
    kr_tile = z[:, q_rank + kv_rank:q_rank + kv_rank + LANE].astype(F32)
    ct = ct_ref[...]
    st = st_ref[...]

    def rope(tile):
        return tile * ct + pltpu.roll(tile, LANE // 2, 1) * st

    quarter = MLA_ROPE_DIM // 2
    lane = lax.broadcasted_iota(jnp.int32, kr_tile.shape, 1)
    swapped = jnp.where(lane < MLA_ROPE_DIM + quarter, pltpu.roll(kr_tile, quarter, 1),
                        pltpu.roll(kr_tile, LANE - quarter, 1))
    krp = jnp.where(lane < MLA_ROPE_DIM, kr_tile, swapped)

    qn = _rmsnorm_f32(cq, qn_ref[...]).astype(BF16)
    q = jnp.dot(qn, wq_ref[...], preferred_element_type=F32)
    kvn = _rmsnorm_f32(ckv, kvn_ref[...]).astype(BF16)
    kv = jnp.dot(kvn, wkv_ref[...], preferred_element_type=F32)
    kr = rope(krp).astype(BF16)
    for h in range(heads):
        c0 = h * MLA_HEAD_PAD
        q_ref[:, c0:c0 + LANE] = (q[:, c0:c0 + LANE] * q_scale).astype(BF16)
        q_ref[:, c0 + LANE:c0 + 2 * LANE] = (rope(q[:, c0 + LANE:c0 + 2 * LANE]) * q_scale).astype(BF16)
        k_ref[:, c0:c0 + LANE] = kv[:, h * LANE:(h + 1) * LANE].astype(BF16)
        k_ref[:, c0 + LANE:c0 + 2 * LANE] = kr
        v_ref[:, c0:c0 + LANE] = kv[:, (heads + h) * LANE:(heads + h + 1) * LANE].astype(BF16)
        v_ref[:, c0 + LANE:c0 + 2 * LANE] = jnp.ones((z.shape[0], LANE), BF16)


def _mla_proj(z, l, q_norm, kv_norm, wq, wkv, ct, st, dims):
    tt = z.shape[0]
    heads, q_rank, kv_rank = dims["mla_heads"], dims["q_rank"], dims["kv_rank"]
    mg = dims["mla_group"]
    tm = _tile(dims["tile_rows"], 512, SUBLANE_BF16)
    col = dims["off_mla"] // mg
    kern = functools.partial(_mla_proj_kernel, q_rank=q_rank, kv_rank=kv_rank, heads=heads,
                             q_scale=MLA_QK_DIM ** -0.5)
    return pl.pallas_call(
        kern,
        out_shape=(jax.ShapeDtypeStruct((tt, heads * MLA_HEAD_PAD), BF16),
                   jax.ShapeDtypeStruct((tt, heads * MLA_HEAD_PAD), BF16),
                   jax.ShapeDtypeStruct((tt, heads * MLA_HEAD_PAD), BF16)),
        grid=(tt // tm,),
        in_specs=[
            pl.BlockSpec((tm, mg), lambda i: (i, col)),
            pl.BlockSpec((None, 1, q_rank), lambda i: (l, 0, 0)),
            pl.BlockSpec((None, 1, kv_rank), lambda i: (l, 0, 0)),
            pl.BlockSpec((None, q_rank, heads * MLA_HEAD_PAD), lambda i: (l, 0, 0)),
            pl.BlockSpec((None, kv_rank, heads * 2 * LANE), lambda i: (l, 0, 0)),
            pl.BlockSpec((tm, LANE), lambda i: (i, 0)),
            pl.BlockSpec((tm, LANE), lambda i: (i, 0)),
        ],
        out_specs=(pl.BlockSpec((tm, heads * MLA_HEAD_PAD), lambda i: (i, 0)),
                   pl.BlockSpec((tm, heads * MLA_HEAD_PAD), lambda i: (i, 0)),
                   pl.BlockSpec((tm, heads * MLA_HEAD_PAD), lambda i: (i, 0))),
        compiler_params=_cparams(("parallel",), 48),
        name="mla_proj",
    )(z, q_norm, kv_norm, wq, wkv, ct, st)


def _mla_flash_kernel(q_ref, kt_ref, v_ref, ktc_ref, vc_ref, o_ref, *, sub, tk):
    n_chunks = kt_ref.shape[1] // tk
    chunks = [None] + list(range(n_chunks))

    def scores(q, ch):
        kt = ktc_ref[...] if ch is None else kt_ref[:, ch * tk:(ch + 1) * tk]
        return jnp.dot(q, kt, preferred_element_type=F32)

    def values(ch):
        return vc_ref[...] if ch is None else v_ref[ch * tk:(ch + 1) * tk, :]

    for a in range(q_ref.shape[0] // sub):
        rows = slice(a * sub, (a + 1) * sub)
        q = q_ref[rows, :]
        s_next = scores(q, chunks[0])
        m = acc = None
        for idx, ch in enumerate(chunks):
            s = s_next
            if idx + 1 < len(chunks):
                s_next = scores(q, chunks[idx + 1])
            m_new = jnp.max(s, axis=-1, keepdims=True)
            if m is not None:
                m_new = jnp.maximum(m, m_new)
            p = jnp.exp(s - m_new).astype(BF16)
            pv = jnp.dot(p, values(ch), preferred_element_type=F32)
            acc = pv if m is None else jnp.exp(m - m_new) * acc + pv
            m = m_new
        o_ref[rows, :] = (acc[:, :MLA_V_DIM] * (1.0 / acc[:, MLA_V_DIM:])).astype(o_ref.dtype)


def _mla_attention(q, kt, v, dims):
    tt = q.shape[0]
    b, n, c, heads = dims["b"], dims["n"], dims["c"], dims["mla_heads"]
    tq = _tile(n, 4096, SUBLANE_BF16)
    sub = _tile(tq, 512, SUBLANE_BF16)
    ctx_blk0 = dims["bn"] // c
    nq = n // tq
    vw = 2 * MLA_V_DIM
    return pl.pallas_call(
        functools.partial(_mla_flash_kernel, sub=sub, tk=_tile(n, 2048, LANE)),
        out_shape=jax.ShapeDtypeStruct((tt, heads * MLA_V_DIM), BF16),
        grid=(b, heads, nq),
        in_specs=[
            pl.BlockSpec((tq, MLA_HEAD_PAD), lambda bi, h, i: (bi * nq + i, h)),
            pl.BlockSpec((None, MLA_HEAD_PAD, n), lambda bi, h, i: (h, 0, bi)),
            pl.BlockSpec((n, vw), lambda bi, h, i: (bi, h)),
            pl.BlockSpec((None, MLA_HEAD_PAD, c), lambda bi, h, i: (h, 0, ctx_blk0 + bi)),
            pl.BlockSpec((c, vw), lambda bi, h, i: (ctx_blk0 + bi, h)),
        ],
        out_specs=pl.BlockSpec((tq, MLA_V_DIM), lambda bi, h, i: (bi * nq + i, h)),
        compiler_params=_cparams(("parallel", "parallel", "arbitrary"), 48),
        name="mla_attention",
    )(q, kt, v, kt, v)


def _merge_kernel(ba_ref, bb_ref, bc_ref, wb_ref, ga_ref, gb_ref, gc_ref, o_ref):
    y = None
    for br, gate, i in ((ba_ref, ga_ref, 0), (bb_ref, gb_ref, 1), (bc_ref, gc_ref, 2)):
        t = jnp.dot(br[...], wb_ref[i], preferred_element_type=F32)
        t = jax.nn.sigmoid(gate[...].astype(F32)) * t
        y = t if y is None else y + t
    o_ref[...] = y.astype(o_ref.dtype)


def _merge(br_a, br_b, br_c, w_branch, z, l, n_rows, dims):
    d = dims["d"]
    bw = br_a.shape[1]
    tm = _tile(dims["tile_rows"], 1024, SUBLANE_BF16)
    tn = _tile(d, 1024, LANE)
    gate_col0 = dims["off_gate"] // tn
    per_gate = d // tn
    branch = lambda: pl.BlockSpec((tm, bw), lambda i, j: (i, 0))
    gate = lambda k: pl.BlockSpec((tm, tn), lambda i, j: (i, gate_col0 + k * per_gate + j))
    return pl.pallas_call(
        _merge_kernel,
        out_shape=jax.ShapeDtypeStruct((n_rows, d), BF16),
        grid=(n_rows // tm, d // tn),
        in_specs=[branch(), branch(), branch(),
                  pl.BlockSpec((None, 3, bw, tn), lambda i, j: (l, 0, 0, j)),
                  gate(0), gate(1), gate(2)],
        out_specs=pl.BlockSpec((tm, tn), lambda i, j: (i, j)),
        compiler_params=_cparams(("parallel", "arbitrary"), 56),
        name="merge",
    )(br_a, br_b, br_c, w_branch, z, z, z)


def _out_proj_kernel(y_ref, w_ref, x_ref, g_ref, o_ref):
    t = jnp.dot(y_ref[...], w_ref[...], preferred_element_type=F32)
    o_ref[...] = x_ref[...] + g_ref[0] * t


def _out_proj(y, w_out, xall, gate, l, dims):
    n_rows, d = y.shape
    tm = _tile(dims["tile_rows"], 1024, SUBLANE_BF16)
    tn = _tile(d, 1024, LANE)
    grp =functools.partial(_group_of_tile, tm=tm, bn=dims["bn"], n=dims["n"])
    return pl.pallas_call(
        _out_proj_kernel,
        out_shape=jax.ShapeDtypeStruct((n_rows, d), F32),
        grid=(n_rows // tm, d // tn),
        in_specs=[
            pl.BlockSpec((tm, d), lambda i, j: (i, 0)),
            pl.BlockSpec((None, d, tn), lambda i, j: (l, 0, j)),
            pl.BlockSpec((tm, tn), lambda i, j: (i, j)),
            pl.BlockSpec((1, 1, tn), lambda i, j: (grp(i), 0, j)),
        ],
        out_specs=pl.BlockSpec((tm, tn), lambda i, j: (i, j)),
        compiler_params=_cparams(("parallel", "arbitrary"), 48),
        name="out_proj",
    )(y, w_out, xall, gate)


def _ffn_kernel(x_ref, xp_ref, xn_ref, gam_ref, sh_ref, sc_ref, g2_ref, wv_ref, wg_ref, cwv_ref, cwg_ref,
                cbv_ref, cbg_ref, wd_ref, nf_ref, *rest, tm, tiles_per_seq, final_norm):
    o_ref, h_ref = rest[-2:]
    i = pl.program_id(0)
    f = pl.program_id(1)
    halo = SUBLANE_F32

    def norm_mod(x):
        y = _rmsnorm_f32(x, gam_ref[...])
        return y * (1.0 + sc_ref[0]) + sh_ref[0]

    @pl.when(f == 0)
    def _():
        h_ref[0:tm, :] = norm_mod(x_ref[...]).astype(BF16)
        hh = norm_mod(jnp.concatenate([xn_ref[...], xp_ref[...]], axis=0))
        pos = i % tiles_per_seq
        keep_next = (pos != tiles_per_seq - 1).astype(F32)
        keep_prev = (pos != 0).astype(F32)
        is_next = lax.broadcasted_iota(jnp.int32, (2 * halo, 1), 0) < halo
        h_ref[tm:tm + 2 * halo, :] = (hh * jnp.where(is_next, keep_next, keep_prev)).astype(BF16)
        o_ref[...] = jnp.zeros_like(o_ref)

    hv = h_ref[...]
    rows = tm + 2 * halo

    def conv(w_ref, cw_ref, cb_ref):
        zz = jnp.dot(hv, w_ref[...], preferred_element_type=F32)
        cw = cw_ref[...]
        zm1 = pltpu.roll(zz, 1, 0)[0:tm]
        zp1 = pltpu.roll(zz, rows - 1, 0)[0:tm]
        return cw[0:1] * zm1 + cw[1:2] * zz[0:tm] + cw[2:3] * zp1 + cb_ref[...]

    val = conv(wv_ref, cwv_ref, cbv_ref)
    gate = conv(wg_ref, cwg_ref, cbg_ref)
    act = (gate * jax.nn.sigmoid(gate) * val).astype(BF16)
    o_ref[...] += jnp.dot(act, wd_ref[...], preferred_element_type=F32)

    @pl.when(f == pl.num_programs(1) - 1)
    def _():
        xo = x_ref[...] + g2_ref[0] * o_ref[...]
        if final_norm:
            xo = _rmsnorm_f32(xo, nf_ref[...])
        o_ref[...] = xo


def _ffn(xall, out_buf, gamma, shift, scale, gate2, w_up, conv_w, conv_b, w_down, norm_final, l, row0, n_rows,
         seq_len, out_rows, final_norm, dims):
    tt, d = xall.shape
    ff = w_down.shape[1]
    tm = _tile(seq_len, 1024, SUBLANE_BF16)
    tf = _tile(ff, 512, LANE)
    nf = ff // tf
    halo = SUBLANE_F32
    per_halo = tm // halo
    n_halo_blocks = tt // halo
    tile0 = row0 // tm
    assert row0 % tm == 0 and tm % halo == 0
    grp = lambda i: _group_of_tile(tile0 + i, tm=tm, bn=dims["bn"], n=dims["n"])
    mod = lambda: pl.BlockSpec((1, 1, d), lambda i, f: (grp(i), 0, 0))
    kern = functools.partial(_ffn_kernel, tm=tm, tiles_per_seq=seq_len // tm, final_norm=final_norm)
    args = [xall, xall, xall, gamma, shift, scale, gate2, w_up, w_up, conv_w, conv_w, conv_b, conv_b, w_down,
            norm_final]
    extra_specs, aliases = [], {}
    if out_buf is not None:
        aliases = {len(args): 0}
        args.append(out_buf)
        extra_specs = [pl.BlockSpec(memory_space=pl.ANY)]
    return pl.pallas_call(
        kern,
        out_shape=jax.ShapeDtypeStruct((out_rows, d), F32),
        grid=(n_rows // tm, nf),
        in_specs=[
            pl.BlockSpec((tm, d), lambda i, f: (tile0 + i, 0), pipeline_mode=pl.Buffered(1)),
            pl.BlockSpec((halo, d), lambda i, f: (jnp.maximum((tile0 + i) * per_halo - 1, 0), 0)),
            pl.BlockSpec((halo, d), lambda i, f: (jnp.minimum((tile0 + i + 1) * per_halo, n_halo_blocks - 1), 0)),
            pl.BlockSpec((None, 1, d), lambda i, f: (l, 0, 0)),
            mod(), mod(), mod(),
            pl.BlockSpec((None, d, tf), lambda i, f: (l, 0, f)),
            pl.BlockSpec((None, d, tf), lambda i, f: (l, 0, nf + f)),
            pl.BlockSpec((None, conv_w.shape[1], tf), lambda i, f: (l, 0, f)),
            pl.BlockSpec((None, conv_w.shape[1], tf), lambda i, f: (l, 0, nf + f)),
            pl.BlockSpec((None, 1, tf), lambda i, f: (l, 0, f)),
            pl.BlockSpec((None, 1, tf), lambda i, f: (l, 0, nf + f)),
            pl.BlockSpec((None, tf, d), lambda i, f: (l, f, 0)),
            pl.BlockSpec((1, d), lambda i, f: (0, 0)),
        ] + extra_specs,
        out_specs=pl.BlockSpec((tm, d), lambda i, f: (tile0 + i, 0)),
        scratch_shapes=[pltpu.VMEM((tm + 2 * halo, d), BF16)],
        input_output_aliases=aliases,
        compiler_params=_cparams(("parallel", "arbitrary"), 58),
        name="conv_ffn",
    )(*args)


def _rope_tables(n, b, c):
    t = jnp.arange(n, dtype=jnp.int32)
    row = (t // GRID_W).astype(F32)
    col = (t % GRID_W).astype(F32)
    n_freq = MLA_ROPE_DIM // 4
    inv_freq = ROPE_THETA ** (-jnp.arange(n_freq, dtype=F32) / n_freq)
    ang = jnp.concatenate([row[:, None] * inv_freq, col[:, None] * inv_freq], axis=-1)
    cos, sin = jnp.cos(ang), jnp.sin(ang)
    zeros = jnp.zeros((n, LANE - MLA_ROPE_DIM), F32)
    ct = jnp.concatenate([cos, cos, zeros], axis=-1)
    st = jnp.concatenate([-sin, sin, zeros], axis=-1)
    ct_ctx = jnp.concatenate([jnp.ones((b * c, MLA_ROPE_DIM), F32), jnp.zeros((b * c, LANE - MLA_ROPE_DIM), F32)], -1)
    ct = jnp.concatenate([jnp.tile(ct, (b, 1)), ct_ctx], axis=0)
    st = jnp.concatenate([jnp.tile(st, (b, 1)), jnp.zeros((b * c, LANE), F32)], axis=0)
    return ct, st


def _swap_halves(w):
    half = w.shape[-1] // 2
    return jnp.concatenate([w[..., half:], w[..., :half]], axis=-1)


def kernel(x, c, ctx, c_ctx, w_mod, b_mod, norm_mix, norm_ffn, w_in, na_rpb, lru_conv_w, lru_conv_b, lru_w_a,
           lru_b_a, lru_w_x, lru_b_x, lru_lam, mla_q_norm, mla_kv_norm, mla_w_q_up, mla_w_kv_up, w_branch, w_out,
           ffn_w_up, ffn_conv_w, ffn_conv_b, ffn_w_down, norm_final):
    b, n, d = x.shape
    ctx_len = ctx.shape[1]
    depth = w_mod.shape[0]
    na_heads = na_rpb.shape[1]
    na_width = na_heads * NA_HEAD_DIM
    lru_width = lru_conv_w.shape[2]
    q_rank = mla_q_norm.shape[1]
    kv_rank = mla_kv_norm.shape[1]
    mla_heads = mla_w_q_up.shape[2] // MLA_QK_DIM
    bn, bc = b * n, b * ctx_len
    assert na_width == lru_width == mla_heads * MLA_V_DIM == w_branch.shape[2]
    assert (n // GRID_W) % NA_QROWS == 0 and n // GRID_W >= NA_KROWS and n % ctx_len == 0

    off_lru_x = 3 * na_width
    off_lru_g = off_lru_x + lru_width
    off_mla = off_lru_g + lru_width
    mla_used = q_rank + kv_rank + MLA_ROPE_DIM
    mla_group = -(-mla_used // LANE) * LANE
    while off_mla % mla_group:
        mla_group += LANE
    off_gate = off_mla + mla_group
    dims = dict(b=b, n=n, c=ctx_len, d=d, bn=bn, na_heads=na_heads, lru_width=lru_width, q_rank=q_rank,
                kv_rank=kv_rank, mla_heads=mla_heads, off_lru_x=off_lru_x, off_lru_g=off_lru_g, off_mla=off_mla,
                mla_group=mla_group, off_gate=off_gate, tile_rows=int(np.gcd(n, bc)))

    w_in_b = w_in.astype(BF16)
    w_gate_b = w_in_b[:, :, off_mla + mla_used:]
    wq = mla_w_q_up.reshape(depth, q_rank, mla_heads, MLA_QK_DIM)
    wq_rope = wq[..., MLA_NOPE_DIM:]
    wq_p = jnp.concatenate([wq[..., :MLA_NOPE_DIM], wq_rope, _swap_halves(wq_rope)], axis=-1)
    wq_p = wq_p.reshape(depth, q_rank, mla_heads * MLA_HEAD_PAD).astype(BF16)
    wkv = mla_w_kv_up.reshape(depth, kv_rank, mla_heads, MLA_NOPE_DIM + MLA_V_DIM)
    wkv_p = jnp.concatenate([wkv[..., :MLA_NOPE_DIM].reshape(depth, kv_rank, -1),
                             wkv[..., MLA_NOPE_DIM:].reshape(depth, kv_rank, -1)], axis=-1).astype(BF16)
    w_branch_b = w_branch.astype(BF16)
    w_out_b = w_out.astype(BF16)
    w_up_b = ffn_w_up.astype(BF16)
    w_down_b = ffn_w_down.astype(BF16)
    lru_w_a_b = lru_w_a.astype(BF16)
    lru_w_x_b = lru_w_x.astype(BF16)
    ct, st = _rope_tables(n, b, ctx_len)
    na_bias = _na_bias_blocks(na_rpb)

    n_groups = 1 + b
    pad_rows = -n_groups % SUBLANE_F32
    cond = jnp.concatenate([c_ctx[None, :], c, jnp.zeros((pad_rows, d), F32)], axis=0)
    mod = _modulation(cond, w_mod, b_mod)[:, :n_groups].reshape(depth, n_groups, N_MOD, 1, d)

    r3 = lambda a: a.reshape(a.shape[0], 1, a.shape[1])
    r4 = lambda a: a.reshape(a.shape[0], a.shape[1], 1, a.shape[2])
    norm_mix3, norm_ffn3 = r3(norm_mix), r3(norm_ffn)
    lru_conv_b3 = r3(lru_conv_b)
    lru_b_a4, lru_b_x4, lru_lam4 = r4(lru_b_a), r4(lru_b_x), r4(lru_lam)
    q_norm3, kv_norm3 = r3(mla_q_norm), r3(mla_kv_norm)
    ffn_conv_b3 = r3(ffn_conv_b)
    norm_final2 = norm_final.reshape(1, d)

    xall = jnp.concatenate([x.reshape(bn, d), ctx.reshape(bc, d)], axis=0)
    for l in range(depth):
        last = l == depth - 1
        n_rows = bn if last else bn + bc
        sh1, sc1, g1, sh2, sc2, g2 = (mod[l, :, k] for k in range(N_MOD))
        z = _in_proj(xall, norm_mix3, sh1, sc1, w_in_b, w_gate_b, l, dims)
        out_a = _na_attention(z, na_bias, l, dims)
        lru_args = (lru_conv_w, lru_conv_b3, lru_w_a_b, lru_b_a4, lru_w_x_b, lru_b_x4, lru_lam4)
        lru_fwd = _lru_scan(z, l, 0, *lru_args, None, dims)
        out_b = _lru_scan(z, l, 1, *lru_args, lru_fwd, dims)
        q_m, k_m, v_m = _mla_proj(z, l, q_norm3, kv_norm3, wq_p, wkv_p, ct, st, dims)
        kt_m = k_m.reshape(bn + bc, mla_heads, MLA_HEAD_PAD).transpose(1, 2, 0)
        out_c = _mla_attention(q_m, kt_m, v_m, dims)
        if not last:
            out_a = _ctx_attention(z, z, z, out_a, dims, na_heads, NA_HEAD_DIM, 0, na_heads, 2 * na_heads, 1,
                                   NA_HEAD_DIM ** -0.5, "na_ctx_attention")
            out_c = _ctx_attention(q_m, k_m, v_m, out_c, dims, mla_heads, MLA_HEAD_PAD, 0, 0, 0, 2, 1.0,
                                   "mla_ctx_attention")
        y = _merge(out_a, out_b, out_c, w_branch_b, z, l, n_rows, dims)
        xall = _out_proj(y, w_out_b, xall, g1, l, dims)
        ffn_args = (norm_ffn3, sh2, sc2, g2, w_up_b, ffn_conv_w, ffn_conv_b3, w_down_b, norm_final2, l)
        x_new = _ffn(xall, None, *ffn_args, 0, bn, n, n_rows, last, dims)
        if not last:
            x_new = _ffn(xall, x_new, *ffn_args, bn, bc, ctx_len, n_rows, False, dims)
        xall = x_new
    return xall.reshape(b, n, d)
```
